```python
import math
import jax, jax.numpy as jnp
from jax import lax
import numpy as np

D_MODEL = 4096
BATCH = 2
SEQ = 8192
DEPTH = 2

BLOCK = 128
EPS = 1e-6
ROPE_THETA = 10000.0
A_WIDTH = D_MODEL // 2
A_GROUPS = 8
A_CHUNK = 128
B_HEAD_DIM = 128
B_HEADS = (D_MODEL // 2) // B_HEAD_DIM
B_WIDTH = B_HEADS * B_HEAD_DIM
IDX_HEADS = 16
IDX_DIM = 64
TOPK_MAX = 256
C_HEAD_DIM = 128
C_HEADS = D_MODEL // C_HEAD_DIM
C_WIDTH = C_HEADS * C_HEAD_DIM
N_EXPERTS = 64
EXPERT_DIM = 256
SHARED_DIM = 256
TOP_K = 8
N_GROUPS = 8
TOPK_GROUPS = 4
ROUTED_SCALE = 2.5
MOE_BLOCK = 128
AB_SPLITS = (A_WIDTH, A_WIDTH, B_WIDTH, B_HEAD_DIM, B_HEAD_DIM, IDX_HEADS * IDX_DIM, IDX_DIM, IDX_HEADS)
AB_IN_WIDTH = sum(AB_SPLITS)

kernel_name = "hybrid_gmlp_dsa_stickbreak_moe_adaln"


def rms_norm(x, g):
    xf = x.astype(jnp.float32)
    y = xf * lax.rsqrt(jnp.mean(xf * xf, axis=-1, keepdims=True) + EPS)
    return (y * g.astype(jnp.float32)).astype(x.dtype)


def layer_norm(x, g, b):
    xf = x.astype(jnp.float32)
    mu = jnp.mean(xf, axis=-1, keepdims=True)
    var = jnp.mean(jnp.square(xf - mu), axis=-1, keepdims=True)
    y = (xf - mu) * lax.rsqrt(var + EPS)
    return (y * g.astype(jnp.float32) + b.astype(jnp.float32)).astype(x.dtype)


def rope_tables(positions, dim):
    inv = ROPE_THETA ** (-jnp.arange(0, dim, 2, dtype=jnp.float32) / dim)
    ang = positions.astype(jnp.float32)[..., None] * inv
    return jnp.cos(ang)[:, :, None, :], jnp.sin(ang)[:, :, None, :]


def apply_rope(x, cos, sin):
    xf = x.astype(jnp.float32)
    x1, x2 = jnp.split(xf, 2, axis=-1)
    out = jnp.concatenate([x1 * cos - x2 * sin, x2 * cos + x1 * sin], axis=-1)
    return out.astype(x.dtype)


def ada_modulation(c, w, b):
    m = jax.nn.silu(c) @ w + b
    shift, scale, gate = jnp.split(m, 3, axis=-1)
    return shift[:, None, :], scale[:, None, :], gate[:, None, :]


def chunked_spatial_gating(u, v, ln_g, ln_b, w_s, b_s):
    bsz, seq, _ = v.shape
    v = layer_norm(v, ln_g, ln_b)
    n_chunks = seq // A_CHUNK
    gw = A_WIDTH // A_GROUPS
    v = v.reshape(bsz, n_chunks, A_CHUNK, A_GROUPS, gw)
    causal = jnp.tril(jnp.ones((A_CHUNK, A_CHUNK), dtype=bool))
    w = jnp.where(causal[None], w_s, 0.0).astype(v.dtype)
    mixed = jnp.einsum('gts,bnsgc->bntgc', w, v) + b_s.T[None, None, :, :, None].astype(v.dtype)
    return u * mixed.reshape(bsz, seq, A_WIDTH)


def indexer_sparse_attention(q, k, v, q_idx, k_idx, w_idx):
    bsz, seq, heads, dh = q.shape
    n_sel = min(TOPK_MAX, seq // 4)
    n_blocks = seq // BLOCK
    key_pos = jnp.arange(seq)
    w_idx = w_idx.astype(jnp.float32) * (IDX_HEADS ** -0.5 * IDX_DIM ** -0.5)
    gather = jax.vmap(lambda table, idx: table[idx])

    def block(i):
        start = i * BLOCK
        qb = lax.dynamic_slice_in_dim(q, start, BLOCK, axis=1)
        qib = lax.dynamic_slice_in_dim(q_idx, start, BLOCK, axis=1)
        wb = lax.dynamic_slice_in_dim(w_idx, start, BLOCK, axis=1)
        q_pos = start + jnp.arange(BLOCK)
        causal = key_pos[None, :] <= q_pos[:, None]
        rel = jax.nn.relu(jnp.einsum('bthd,bsd->bths', qib, k_idx).astype(jnp.float32))
        score = jnp.einsum('bth,bths->bts', wb, rel)
        score = jnp.where(causal[None], score, -jnp.inf)
        _, sel = lax.top_k(score, n_sel)
        k_sel = gather(k, sel)
        v_sel = gather(v, sel)
        logits = jnp.einsum('bthd,btkd->bhtk', qb, k_sel).astype(jnp.float32) * (dh ** -0.5)
        valid = sel <= q_pos[None, :, None]
        logits = jnp.where(valid[:, None], logits, -jnp.inf)
        p = jax.nn.softmax(logits, axis=-1)
        return jnp.einsum('bhtk,btkd->bthd', p.astype(v.dtype), v_sel)

    out = lax.map(block, jnp.arange(n_blocks))
    return out.transpose(1, 0, 2, 3, 4).reshape(bsz, seq, heads * dh)


def stick_breaking_attention(q, k, v):
    bsz, seq, heads, dh = q.shape
    n_blocks = seq // BLOCK
    key_pos = jnp.arange(seq)

    def block(i):
        start = i * BLOCK
        qb = lax.dynamic_slice_in_dim(q, start, BLOCK, axis=1)
        q_pos = start + jnp.arange(BLOCK)
        strict = (key_pos[None, :] < q_pos[:, None])[None, None]
        z = jnp.einsum('bthd,bshd->bhts', qb, k).astype(jnp.float32) * (dh ** -0.5)
        log_beta = jax.nn.log_sigmoid(z)
        log_fail = jnp.where(strict, jax.nn.log_sigmoid(-z), 0.0)
        between = lax.cumsum(log_fail, axis=3, reverse=True) - log_fail
        a = jnp.where(strict, jnp.exp(log_beta + between), 0.0)
        return jnp.einsum('bhts,bshd->bthd', a.astype(v.dtype), v)

    out = lax.map(block, jnp.arange(n_blocks))
    return out.transpose(1, 0, 2, 3, 4).reshape(bsz, seq, heads * dh)


def mixer_gmlp_dsa(h, rope128, rope64, w_in, w_out, ln_g, ln_b, w_s, b_s):
    bsz, seq, _ = h.shape
    proj = h @ w_in
    offsets = []
    acc = 0
    for width in AB_SPLITS[:-1]:
        acc += width
        offsets.append(acc)
    u, v, q, k, vb, qi, ki, wi = jnp.split(proj, offsets, axis=-1)
    a_out = chunked_spatial_gating(jax.nn.gelu(u, approximate=False), jax.nn.gelu(v, approximate=False),
                                   ln_g, ln_b, w_s, b_s)
    cos_b, sin_b = rope128
    cos_i, sin_i = rope64
    q = apply_rope(q.reshape(bsz, seq, B_HEADS, B_HEAD_DIM), cos_b, sin_b)
    k = apply_rope(k[:, :, None, :], cos_b, sin_b)[:, :, 0, :]
    qi = apply_rope(qi.reshape(bsz, seq, IDX_HEADS, IDX_DIM), cos_i, sin_i)
    ki = apply_rope(ki[:, :, None, :], cos_i, sin_i)[:, :, 0, :]
    b_out = indexer_sparse_attention(q, k, vb, qi, ki, wi)
    return jnp.concatenate([a_out, b_out], axis=-1) @ w_out


def mixer_stick_breaking(h, w_qkv, w_out):
    bsz, seq, _ = h.shape
    q, k, v = jnp.split(h @ w_qkv, 3, axis=-1)
    shp = (bsz, seq, C_HEADS, C_HEAD_DIM)
    o = stick_breaking_attention(q.reshape(shp), k.reshape(shp), v.reshape(shp))
    return o @ w_out


def moe_ffn(h, w_router, router_bias, w_gate, w_up, w_down, ws_gate, ws_up, ws_down):
    bsz, seq, dm = h.shape
    hf = h.reshape(-1, dm)
    n_tok = hf.shape[0]
    scores = jax.nn.sigmoid(hf.astype(jnp.float32) @ w_router.astype(jnp.float32))
    choice = scores + router_bias.astype(jnp.float32)
    per_group = N_EXPERTS // N_GROUPS
    group_score = lax.top_k(choice.reshape(n_tok, N_GROUPS, per_group), 2)[0].sum(-1)
    _, top_groups = lax.top_k(group_score, TOPK_GROUPS)
    group_mask = jnp.sum(jax.nn.one_hot(top_groups, N_GROUPS, dtype=jnp.float32), axis=1)
    expert_mask = jnp.repeat(group_mask, per_group, axis=1) > 0
    _, top_e = lax.top_k(jnp.where(expert_mask, choice, -jnp.inf), TOP_K)
    top_w = jnp.take_along_axis(scores, top_e, axis=1)
    top_w = top_w / jnp.sum(top_w, axis=-1, keepdims=True) * ROUTED_SCALE
    gates = jnp.einsum('nk,nke->ne', top_w, jax.nn.one_hot(top_e, N_EXPERTS, dtype=jnp.float32))
    n_blocks = n_tok // MOE_BLOCK

    def block(args):
        xb, gb = args
        g = jnp.einsum('nd,edf->nef', xb, w_gate)
        u = jnp.einsum('nd,edf->nef', xb, w_up)
        act = jax.nn.silu(g) * u * gb[..., None].astype(xb.dtype)
        return jnp.einsum('nef,efd->nd', act, w_down)

    routed = lax.map(block, (hf.reshape(n_blocks, MOE_BLOCK, dm),
                             gates.reshape(n_blocks, MOE_BLOCK, N_EXPERTS)))
    shared = (jax.nn.silu(hf @ ws_gate) * (hf @ ws_up)) @ ws_down
    return (routed.reshape(n_tok, dm) + shared).reshape(bsz, seq, dm)


def setup_inputs(seed: int = 0) -> dict:
    key = jax.random.key(seed)
    ks = iter(jax.random.split(key, 40))

    def nrm(shape, scale):
        return jax.random.normal(next(ks), shape, jnp.float32) * scale

    n_even = (DEPTH + 1) // 2
    n_odd = DEPTH // 2
    D = D_MODEL
    x = nrm((BATCH, SEQ, D), 1.0)
    c = nrm((BATCH, D), 1.0)
    offset = jax.random.randint(next(ks), (BATCH, 1), 0, 1024, dtype=jnp.int32)
    positions = offset + jnp.arange(SEQ, dtype=jnp.int32)[None, :]
    return {
        'x': x,
        'c': c,
        'positions': positions,
        'ab_w_in': nrm((n_even, D, AB_IN_WIDTH), D ** -0.5),
        'ab_w_out': nrm((n_even, A_WIDTH + B_WIDTH, D), (A_WIDTH + B_WIDTH) ** -0.5),
        'gmlp_ln_g': 1.0 + nrm((n_even, A_WIDTH), 0.05),
        'gmlp_ln_b': nrm((n_even, A_WIDTH), 0.02),
        'gmlp_w_s': nrm((n_even, A_GROUPS, A_CHUNK, A_CHUNK), 0.5 * A_CHUNK ** -0.5),
        'gmlp_b_s': 1.0 + nrm((n_even, A_GROUPS, A_CHUNK), 0.1),
        'sb_w_qkv': nrm((n_odd, D, 3 * C_WIDTH), D ** -0.5),
        'sb_w_out': nrm((n_odd, C_WIDTH, D), C_WIDTH ** -0.5),
        'norm_mix_g': 1.0 + nrm((DEPTH, D), 0.05),
        'ada_mix_w': nrm((DEPTH, D, 3 * D), 0.5 * D ** -0.5),
        'ada_mix_b': nrm((DEPTH, 3 * D), 0.02),
        'norm_ffn_g': 1.0 + nrm((DEPTH, D), 0.05),
        'ada_ffn_w': nrm((DEPTH, D, 3 * D), 0.5 * D ** -0.5),
        'ada_ffn_b': nrm((DEPTH, 3 * D), 0.02),
        'router_w': nrm((DEPTH, D, N_EXPERTS), D ** -0.5),
        'router_bias': nrm((DEPTH, N_EXPERTS), 0.01),
        'expert_w_gate': nrm((DEPTH, N_EXPERTS, D, EXPERT_DIM), D ** -0.5),
        'expert_w_up': nrm((DEPTH, N_EXPERTS, D, EXPERT_DIM), D ** -0.5),
        'expert_w_down': nrm((DEPTH, N_EXPERTS, EXPERT_DIM, D), EXPERT_DIM ** -0.5),
        'shared_w_gate': nrm((DEPTH, D, SHARED_DIM), D ** -0.5),
        'shared_w_up': nrm((DEPTH, D, SHARED_DIM), D ** -0.5),
        'shared_w_down': nrm((DEPTH, SHARED_DIM, D), SHARED_DIM ** -0.5),
        'final_norm_g': 1.0 + nrm((D,), 0.05),
    }


def reference(x, c, positions, ab_w_in, ab_w_out, gmlp_ln_g, gmlp_ln_b, gmlp_w_s, gmlp_b_s,
              sb_w_qkv, sb_w_out, norm_mix_g, ada_mix_w, ada_mix_b, norm_ffn_g, ada_ffn_w, ada_ffn_b,
              router_w, router_bias, expert_w_gate, expert_w_up, expert_w_down,
              shared_w_gate, shared_w_up, shared_w_down, final_norm_g):
    rope128 = rope_tables(positions, B_HEAD_DIM)
    rope64 = rope_tables(positions, IDX_DIM)
    for layer in range(DEPTH):
        j = layer // 2
        shift, scale, gate = ada_modulation(c, ada_mix_w[layer], ada_mix_b[layer])
        h = rms_norm(x, norm_mix_g[layer]) * (1.0 + scale) + shift
        if layer % 2 == 0:
            y = mixer_gmlp_dsa(h, rope128, rope64, ab_w_in[j], ab_w_out[j], gmlp_ln_g[j], gmlp_ln_b[j],
                               gmlp_w_s[j], gmlp_b_s[j])
        else:
            y = mixer_stick_breaking(h, sb_w_qkv[j], sb_w_out[j])
        x = x + gate * y
        shift, scale, gate = ada_modulation(c, ada_ffn_w[layer], ada_ffn_b[layer])
        h = rms_norm(x, norm_ffn_g[layer]) * (1.0 + scale) + shift
        x = x + gate * moe_ffn(h, router_w[layer], router_bias[layer], expert_w_gate[layer],
                               expert_w_up[layer], expert_w_down[layer], shared_w_gate[layer],
                               shared_w_up[layer], shared_w_down[layer])
    return rms_norm(x, final_norm_g)
```

```python
import functools

import jax
import jax.numpy as jnp
from jax import lax
from jax.experimental import pallas as pl
from jax.experimental.pallas import tpu as pltpu

F32 = jnp.float32
BF16 = jnp.bfloat16
I32 = jnp.int32

EPS = 1e-6
ROPE_THETA = 10000.0
LANES = 128
HEAD_DIM = 128
A_GROUPS = 8
A_CHUNK = 128
IDX_HEADS = 16
IDX_DIM = 64
TOPK_MAX = 256
N_GROUPS = 8
TOPK_GROUPS = 4
TOP_K = 8
ROUTED_SCALE = 2.5
VMEM_LIMIT = 56 * 1024 * 1024
INT_MIN = -(2 ** 31)
MASKED = -1e30


def _params(*sem):
    return pltpu.CompilerParams(dimension_semantics=sem, vmem_limit_bytes=VMEM_LIMIT)


def _mod_kernel(c_ref, w_ref, b_ref, o_ref):
    c = c_ref[...]
    s = c * jax.nn.sigmoid(c)
    o_ref[...] = jnp.dot(s.astype(BF16), w_ref[...].astype(BF16), preferred_element_type=F32) + b_ref[...]


def ada_modulation_all(c, w, b):
    bsz, d = c.shape
    n_layers, _, n_out = w.shape
    rows = 8
    c_pad = jnp.zeros((rows, d), F32).at[:bsz].set(c)
    tn = min(512, n_out)
    out = pl.pallas_call(
        _mod_kernel,
        grid=(n_layers, n_out // tn),
        in_specs=[pl.BlockSpec((rows, d), lambda l, j: (0, 0)),
                  pl.BlockSpec((None, d, tn), lambda l, j: (l, 0, j)),
                  pl.BlockSpec((None, 1, tn), lambda l, j: (l, 0, j))],
        out_specs=pl.BlockSpec((None, rows, tn), lambda l, j: (l, 0, j)),
        out_shape=jax.ShapeDtypeStruct((n_layers, rows, n_out), F32),
        compiler_params=_params("parallel", "parallel"),
    )(c_pad, w, b.reshape(n_layers, 1, n_out))
    return out[:, :bsz].reshape(n_layers, bsz, 3, 1, d)


def _norm_mod_kernel(x_ref, g_ref, sc_ref, sh_ref, o_ref):
    x = x_ref[...]
    y = x * lax.rsqrt(jnp.mean(x * x, axis=-1, keepdims=True) + EPS)
    o_ref[...] = ((y * g_ref[...]) * (1.0 + sc_ref[...]) + sh_ref[...]).astype(o_ref.dtype)


def norm_modulate(x2, g, scale, shift, seq):
    n, d = x2.shape
    tm = min(256, seq)
    per_batch = seq // tm
    return pl.pallas_call(
        _norm_mod_kernel,
        grid=(n // tm,),
        in_specs=[pl.BlockSpec((tm, d), lambda i: (i, 0)),
                  pl.BlockSpec((1, d), lambda i: (0, 0)),
                  pl.BlockSpec((None, 1, d), lambda i: (i // per_batch, 0, 0)),
                  pl.BlockSpec((None, 1, d), lambda i: (i // per_batch, 0, 0))],
        out_specs=pl.BlockSpec((tm, d), lambda i: (i, 0)),
        out_shape=jax.ShapeDtypeStruct((n, d), BF16),
        compiler_params=_params("parallel"),
    )(x2, g.reshape(1, d), scale, shift)


def _final_norm_kernel(x_ref, g_ref, o_ref):
    x = x_ref[...]
    y = x * lax.rsqrt(jnp.mean(x * x, axis=-1, keepdims=True) + EPS)
    o_ref[...] = y * g_ref[...]


def final_norm(x2, g):
    n, d = x2.shape
    tm = min(256, n)
    return pl.pallas_call(
        _final_norm_kernel,
        grid=(n // tm,),
        in_specs=[pl.BlockSpec((tm, d), lambda i: (i, 0)), pl.BlockSpec((1, d), lambda i: (0, 0))],
        out_specs=pl.BlockSpec((tm, d), lambda i: (i, 0)),
        out_shape=jax.ShapeDtypeStruct((n, d), F32),
        compiler_params=_params("parallel"),
    )(x2, g.reshape(1, d))


def _mm_kernel(a_ref, w_ref, o_ref):
    o_ref[...] = jnp.dot(a_ref[...], w_ref[...], preferred_element_type=F32).astype(o_ref.dtype)


def matmul(a, w, out_dtype, tm=1024, tn=512):
    m, k = a.shape
    n = w.shape[1]
    tm, tn = min(tm, m), min(tn, n)
    return pl.pallas_call(
        _mm_kernel,
        grid=(m // tm, n // tn),
        in_specs=[pl.BlockSpec((tm, k), lambda i, j: (i, 0)),
                  pl.BlockSpec((k, tn), lambda i, j: (0, j))],
        out_specs=pl.BlockSpec((tm, tn), lambda i, j: (i, j)),
        out_shape=jax.ShapeDtypeStruct((m, n), out_dtype),
        compiler_params=_params("parallel", "parallel"),
    )(a, w)


def _mm_res_kernel(a_ref, w_ref, r_ref, g_ref, o_ref):
    acc = jnp.dot(a_ref[...], w_ref[...], preferred_element_type=F32)
    o_ref[...] = r_ref[...] + g_ref[...] * acc


def matmul_residual(a, w, res, gate, seq, tm=1024, tn=512):
    m, k = a.shape
    n = w.shape[1]
    tm, tn = min(tm, seq), min(tn, n)
    per_batch = seq // tm
    return pl.pallas_call(
        _mm_res_kernel,
        grid=(m // tm, n // tn),
        in_specs=[pl.BlockSpec((tm, k), lambda i, j: (i, 0)),
                  pl.BlockSpec((k, tn), lambda i, j: (0, j)),
                  pl.BlockSpec((tm, tn), lambda i, j: (i, j)),
                  pl.BlockSpec((None, 1, tn), lambda i, j: (i // per_batch, 0, j))],
        out_specs=pl.BlockSpec((tm, tn), lambda i, j: (i, j)),
        out_shape=jax.ShapeDtypeStruct((m, n), F32),
        compiler_params=_params("parallel", "parallel"),
    )(a, w, res, gate)


def _gelu(x):
    return 0.5 * x * (1.0 + lax.erf(x * (2.0 ** -0.5)))


def _gmlp_kernel(u_ref, v_ref, lng_ref, lnb_ref, ws_ref, bs_ref, o_ref, *, groups):
    u = _gelu(u_ref[...].astype(F32))
    v = _gelu(v_ref[...].astype(F32))
    mu = jnp.mean(v, axis=-1, keepdims=True)
    var = jnp.mean(jnp.square(v - mu), axis=-1, keepdims=True)
    vn = ((v - mu) * lax.rsqrt(var + EPS) * lng_ref[...] + lnb_ref[...]).astype(BF16)
    t = ws_ref.shape[1]
    gw = u.shape[1] // groups
    row = lax.broadcasted_iota(I32, (t, t), 0)
    col = lax.broadcasted_iota(I32, (t, t), 1)
    causal = col <= row
    for g in range(groups):
        w = jnp.where(causal, ws_ref[g], 0.0).astype(BF16)
        mixed = jnp.dot(w, vn[:, g * gw:(g + 1) * gw], preferred_element_type=F32) + bs_ref[:, g:g + 1]
        o_ref[:, g * gw:(g + 1) * gw] = (u[:, g * gw:(g + 1) * gw] * mixed).astype(o_ref.dtype)


def gmlp_mixer(proj, a_width, ln_g, ln_b, w_s, b_s, out_width):
    n = proj.shape[0]
    groups, t, _ = w_s.shape
    return pl.pallas_call(
        functools.partial(_gmlp_kernel, groups=groups),
        grid=(n // t,),
        in_specs=[pl.BlockSpec((t, a_width), lambda i: (i, 0)),
                  pl.BlockSpec((t, a_width), lambda i: (i, 1)),
                  pl.BlockSpec((1, a_width), lambda i: (0, 0)),
                  pl.BlockSpec((1, a_width), lambda i: (0, 0)),
                  pl.BlockSpec((groups, t, t), lambda i: (0, 0, 0)),
                  pl.BlockSpec((t, groups), lambda i: (0, 0))],
        out_specs=pl.BlockSpec((t, a_width), lambda i: (i, 0)),
        out_shape=jax.ShapeDtypeStruct((n, out_width), BF16),
        compiler_params=_params("parallel"),
    )(proj, proj, ln_g.reshape(1, a_width), ln_b.reshape(1, a_width), w_s, b_s.T)


def _rope_kernel(q_ref, k_ref, x_ref, cb_ref, sb_ref, ci_ref, si_ref, ck_ref, sk_ref,
                 qo_ref, ko_ref, qio_ref, kia_ref, kib_ref, wo_ref, *, heads, idx_pairs, q_scale, w_scale):
    cb, sb = cb_ref[...], sb_ref[...]
    half = HEAD_DIM // 2
    for h in range(heads):
        x = q_ref[:, h * HEAD_DIM:(h + 1) * HEAD_DIM].astype(F32)
        r = x * cb + pltpu.roll(x, half, 1) * sb
        qo_ref[:, h * HEAD_DIM:(h + 1) * HEAD_DIM] = (r * q_scale).astype(qo_ref.dtype)
    x = k_ref[...].astype(F32)
    ko_ref[...] = (x * cb + pltpu.roll(x, half, 1) * sb).astype(ko_ref.dtype)

    lane = lax.broadcasted_iota(I32, cb.shape, 1)
    first_half = (lane % IDX_DIM) < (IDX_DIM // 2)

    def rope_idx(x, c, s):
        rot = jnp.where(first_half, pltpu.roll(x, LANES - IDX_DIM // 2, 1), pltpu.roll(x, IDX_DIM // 2, 1))
        return x * c + rot * s

    ci, si = ci_ref[...], si_ref[...]
    for p in range(idx_pairs):
        x = x_ref[:, p * LANES:(p + 1) * LANES]
        qio_ref[:, p * LANES:(p + 1) * LANES] = rope_idx(x, ci, si).astype(qio_ref.dtype)
    tail = x_ref[:, idx_pairs * LANES:(idx_pairs + 1) * LANES]
    roped = rope_idx(tail, ck_ref[...], sk_ref[...])
    is_k = lane < IDX_DIM
    ka = jnp.where(is_k, roped, 0.0)
    kia_ref[...] = ka.astype(kia_ref.dtype)
    kib_ref[...] = pltpu.roll(ka, IDX_DIM, 1).astype(kib_ref.dtype)
    w = pltpu.roll(tail, LANES - IDX_DIM, 1)
    wo_ref[...] = jnp.where(lane < IDX_HEADS, w * w_scale, 0.0)


def rope_prepare(proj, proj_idx, tables, q_off, heads):
    n = proj.shape[0]
    tm = min(256, n)
    idx_pairs = IDX_HEADS * IDX_DIM // LANES
    qb = q_off // (heads * HEAD_DIM)
    kb = (q_off + heads * HEAD_DIM) // HEAD_DIM
    row = lambda w: pl.BlockSpec((tm, w), lambda i: (i, 0))
    kernel = functools.partial(_rope_kernel, heads=heads, idx_pairs=idx_pairs, q_scale=HEAD_DIM ** -0.5,
                               w_scale=IDX_HEADS ** -0.5 * IDX_DIM ** -0.5)
    return pl.pallas_call(
        kernel,
        grid=(n // tm,),
        in_specs=[pl.BlockSpec((tm, heads * HEAD_DIM), lambda i: (i, qb)),
                  pl.BlockSpec((tm, HEAD_DIM), lambda i: (i, kb)),
                  row(proj_idx.shape[1])] + [row(LANES)] * 6,
        out_specs=[row(heads * HEAD_DIM), row(HEAD_DIM), row(idx_pairs * LANES), row(LANES), row(LANES), row(LANES)],
        out_shape=[jax.ShapeDtypeStruct((n, heads * HEAD_DIM), BF16),
                   jax.ShapeDtypeStruct((n, HEAD_DIM), BF16),
                   jax.ShapeDtypeStruct((n, idx_pairs * LANES), BF16),
                   jax.ShapeDtypeStruct((n, LANES), BF16),
                   jax.ShapeDtypeStruct((n, LANES), BF16),
                   jax.ShapeDtypeStruct((n, LANES), F32)],
        compiler_params=_params("parallel"),
    )(proj, proj, proj_idx, *tables)


def rope_tables(positions):
    pos = positions.reshape(-1).astype(F32)[:, None]

    def cs(dim):
        inv = ROPE_THETA ** (-jnp.arange(0, dim, 2, dtype=F32) / dim)
        ang = pos * inv
        return jnp.cos(ang), jnp.sin(ang)

    cb, sb = cs(HEAD_DIM)
    ci, si = cs(IDX_DIM)
    ones, zeros = jnp.ones_like(cb), jnp.zeros_like(cb)
    return (jnp.concatenate([cb, cb], 1), jnp.concatenate([-sb, sb], 1),
            jnp.concatenate([ci, ci, ci, ci], 1), jnp.concatenate([-si, si, -si, si], 1),
            jnp.concatenate([ci, ci, ones], 1), jnp.concatenate([-si, si, zeros], 1))


def _sort_key(x):
    x = jnp.where(x == 0.0, 0.0, x)
    bits = pltpu.bitcast(x, I32)
    return jnp.where(bits < 0, bits ^ 0x7FFFFFFF, bits)


def _dsa_kernel(qi_ref, w_ref, kia_ref, kib_ref, q_ref, k_ref, v_ref, buf_ref, o_ref, key_ref,
                *, tq, tk, heads, n_sel):
    del buf_ref
    qt = pl.program_id(1)
    q_lo = qt * tq
    n_kb = (q_lo + tq + tk - 1) // tk
    sub = tk // LANES
    row_pos = q_lo + lax.broadcasted_iota(I32, (tq, LANES), 0)
    lane = lax.broadcasted_iota(I32, (tq, LANES), 1)

    w = w_ref[...]
    w_cols = [jnp.broadcast_to(w[:, h:h + 1], (tq, tk)) for h in range(IDX_HEADS)]

    def score_block(kb, carry):
        k0 = pl.multiple_of(kb * tk, tk)
        ka = kia_ref[pl.ds(k0, tk), :]
        kb_ = kib_ref[pl.ds(k0, tk), :]
        acc = jnp.zeros((tq, tk), F32)
        for p in range(IDX_HEADS // 2):
            x = qi_ref[:, p * LANES:(p + 1) * LANES]
            ra = lax.dot_general(x, ka, (((1,), (1,)), ((), ())), preferred_element_type=F32)
            rb = lax.dot_general(x, kb_, (((1,), (1,)), ((), ())), preferred_element_type=F32)
            acc = acc + w_cols[2 * p] * jnp.maximum(ra, 0.0)
            acc = acc + w_cols[2 * p + 1] * jnp.maximum(rb, 0.0)
        key = _sort_key(acc)
        for j in range(sub):
            col_pos = k0 + j * LANES + lane
            key_ref[kb * sub + j] = jnp.where(col_pos <= row_pos, key[:, j * LANES:(j + 1) * LANES], INT_MIN)
        return carry

    lax.fori_loop(0, n_kb, score_block, 0)
    n_slabs = n_kb * sub

    def count(pred):
        def body(c, acc):
            return acc + jnp.where(pred(key_ref[c], c), 1, 0)
        part = lax.fori_loop(0, n_slabs, body, jnp.zeros((tq, LANES), I32))
        return jnp.sum(part.astype(F32), axis=1, keepdims=True).astype(I32)

    def search_bit(i, u):
        cand = u | (1 << (31 - i))
        thr = jnp.broadcast_to(cand ^ INT_MIN, (tq, LANES))
        total = count(lambda kv, c: kv >= thr)
        return jnp.where(total >= n_sel, cand, u)

    u = lax.fori_loop(0, 32, search_bit, jnp.zeros((tq, 1), I32))
    thr = u ^ INT_MIN
    thr_b = jnp.broadcast_to(thr, (tq, LANES))
    n_gt = count(lambda kv, c: kv > thr_b)
    n_ge = count(lambda kv, c: kv >= thr_b)
    need = n_sel - n_gt
    tie_rows = jnp.logical_and(n_ge > n_sel, need > 0)

    seq_bits = max(1, (key_ref.shape[0] * LANES).bit_length())

    def tie_search():
        need_b = need

        def bit_step(i, j):
            cand = j | (1 << (seq_bits - 1 - i))
            cand_b = jnp.broadcast_to(cand, (tq, LANES))
            total = count(lambda kv, c: jnp.logical_and(kv == thr_b, c * LANES + lane < cand_b))
            return jnp.where(total <= need_b, cand, j)

        return lax.fori_loop(0, seq_bits, bit_step, jnp.zeros((tq, 1), I32))

    any_tie = jnp.max(jnp.where(tie_rows, 1.0, 0.0)) > 0.0
    bound = lax.cond(any_tie, tie_search, lambda: jnp.full((tq, 1), 2 ** 30, I32))
    bound = jnp.where(tie_rows, bound, 2 ** 30)
    bound_b = jnp.broadcast_to(bound, (tq, LANES))

    q_all = jnp.concatenate([q_ref[:, h * HEAD_DIM:(h + 1) * HEAD_DIM] for h in range(heads)], axis=0)

    def attn_block(kb, carry):
        m, l, acc = carry
        k0 = pl.multiple_of(kb * tk, tk)
        kk = k_ref[pl.ds(k0, tk), :]
        vv = v_ref[pl.ds(k0, tk), :]
        s = lax.dot_general(q_all, kk, (((1,), (1,)), ((), ())), preferred_element_type=F32)
        bias_cols = []
        for j in range(sub):
            kv = key_ref[kb * sub + j]
            col_pos = k0 + j * LANES + lane
            take = jnp.logical_or(kv > thr_b, jnp.logical_and(kv == thr_b, col_pos < bound_b))
            take = jnp.logical_and(take, col_pos <= row_pos)
            bias_cols.append(jnp.where(take, 0.0, MASKED))
        bias = jnp.concatenate(bias_cols, axis=1)
        s = s.reshape(heads, tq, tk) + bias[None]
        m_new = jnp.maximum(m, jnp.max(s, axis=-1, keepdims=True))
        alpha = jnp.exp(m - m_new)
        p = jnp.exp(s - m_new)
        l = alpha * l + jnp.sum(p, axis=-1, keepdims=True)
        pv = jnp.dot(p.reshape(heads * tq, tk).astype(BF16), vv, preferred_element_type=F32)
        acc = alpha * acc + pv.reshape(heads, tq, HEAD_DIM)
        return m_new, l, acc

    init = (jnp.full((heads, tq, 1), MASKED, F32), jnp.zeros((heads, tq, 1), F32),
            jnp.zeros((heads, tq, HEAD_DIM), F32))
    m, l, acc = lax.fori_loop(0, n_kb, attn_block, init)
    out = acc / l
    for h in range(heads):
        o_ref[:, h * HEAD_DIM:(h + 1) * HEAD_DIM] = out[h].astype(o_ref.dtype)


def dsa_mixer(qi, w_idx, kia, kib, q, k, proj, v_col_block, buf, seq, n_sel):
    n = q.shape[0]
    bsz = n // seq
    heads = q.shape[1] // HEAD_DIM
    tq = min(128, seq)
    tk = min(512, seq)
    nq = seq // tq
    kernel = functools.partial(_dsa_kernel, tq=tq, tk=tk, heads=heads, n_sel=n_sel)
    qrow = lambda w: pl.BlockSpec((tq, w), lambda b, i: (b * nq + i, 0))
    kv = lambda c: pl.BlockSpec((seq, LANES), lambda b, i: (b, c))
    return pl.pallas_call(
        kernel,
        grid=(bsz, nq),
        in_specs=[qrow(qi.shape[1]), qrow(LANES), kv(0), kv(0), qrow(q.shape[1]), kv(0), kv(v_col_block),
                  pl.BlockSpec(memory_space=pl.ANY)],
        out_specs=pl.BlockSpec((tq, heads * HEAD_DIM), lambda b, i: (b * nq + i, 1)),
        out_shape=jax.ShapeDtypeStruct(buf.shape, buf.dtype),
        scratch_shapes=[pltpu.VMEM((seq // LANES, tq, LANES), I32)],
        input_output_aliases={7: 0},
        compiler_params=_params("parallel", "arbitrary"),
    )(qi, w_idx, kia, kib, q, k, proj, buf)


def _sb_kernel(q_ref, k_ref, v_ref, o_ref, *, t, scale):
    qt = pl.program_id(2)
    q = q_ref[...]
    row = lax.broadcasted_iota(I32, (t, t), 0)
    col = lax.broadcasted_iota(I32, (t, t), 1)
    later = jnp.where(row > col, 1.0, 0.0).astype(BF16)
    strict = col < row

    def block(kb, masked, acc, run):
        k0 = pl.multiple_of(kb * t, t)
        z = lax.dot_general(q, k_ref[pl.ds(k0, t), :], (((1,), (1,)), ((), ())),
                            preferred_element_type=F32) * scale
        log_beta = jnp.minimum(z, 0.0) - jnp.log1p(jnp.exp(-jnp.abs(z)))
        log_fail = log_beta - z
        if masked:
            log_fail = jnp.where(strict, log_fail, 0.0)
        between = jnp.dot(log_fail.astype(BF16), later, preferred_element_type=F32) + run
        a = jnp.exp(log_beta + between)
        if masked:
            a = jnp.where(strict, a, 0.0)
        acc = acc + jnp.dot(a.astype(BF16), v_ref[pl.ds(k0, t), :], preferred_element_type=F32)
        run = run + jnp.sum(log_fail, axis=-1, keepdims=True)
        return acc, run

    acc, run = block(qt, True, jnp.zeros((t, HEAD_DIM), F32), jnp.zeros((t, 1), F32))

    def body(i, carry):
        return block(qt - 1 - i, False, *carry)

    acc, run = lax.fori_loop(0, qt, body, (acc, run))
    o_ref[...] = acc.astype(o_ref.dtype)


def stick_breaking_mixer(qkv, seq, heads):
    n = qkv.shape[0]
    bsz = n // seq
    t = min(256, seq)
    nq = seq // t
    kernel = functools.partial(_sb_kernel, t=t, scale=HEAD_DIM ** -0.5)
    return pl.pallas_call(
        kernel,
        grid=(bsz, heads, nq),
        in_specs=[pl.BlockSpec((t, HEAD_DIM), lambda b, h, i: (b * nq + i, h)),
                  pl.BlockSpec((seq, HEAD_DIM), lambda b, h, i: (b, heads + h)),
                  pl.BlockSpec((seq, HEAD_DIM), lambda b, h, i: (b, 2 * heads + h))],
        out_specs=pl.BlockSpec((t, HEAD_DIM), lambda b, h, i: (b * nq + i, h)),
        out_shape=jax.ShapeDtypeStruct((n, heads * HEAD_DIM), BF16),
        compiler_params=_params("parallel", "parallel", "arbitrary"),
    )(qkv, qkv, qkv)


def _router_kernel(h_ref, wr_ref, b_ref, g_ref, *, n_experts):
    per_group = n_experts // N_GROUPS
    tm = h_ref.shape[0]
    logits = lax.dot_general(wr_ref[...], h_ref[...], (((1,), (1,)), ((), ())), preferred_element_type=F32)
    scores = jax.nn.sigmoid(logits)
    choice = (scores + b_ref[...]).reshape(N_GROUPS, per_group, tm)
    s3 = scores.reshape(N_GROUPS, per_group, tm)
    neg = -jnp.inf
    in_group = lax.broadcasted_iota(I32, choice.shape, 1)
    m1 = jnp.max(choice, axis=1, keepdims=True)
    first = jnp.min(jnp.where(choice == m1, in_group, per_group), axis=1, keepdims=True)
    m2 = jnp.max(jnp.where(in_group == first, neg, choice), axis=1, keepdims=True)
    group_score = m1 + m2
    gid = lax.broadcasted_iota(I32, group_score.shape, 0)
    group_sel = jnp.zeros(group_score.shape, jnp.bool_)
    for _ in range(TOPK_GROUPS):
        m = jnp.max(group_score, axis=0, keepdims=True)
        f = jnp.min(jnp.where(group_score == m, gid, N_GROUPS), axis=0, keepdims=True)
        hit = gid == f
        group_sel = jnp.logical_or(group_sel, hit)
        group_score = jnp.where(hit, neg, group_score)
    cand = jnp.where(group_sel, choice, neg)
    eid = lax.broadcasted_iota(I32, choice.shape, 0) * per_group + in_group
    sel = jnp.zeros(choice.shape, jnp.bool_)
    for _ in range(TOP_K):
        m = jnp.max(jnp.max(cand, axis=1, keepdims=True), axis=0, keepdims=True)
        f = jnp.min(jnp.min(jnp.where(cand == m, eid, n_experts), axis=1, keepdims=True), axis=0, keepdims=True)
        hit = eid == f
        sel = jnp.logical_or(sel, hit)
        cand = jnp.where(hit, neg, cand)
    top_w = jnp.where(sel, s3, 0.0)
    total = jnp.sum(jnp.sum(top_w, axis=1, keepdims=True), axis=0, keepdims=True)
    gates = top_w / total * ROUTED_SCALE
    g_ref[...] = gates.reshape(n_experts, tm)


def moe_router(h, w_router, bias):
    n, d = h.shape
    e = w_router.shape[1]
    tm = min(512, n)
    return pl.pallas_call(
        functools.partial(_router_kernel, n_experts=e),
        grid=(n // tm,),
        in_specs=[pl.BlockSpec((tm, d), lambda i: (i, 0)),
                  pl.BlockSpec((e, d), lambda i: (0, 0)),
                  pl.BlockSpec((e, 1), lambda i: (0, 0))],
        out_specs=pl.BlockSpec((e, tm), lambda i: (0, i)),
        out_shape=jax.ShapeDtypeStruct((e, n), F32),
        compiler_params=_params("parallel"),
    )(h, w_router.T.astype(BF16), bias.reshape(e, 1))


def _moe_kernel(h_ref, g_ref, wg_ref, wu_ref, wd_ref, sg_ref, su_ref, sd_ref, o_ref):
    e = pl.program_id(1)
    x = h_ref[...]

    def ffn(wg, wu, wd, gate):
        g = jnp.dot(x, wg, preferred_element_type=F32)
        u = jnp.dot(x, wu, preferred_element_type=F32)
        act = g * jax.nn.sigmoid(g) * u
        if gate is not None:
            act = act * gate
        return jnp.dot(act.astype(BF16), wd, preferred_element_type=F32)

    @pl.when(e == 0)
    def _():
        o_ref[...] = ffn(sg_ref[...], su_ref[...], sd_ref[...], None)

    lane = lax.broadcasted_iota(I32, g_ref.shape, 1)
    gate = jnp.sum(jnp.where(lane == e, g_ref[...], 0.0), axis=1, keepdims=True)
    o_ref[...] += ffn(wg_ref[...], wu_ref[...], wd_ref[...], gate)


def moe_experts(h, gates, wg, wu, wd, sg, su, sd):
    n, d = h.shape
    e, _, f = wg.shape
    fs = sg.shape[1]
    tm = min(512, n)
    return pl.pallas_call(
        _moe_kernel,
        grid=(n // tm, e),
        in_specs=[pl.BlockSpec((tm, d), lambda i, j: (i, 0)),
                  pl.BlockSpec((tm, e), lambda i, j: (i, 0)),
                  pl.BlockSpec((None, d, f), lambda i, j: (j, 0, 0)),
                  pl.BlockSpec((None, d, f), lambda i, j: (j, 0, 0)),
                  pl.BlockSpec((None, f, d), lambda i, j: (j, 0, 0)),
                  pl.BlockSpec((d, fs), lambda i, j: (0, 0)),
                  pl.BlockSpec((d, fs), lambda i, j: (0, 0)),
                  pl.BlockSpec((fs, d), lambda i, j: (0, 0))],
        out_specs=pl.BlockSpec((tm, d), lambda i, j: (i, 0)),
        out_shape=jax.ShapeDtypeStruct((n, d), F32),
        compiler_params=_params("parallel", "arbitrary"),
    )(h, gates, wg, wu, wd, sg, su, sd)


def _residual_kernel(x_ref, y_ref, g_ref, o_ref):
    o_ref[...] = x_ref[...] + g_ref[...] * y_ref[...]


def residual_gate(x2, y, gate, seq):
    n, d = x2.shape
    tm = min(256, seq)
    per_batch = seq // tm
    return pl.pallas_call(
        _residual_kernel,
        grid=(n // tm,),
        in_specs=[pl.BlockSpec((tm, d), lambda i: (i, 0)),
                  pl.BlockSpec((tm, d), lambda i: (i, 0)),
                  pl.BlockSpec((None, 1, d), lambda i: (i // per_batch, 0, 0))],
        out_specs=pl.BlockSpec((tm, d), lambda i: (i, 0)),
        out_shape=jax.ShapeDtypeStruct((n, d), F32),
        compiler_params=_params("parallel"),
    )(x2, y, gate)


def _pad_cols(w, width):
    return jnp.pad(w, ((0, 0), (0, width - w.shape[1])))


def kernel(x, c, positions, ab_w_in, ab_w_out, gmlp_ln_g, gmlp_ln_b, gmlp_w_s, gmlp_b_s, sb_w_qkv, sb_w_out,
           norm_mix_g, ada_mix_w, ada_mix_b, norm_ffn_g, ada_ffn_w, ada_ffn_b, router_w, router_bias,
           expert_w_gate, expert_w_up, expert_w_down, shared_w_gate, shared_w_up, shared_w_down, final_norm_g):
    bsz, seq, d = x.shape
    depth = norm_mix_g.shape[0]
    a_width = gmlp_ln_g.shape[1]
    b_width = ab_w_out.shape[1] - a_width
    b_heads = b_width // HEAD_DIM
    c_heads = sb_w_out.shape[1] // HEAD_DIM
    n_sel = min(TOPK_MAX, seq // 4)
    main_width = 2 * a_width + b_width + 2 * HEAD_DIM
    idx_width = IDX_HEADS * IDX_DIM + LANES

    mod_mix = ada_modulation_all(c, ada_mix_w, ada_mix_b)
    mod_ffn = ada_modulation_all(c, ada_ffn_w, ada_ffn_b)
    tables = rope_tables(positions)
    x2 = x.reshape(bsz * seq, d)

    for layer in range(depth):
        j = layer // 2
        shift, scale, gate = mod_mix[layer, :, 0], mod_mix[layer, :, 1], mod_mix[layer, :, 2]
        h = norm_modulate(x2, norm_mix_g[layer], scale, shift, seq)
        if layer % 2 == 0:
            w_in = ab_w_in[j]
            proj = matmul(h, w_in[:, :main_width].astype(BF16), BF16, tn=640 if main_width % 640 == 0 else 128)
            proj_idx = matmul(h, _pad_cols(w_in[:, main_width:], idx_width).astype(BF16), F32, tn=idx_width)
            q, k, qi, kia, kib, w_idx = rope_prepare(proj, proj_idx, tables, 2 * a_width, b_heads)
            buf = gmlp_mixer(proj, a_width, gmlp_ln_g[j], gmlp_ln_b[j], gmlp_w_s[j], gmlp_b_s[j],
                             a_width + b_width)
            v_col_block = (2 * a_width + b_width + HEAD_DIM) // HEAD_DIM
            mixed = dsa_mixer(qi, w_idx, kia, kib, q, k, proj, v_col_block, buf, seq, n_sel)
            x2 = matmul_residual(mixed, ab_w_out[j].astype(BF16), x2, gate, seq)
        else:
            qkv = matmul(h, sb_w_qkv[j].astype(BF16), BF16)
            o = stick_breaking_mixer(qkv, seq, c_heads)
            x2 = matmul_residual(o, sb_w_out[j].astype(BF16), x2, gate, seq)

        shift, scale, gate = mod_ffn[layer, :, 0], mod_ffn[layer, :, 1], mod_ffn[layer, :, 2]
        h = norm_modulate(x2, norm_ffn_g[layer], scale, shift, seq)
        gates = moe_router(h, router_w[layer], router_bias[layer]).T
        y = moe_experts(h, gates, expert_w_gate[layer].astype(BF16), expert_w_up[layer].astype(BF16),
                        expert_w_down[layer].astype(BF16), shared_w_gate[layer].astype(BF16),
                        shared_w_up[layer].astype(BF16), shared_w_down[layer].astype(BF16))
        x2 = residual_gate(x2, y, gate, seq)

    return final_norm(x2, final_norm_g).reshape(bsz, seq, d)
```

```python
import functools

import jax
import jax.numpy as jnp
from jax import lax
from jax.experimental import pallas as pl
from jax.experimental.pallas import tpu as pltpu

F32 = jnp.float32
BF16 = jnp.bfloat16
I32 = jnp.int32

EPS = 1e-6
ROPE_THETA = 10000.0
LANES = 128
HEAD_DIM = 128
A_GROUPS = 8
A_CHUNK = 128
IDX_HEADS = 16
IDX_DIM = 64
TOPK_MAX = 256
N_GROUPS = 8
TOPK_GROUPS = 4
TOP_K = 8
ROUTED_SCALE = 2.5
VMEM_LIMIT = 56 * 1024 * 1024
INT_MIN = -(2 ** 31)
MASKED = -1e30
LOG2E = 1.4426950408889634


def _params(*sem):
    return pltpu.CompilerParams(dimension_semantics=sem, vmem_limit_bytes=VMEM_LIMIT)


def _mod_kernel(c_ref, w_ref, b_ref, o_ref):
    c = c_ref[...]
    s = c * jax.nn.sigmoid(c)
    o_ref[...] = jnp.dot(s.astype(BF16), w_ref[...].astype(BF16), preferred_element_type=F32) + b_ref[...]


def ada_modulation_all(c, w, b):
    bsz, d = c.shape
    n_layers, _, n_out = w.shape
    rows = 8
    c_pad = jnp.zeros((rows, d), F32).at[:bsz].set(c)
    tn = min(512, n_out)
    out = pl.pallas_call(
        _mod_kernel,
        grid=(n_layers, n_out // tn),
        in_specs=[pl.BlockSpec((rows, d), lambda l, j: (0, 0)),
                  pl.BlockSpec((None, d, tn), lambda l, j: (l, 0, j)),
                  pl.BlockSpec((None, 1, tn), lambda l, j: (l, 0, j))],
        out_specs=pl.BlockSpec((None, rows, tn), lambda l, j: (l, 0, j)),
        out_shape=jax.ShapeDtypeStruct((n_layers, rows, n_out), F32),
        compiler_params=_params("parallel", "parallel"), name="ada_modulation",
    )(c_pad, w, b.reshape(n_layers, 1, n_out))
    return out[:, :bsz].reshape(n_layers, bsz, 3, 1, d)


def _norm_mod_kernel(x_ref, g_ref, sc_ref, sh_ref, o_ref):
    x = x_ref[...]
    y = x * lax.rsqrt(jnp.mean(x * x, axis=-1, keepdims=True) + EPS)
    o_ref[...] = ((y * g_ref[...]) * (1.0 + sc_ref[...]) + sh_ref[...]).astype(o_ref.dtype)


def _norm_mod_pack_kernel(x_ref, g_ref, sc_ref, sh_ref, o_ref, p_ref):
    x = x_ref[...]
    y = x * lax.rsqrt(jnp.mean(x * x, axis=-1, keepdims=True) + EPS)
    h = (y * g_ref[...]) * (1.0 + sc_ref[...]) + sh_ref[...]
    o_ref[...] = h.astype(o_ref.dtype)
    p_ref[...] = _pack_pairs(h)


def norm_modulate(x2, g, scale, shift, seq, pack=False):
    n, d = x2.shape
    tm = min(256, seq)
    per_batch = seq // tm
    row = lambda w: pl.BlockSpec((tm, w), lambda i: (i, 0))
    out_specs, out_shape = row(d), jax.ShapeDtypeStruct((n, d), BF16)
    if pack:
        out_specs, out_shape = [out_specs, row(d // 2)], [out_shape, jax.ShapeDtypeStruct((n, d // 2), I32)]
    return pl.pallas_call(
        _norm_mod_pack_kernel if pack else _norm_mod_kernel,
        grid=(n // tm,),
        in_specs=[row(d),
                  pl.BlockSpec((1, d), lambda i: (0, 0)),
                  pl.BlockSpec((None, 1, d), lambda i: (i // per_batch, 0, 0)),
                  pl.BlockSpec((None, 1, d), lambda i: (i // per_batch, 0, 0))],
        out_specs=out_specs,
        out_shape=out_shape,
        compiler_params=_params("parallel"), name="norm_modulate",
    )(x2, g.reshape(1, d), scale, shift)


def _final_norm_kernel(x_ref, g_ref, o_ref):
    x = x_ref[...]
    y = x * lax.rsqrt(jnp.mean(x * x, axis=-1, keepdims=True) + EPS)
    o_ref[...] = y * g_ref[...]


def final_norm(x2, g):
    n, d = x2.shape
    tm = min(256, n)
    return pl.pallas_call(
        _final_norm_kernel,
        grid=(n // tm,),
        in_specs=[pl.BlockSpec((tm, d), lambda i: (i, 0)), pl.BlockSpec((1, d), lambda i: (0, 0))],
        out_specs=pl.BlockSpec((tm, d), lambda i: (i, 0)),
        out_shape=jax.ShapeDtypeStruct((n, d), F32),
        compiler_params=_params("parallel"), name="final_norm",
    )(x2, g.reshape(1, d))


def _mm_kernel(a_ref, w_ref, o_ref):
    o_ref[...] = jnp.dot(a_ref[...], w_ref[...], preferred_element_type=F32).astype(o_ref.dtype)


def matmul(a, w, out_dtype, tm=1024, tn=512):
    m, k = a.shape
    n = w.shape[1]
    tm, tn = min(tm, m), min(tn, n)
    return pl.pallas_call(
        _mm_kernel,
        grid=(m // tm, n // tn),
        in_specs=[pl.BlockSpec((tm, k), lambda i, j: (i, 0)),
                  pl.BlockSpec((k, tn), lambda i, j: (0, j))],
        out_specs=pl.BlockSpec((tm, tn), lambda i, j: (i, j)),
        out_shape=jax.ShapeDtypeStruct((m, n), out_dtype),
        compiler_params=_params("parallel", "parallel"), name="projection",
    )(a, w)


def _mm_res_kernel(a_ref, w_ref, r_ref, g_ref, o_ref):
    acc = jnp.dot(a_ref[...], w_ref[...], preferred_element_type=F32)
    o_ref[...] = r_ref[...] + g_ref[...] * acc


def matmul_residual(a, w, res, gate, seq, tm=1024, tn=512):
    m, k = a.shape
    n = w.shape[1]
    tm, tn = min(tm, seq), min(tn, n)
    per_batch = seq // tm
    return pl.pallas_call(
        _mm_res_kernel,
        grid=(m // tm, n // tn),
        in_specs=[pl.BlockSpec((tm, k), lambda i, j: (i, 0)),
                  pl.BlockSpec((k, tn), lambda i, j: (0, j)),
                  pl.BlockSpec((tm, tn), lambda i, j: (i, j)),
                  pl.BlockSpec((None, 1, tn), lambda i, j: (i // per_batch, 0, j))],
        out_specs=pl.BlockSpec((tm, tn), lambda i, j: (i, j)),
        out_shape=jax.ShapeDtypeStruct((m, n), F32),
        compiler_params=_params("parallel", "parallel"), name="projection_residual",
    )(a, w, res, gate)


def _gelu(x):
    return 0.5 * x * (1.0 + lax.erf(x * (2.0 ** -0.5)))


def _gmlp_kernel(u_ref, v_ref, lng_ref, lnb_ref, ws_ref, bs_ref, o_ref, *, groups):
    u = _gelu(u_ref[...].astype(F32))
    v = _gelu(v_ref[...].astype(F32))
    mu = jnp.mean(v, axis=-1, keepdims=True)
    var = jnp.mean(jnp.square(v - mu), axis=-1, keepdims=True)
    vn = ((v - mu) * lax.rsqrt(var + EPS) * lng_ref[...] + lnb_ref[...]).astype(BF16)
    t = ws_ref.shape[1]
    gw = u.shape[1] // groups
    row = lax.broadcasted_iota(I32, (t, t), 0)
    col = lax.broadcasted_iota(I32, (t, t), 1)
    causal = col <= row
    for g in range(groups):
        w = jnp.where(causal, ws_ref[g], 0.0).astype(BF16)
        mixed = jnp.dot(w, vn[:, g * gw:(g + 1) * gw], preferred_element_type=F32) + bs_ref[:, g:g + 1]
        o_ref[:, g * gw:(g + 1) * gw] = (u[:, g * gw:(g + 1) * gw] * mixed).astype(o_ref.dtype)
    o_ref[:, u.shape[1]:] = jnp.zeros((t, o_ref.shape[1] - u.shape[1]), o_ref.dtype)


def gmlp_mixer(proj, a_width, ln_g, ln_b, w_s, b_s, out_width):
    n = proj.shape[0]
    groups, t, _ = w_s.shape
    return pl.pallas_call(
        functools.partial(_gmlp_kernel, groups=groups),
        grid=(n // t,),
        in_specs=[pl.BlockSpec((t, a_width), lambda i: (i, 0)),
                  pl.BlockSpec((t, a_width), lambda i: (i, 1)),
                  pl.BlockSpec((1, a_width), lambda i: (0, 0)),
                  pl.BlockSpec((1, a_width), lambda i: (0, 0)),
                  pl.BlockSpec((groups, t, t), lambda i: (0, 0, 0)),
                  pl.BlockSpec((t, groups), lambda i: (0, 0))],
        out_specs=pl.BlockSpec((t, out_width), lambda i: (i, 0)),
        out_shape=jax.ShapeDtypeStruct((n, out_width), BF16),
        compiler_params=_params("parallel"), name="gmlp_mixer",
    )(proj, proj, ln_g.reshape(1, a_width), ln_b.reshape(1, a_width), w_s, b_s.T)


def _rope_kernel(q_ref, k_ref, x_ref, cb_ref, sb_ref, ci_ref, si_ref, ck_ref, sk_ref,
                 qo_ref, ko_ref, qio_ref, kia_ref, kib_ref, wo_ref, *, heads, idx_pairs, q_scale, w_scale):
    cb, sb = cb_ref[...], sb_ref[...]
    half = HEAD_DIM // 2
    for h in range(heads):
        x = q_ref[:, h * HEAD_DIM:(h + 1) * HEAD_DIM].astype(F32)
        r = x * cb + pltpu.roll(x, half, 1) * sb
        qo_ref[:, h * HEAD_DIM:(h + 1) * HEAD_DIM] = (r * q_scale).astype(qo_ref.dtype)
    x = k_ref[...].astype(F32)
    ko_ref[...] = (x * cb + pltpu.roll(x, half, 1) * sb).astype(ko_ref.dtype)

    lane = lax.broadcasted_iota(I32, cb.shape, 1)
    first_half = (lane % IDX_DIM) < (IDX_DIM // 2)

    def rope_idx(x, c, s):
        rot = jnp.where(first_half, pltpu.roll(x, LANES - IDX_DIM // 2, 1), pltpu.roll(x, IDX_DIM // 2, 1))
        return x * c + rot * s

    ci, si = ci_ref[...], si_ref[...]
    for p in range(idx_pairs):
        x = x_ref[:, p * LANES:(p + 1) * LANES]
        qio_ref[:, p * LANES:(p + 1) * LANES] = rope_idx(x, ci, si).astype(qio_ref.dtype)
    tail = x_ref[:, idx_pairs * LANES:(idx_pairs + 1) * LANES]
    roped = rope_idx(tail, ck_ref[...], sk_ref[...])
    is_k = lane < IDX_DIM
    ka = jnp.where(is_k, roped, 0.0)
    kia_ref[...] = ka.astype(kia_ref.dtype)
    kib_ref[...] = pltpu.roll(ka, IDX_DIM, 1).astype(kib_ref.dtype)
    w = pltpu.roll(tail, LANES - IDX_DIM, 1)
    wo_ref[...] = jnp.where(lane < IDX_HEADS, w * w_scale, 0.0)


def rope_prepare(proj, proj_idx, tables, q_off, heads):
    n = proj.shape[0]
    tm = min(256, n)
    idx_pairs = IDX_HEADS * IDX_DIM // LANES
    qb = q_off // (heads * HEAD_DIM)
    kb = (q_off + heads * HEAD_DIM) // HEAD_DIM
    row = lambda w: pl.BlockSpec((tm, w), lambda i: (i, 0))
    kernel = functools.partial(_rope_kernel, heads=heads, idx_pairs=idx_pairs, q_scale=HEAD_DIM ** -0.5,
                               w_scale=IDX_HEADS ** -0.5 * IDX_DIM ** -0.5)
    return pl.pallas_call(
        kernel,
        grid=(n // tm,),
        in_specs=[pl.BlockSpec((tm, heads * HEAD_DIM), lambda i: (i, qb)),
                  pl.BlockSpec((tm, HEAD_DIM), lambda i: (i, kb)),
                  row(proj_idx.shape[1])] + [row(LANES)] * 6,
        out_specs=[row(heads * HEAD_DIM), row(HEAD_DIM), row(idx_pairs * LANES), row(LANES), row(LANES), row(LANES)],
        out_shape=[jax.ShapeDtypeStruct((n, heads * HEAD_DIM), BF16),
                   jax.ShapeDtypeStruct((n, HEAD_DIM), BF16),
                   jax.ShapeDtypeStruct((n, idx_pairs * LANES), BF16),
                   jax.ShapeDtypeStruct((n, LANES), BF16),
                   jax.ShapeDtypeStruct((n, LANES), BF16),
                   jax.ShapeDtypeStruct((n, LANES), F32)],
        compiler_params=_params("parallel"), name="rope_prepare",
    )(proj, proj, proj_idx, *tables)


def rope_tables(positions):
    pos = positions.reshape(-1).astype(F32)[:, None]

    def cs(dim):
        inv = ROPE_THETA ** (-jnp.arange(0, dim, 2, dtype=F32) / dim)
        ang = pos * inv
        return jnp.cos(ang), jnp.sin(ang)

    cb, sb = cs(HEAD_DIM)
    ci, si = cs(IDX_DIM)
    ones, zeros = jnp.ones_like(cb), jnp.zeros_like(cb)
    return (jnp.concatenate([cb, cb], 1), jnp.concatenate([-sb, sb], 1),
            jnp.concatenate([ci, ci, ci, ci], 1), jnp.concatenate([-si, si, -si, si], 1),
            jnp.concatenate([ci, ci, ones], 1), jnp.concatenate([-si, si, zeros], 1))


def _sort_key(x):
    x = jnp.where(x == 0.0, 0.0, x)
    bits = pltpu.bitcast(x, I32)
    return jnp.where(bits < 0, bits ^ 0x7FFFFFFF, bits)


def _dsa_kernel(qi_ref, w_ref, kia_ref, kib_ref, q_ref, k_ref, v_ref, buf_ref, o_ref, key_ref,
                *, tq, tk, heads, n_sel):
    del buf_ref
    qt = pl.program_id(1)
    q_lo = qt * tq
    n_kb = (q_lo + tq + tk - 1) // tk
    sub = tk // LANES
    row_pos = q_lo + lax.broadcasted_iota(I32, (tq, LANES), 0)
    lane = lax.broadcasted_iota(I32, (tq, LANES), 1)

    w = w_ref[...]
    w_cols = [jnp.broadcast_to(w[:, h:h + 1], (tq, tk)) for h in range(IDX_HEADS)]

    def score_block(kb, carry):
        k0 = pl.multiple_of(kb * tk, tk)
        ka = kia_ref[pl.ds(k0, tk), :]
        kb_ = kib_ref[pl.ds(k0, tk), :]
        acc = jnp.zeros((tq, tk), F32)
        for p in range(IDX_HEADS // 2):
            x = qi_ref[:, p * LANES:(p + 1) * LANES]
            ra = lax.dot_general(x, ka, (((1,), (1,)), ((), ())), preferred_element_type=F32)
            rb = lax.dot_general(x, kb_, (((1,), (1,)), ((), ())), preferred_element_type=F32)
            acc = acc + w_cols[2 * p] * jnp.maximum(ra, 0.0)
            acc = acc + w_cols[2 * p + 1] * jnp.maximum(rb, 0.0)
        key = _sort_key(acc)
        for j in range(sub):
            col_pos = k0 + j * LANES + lane
            key_ref[kb * sub + j] = jnp.where(col_pos <= row_pos, key[:, j * LANES:(j + 1) * LANES], INT_MIN)
        return carry

    lax.fori_loop(0, n_kb, score_block, 0)
    n_slabs = n_kb * sub

    def count(pred):
        def body(c, acc):
            return acc + jnp.where(pred(key_ref[c], c), 1, 0)
        part = lax.fori_loop(0, n_slabs, body, jnp.zeros((tq, LANES), I32))
        return jnp.sum(part.astype(F32), axis=1, keepdims=True).astype(I32)

    def search_bit(i, u):
        cand = u | (1 << (31 - i))
        thr = jnp.broadcast_to(cand ^ INT_MIN, (tq, LANES))
        total = count(lambda kv, c: kv >= thr)
        return jnp.where(total >= n_sel, cand, u)

    u = lax.fori_loop(0, 32, search_bit, jnp.zeros((tq, 1), I32))
    thr = u ^ INT_MIN
    thr_b = jnp.broadcast_to(thr, (tq, LANES))
    n_gt = count(lambda kv, c: kv > thr_b)
    n_ge = count(lambda kv, c: kv >= thr_b)
    need = n_sel - n_gt
    tie_rows = jnp.logical_and(n_ge > n_sel, need > 0)

    seq_bits = max(1, (key_ref.shape[0] * LANES).bit_length())

    def tie_search():
        need_b = need

        def bit_step(i, j):
            cand = j | (1 << (seq_bits - 1 - i))
            cand_b = jnp.broadcast_to(cand, (tq, LANES))
            total = count(lambda kv, c: jnp.logical_and(kv == thr_b, c * LANES + lane < cand_b))
            return jnp.where(total <= need_b, cand, j)

        return lax.fori_loop(0, seq_bits, bit_step, jnp.zeros((tq, 1), I32))

    any_tie = jnp.max(jnp.where(tie_rows, 1.0, 0.0)) > 0.0
    bound = lax.cond(any_tie, tie_search, lambda: jnp.full((tq, 1), 2 ** 30, I32))
    bound = jnp.where(tie_rows, bound, 2 ** 30)
    bound_b = jnp.broadcast_to(bound, (tq, LANES))

    q_all = jnp.concatenate([q_ref[:, h * HEAD_DIM:(h + 1) * HEAD_DIM] for h in range(heads)], axis=0)

    def attn_block(kb, carry):
        m, l, acc = carry
        k0 = pl.multiple_of(kb * tk, tk)
        kk = k_ref[pl.ds(k0, tk), :]
        vv = v_ref[pl.ds(k0, tk), :]
        s = lax.dot_general(q_all, kk, (((1,), (1,)), ((), ())), preferred_element_type=F32)
        bias_cols = []
        for j in range(sub):
            kv = key_ref[kb * sub + j]
            col_pos = k0 + j * LANES + lane
            take = jnp.logical_or(kv > thr_b, jnp.logical_and(kv == thr_b, col_pos < bound_b))
            take = jnp.logical_and(take, col_pos <= row_pos)
            bias_cols.append(jnp.where(take, 0.0, MASKED))
        bias = jnp.concatenate(bias_cols, axis=1)
        s = s.reshape(heads, tq, tk) + bias[None]
        m_new = jnp.maximum(m, jnp.max(s, axis=-1, keepdims=True))
        alpha = jnp.exp(m - m_new)
        p = jnp.exp(s - m_new)
        l = alpha * l + jnp.sum(p, axis=-1, keepdims=True)
        pv = jnp.dot(p.reshape(heads * tq, tk).astype(BF16), vv, preferred_element_type=F32)
        acc = alpha * acc + pv.reshape(heads, tq, HEAD_DIM)
        return m_new, l, acc

    init = (jnp.full((heads, tq, 1), MASKED, F32), jnp.zeros((heads, tq, 1), F32),
            jnp.zeros((heads, tq, HEAD_DIM), F32))
    m, l, acc = lax.fori_loop(0, n_kb, attn_block, init)
    out = acc / l
    for h in range(heads):
        o_ref[:, h * HEAD_DIM:(h + 1) * HEAD_DIM] = out[h].astype(o_ref.dtype)


def dsa_mixer(qi, w_idx, kia, kib, q, k, proj, v_col_block, buf, seq, n_sel):
    n = q.shape[0]
    bsz = n // seq
    heads = q.shape[1] // HEAD_DIM
    tq = min(128, seq)
    tk = min(512, seq)
    nq = seq // tq
    kernel = functools.partial(_dsa_kernel, tq=tq, tk=tk, heads=heads, n_sel=n_sel)
    qrow = lambda w: pl.BlockSpec((tq, w), lambda b, i: (b * nq + i, 0))
    kv = lambda c: pl.BlockSpec((seq, LANES), lambda b, i: (b, c))
    return pl.pallas_call(
        kernel,
        grid=(bsz, nq),
        in_specs=[qrow(qi.shape[1]), qrow(LANES), kv(0), kv(0), qrow(q.shape[1]), kv(0), kv(v_col_block),
                  pl.BlockSpec(memory_space=pl.ANY)],
        out_specs=pl.BlockSpec((tq, heads * HEAD_DIM), lambda b, i: (b * nq + i, 1)),
        out_shape=jax.ShapeDtypeStruct(buf.shape, buf.dtype),
        scratch_shapes=[pltpu.VMEM((seq // LANES, tq, LANES), I32)],
        input_output_aliases={7: 0},
        compiler_params=_params("parallel", "arbitrary"), name="dsa_mixer",
    )(qi, w_idx, kia, kib, q, k, proj, buf)


SB_HEADS_PER_STEP = 4
SB_LOGIT_SCALE = HEAD_DIM ** -0.5 * LOG2E


def _sb_kernel(q_ref, k_ref, v_ref, o_ref, *, t, group):
    qt = pl.program_id(2)
    row = lax.broadcasted_iota(I32, (t, t), 0)
    col = lax.broadcasted_iota(I32, (t, t), 1)
    later = jnp.where(row > col, 1.0, 0.0).astype(BF16)
    strict = col < row

    def block(kb, masked, carry):
        k0 = pl.multiple_of(kb * t, t)
        heads = range(group)
        cols = [slice(h * HEAD_DIM, (h + 1) * HEAD_DIM) for h in heads]
        z = [lax.dot_general(q_ref[:, cols[h]], k_ref[pl.ds(k0, t), cols[h]], (((1,), (1,)), ((), ())),
                             preferred_element_type=F32) for h in heads]
        log_beta = [jnp.minimum(z[h], 0.0) - jnp.log(1.0 + jnp.exp2(-jnp.abs(z[h]))) * LOG2E for h in heads]
        log_fail = [log_beta[h] - z[h] for h in heads]
        if masked:
            log_fail = [jnp.where(strict, log_fail[h], 0.0) for h in heads]
        between = [jnp.dot(log_fail[h].astype(BF16), later, preferred_element_type=F32) + carry[h][1]
                   for h in heads]
        a = [jnp.exp2(log_beta[h] + between[h]) for h in heads]
        if masked:
            a = [jnp.where(strict, a[h], 0.0) for h in heads]
        acc = [carry[h][0] + jnp.dot(a[h].astype(BF16), v_ref[pl.ds(k0, t), cols[h]], preferred_element_type=F32)
               for h in heads]
        run = [carry[h][1] + jnp.sum(log_fail[h], axis=-1, keepdims=True) for h in heads]
        return tuple((acc[h], run[h]) for h in heads)

    init = tuple((jnp.zeros((t, HEAD_DIM), F32), jnp.zeros((t, 1), F32)) for _ in range(group))
    carry = block(qt, True, init)
    carry = lax.fori_loop(0, qt, lambda i, c: block(qt - 1 - i, False, c), carry)
    for h in range(group):
        o_ref[:, h * HEAD_DIM:(h + 1) * HEAD_DIM] = carry[h][0].astype(o_ref.dtype)


def stick_breaking_mixer(qkv, seq, heads):
    n = qkv.shape[0]
    bsz = n // seq
    t = min(256, seq)
    nq = seq // t
    group = SB_HEADS_PER_STEP
    hg = heads // group
    width = group * HEAD_DIM
    kernel = functools.partial(_sb_kernel, t=t, group=group)
    return pl.pallas_call(
        kernel,
        grid=(bsz, hg, nq),
        in_specs=[pl.BlockSpec((t, width), lambda b, h, i: (b * nq + i, h)),
                  pl.BlockSpec((seq, width), lambda b, h, i: (b, hg + h)),
                  pl.BlockSpec((seq, width), lambda b, h, i: (b, 2 * hg + h))],
        out_specs=pl.BlockSpec((t, width), lambda b, h, i: (b * nq + i, h)),
        out_shape=jax.ShapeDtypeStruct((n, heads * HEAD_DIM), BF16),
        compiler_params=_params("parallel", "parallel", "arbitrary"), name="stick_breaking",
    )(qkv, qkv, qkv)


def _router_kernel(h_ref, wr_ref, b_ref, e_ref, w_ref, r_ref, c_ref, cnt_ref, *, n_experts):
    per_group = n_experts // N_GROUPS
    tm = h_ref.shape[0]
    logits = lax.dot_general(wr_ref[...], h_ref[...], (((1,), (1,)), ((), ())), preferred_element_type=F32)
    scores = jax.nn.sigmoid(logits)
    choice = (scores + b_ref[...]).reshape(N_GROUPS, per_group, tm)
    s3 = scores.reshape(N_GROUPS, per_group, tm)
    neg = -jnp.inf
    in_group = lax.broadcasted_iota(I32, choice.shape, 1)
    m1 = jnp.max(choice, axis=1, keepdims=True)
    first = jnp.min(jnp.where(choice == m1, in_group, per_group), axis=1, keepdims=True)
    m2 = jnp.max(jnp.where(in_group == first, neg, choice), axis=1, keepdims=True)
    group_score = m1 + m2
    gid = lax.broadcasted_iota(I32, group_score.shape, 0)
    group_sel = jnp.zeros(group_score.shape, jnp.bool_)
    for _ in range(TOPK_GROUPS):
        m = jnp.max(group_score, axis=0, keepdims=True)
        f = jnp.min(jnp.where(group_score == m, gid, N_GROUPS), axis=0, keepdims=True)
        hit = gid == f
        group_sel = jnp.logical_or(group_sel, hit)
        group_score = jnp.where(hit, neg, group_score)
    cand = jnp.where(group_sel, choice, neg)
    eid = lax.broadcasted_iota(I32, choice.shape, 0) * per_group + in_group
    sel = jnp.zeros(choice.shape, jnp.bool_)
    hits = []
    for _ in range(TOP_K):
        m = jnp.max(jnp.max(cand, axis=1, keepdims=True), axis=0, keepdims=True)
        f = jnp.min(jnp.min(jnp.where(cand == m, eid, n_experts), axis=1, keepdims=True), axis=0, keepdims=True)
        hit = eid == f
        hits.append((hit, f))
        sel = jnp.logical_or(sel, hit)
        cand = jnp.where(hit, neg, cand)
    top_w = jnp.where(sel, s3, 0.0)
    total = jnp.sum(jnp.sum(top_w, axis=1, keepdims=True), axis=0, keepdims=True)
    gates = top_w / total * ROUTED_SCALE

    @pl.when(pl.program_id(0) == 0)
    def _():
        cnt_ref[...] = jnp.zeros_like(cnt_ref)

    sel_f = jnp.where(sel, 1.0, 0.0).reshape(n_experts, tm)
    row = lax.broadcasted_iota(I32, (tm, tm), 0)
    col = lax.broadcasted_iota(I32, (tm, tm), 1)
    before = jnp.where(row < col, 1.0, 0.0).astype(BF16)
    prefix = jnp.dot(sel_f.astype(BF16), before, preferred_element_type=F32)
    rank_all = (prefix + cnt_ref[:, 0:1]).reshape(choice.shape)

    def pick(hit, val):
        return jnp.sum(jnp.sum(jnp.where(hit, val, 0.0), axis=1, keepdims=True), axis=0, keepdims=True)

    for k, (hit, f) in enumerate(hits):
        e_ref[k:k + 1, :] = f.reshape(1, tm)
        w_ref[k:k + 1, :] = pick(hit, gates).reshape(1, tm)
        r_ref[k:k + 1, :] = pick(hit, rank_all).reshape(1, tm).astype(I32)
    cnt_ref[...] = cnt_ref[...] + jnp.sum(sel_f, axis=1, keepdims=True)
    c_ref[...] = cnt_ref[...].astype(I32)


def moe_router(h, w_router, bias):
    n, d = h.shape
    e = w_router.shape[1]
    tm = min(512, n)
    top = lambda dt: jax.ShapeDtypeStruct((TOP_K, n), dt)
    top_spec = pl.BlockSpec((TOP_K, tm), lambda i: (0, i))
    ids, wts, rank, counts = pl.pallas_call(
        functools.partial(_router_kernel, n_experts=e),
        grid=(n // tm,),
        in_specs=[pl.BlockSpec((tm, d), lambda i: (i, 0)),
                  pl.BlockSpec((e, d), lambda i: (0, 0)),
                  pl.BlockSpec((e, 1), lambda i: (0, 0))],
        out_specs=[top_spec, top_spec, top_spec, pl.BlockSpec((e, LANES), lambda i: (0, 0))],
        out_shape=[top(I32), top(F32), top(I32), jax.ShapeDtypeStruct((e, LANES), I32)],
        scratch_shapes=[pltpu.VMEM((e, LANES), F32)],
        compiler_params=_params("arbitrary"), name="moe_router",
    )(h, w_router.T.astype(BF16), bias.reshape(e, 1))
    return ids, wts, rank, counts[:, 0]


MOE_TILE = 512
DISPATCH_TOKENS = 256
COMBINE_TOKENS = 128


def _pack_pairs(x):
    w = x.shape[1] // 2
    lo = pltpu.bitcast(x[:, :w].astype(BF16).astype(F32), I32)
    hi = pltpu.bitcast(x[:, w:].astype(BF16).astype(F32), I32)
    return lax.shift_right_logical(lo, 16) | hi


def _unpack_pairs(p):
    lo = pltpu.bitcast(lax.shift_left(p, 16), F32)
    hi = pltpu.bitcast(p & jnp.int32(-65536), F32)
    return lo, hi


def _row_copy(src, src_row, dst, dst_row, sem):
    return pltpu.make_async_copy(src.at[pl.ds(src_row, 1), :], dst.at[pl.ds(dst_row, 1), :], sem)


def _dispatch_kernel(pos_ref, hp_ref, xs_in_ref, xs_ref, sem, *, td, n_tok):
    del xs_in_ref
    base = pl.program_id(0) * td

    def issue(r, carry):
        k, t = r // td, r % td
        _row_copy(hp_ref, t, xs_ref, pos_ref[k * n_tok + base + t], sem).start()
        return carry

    def drain(r, carry):
        _row_copy(hp_ref, 0, xs_ref, 0, sem).wait()
        return carry

    lax.fori_loop(0, TOP_K * td, issue, 0)
    lax.fori_loop(0, TOP_K * td, drain, 0)


def moe_dispatch(hp, pos_flat, n_rows):
    n, w = hp.shape
    td = min(DISPATCH_TOKENS, n)
    kernel = functools.partial(_dispatch_kernel, td=td, n_tok=n)
    return pl.pallas_call(
        kernel,
        grid_spec=pltpu.PrefetchScalarGridSpec(
            num_scalar_prefetch=1,
            grid=(n // td,),
            in_specs=[pl.BlockSpec((td, w), lambda i, pos: (i, 0)),
                      pl.BlockSpec(memory_space=pl.ANY)],
            out_specs=pl.BlockSpec(memory_space=pl.ANY),
            scratch_shapes=[pltpu.SemaphoreType.DMA(())]),
        out_shape=jax.ShapeDtypeStruct((n_rows, w), I32),
        input_output_aliases={2: 0},
        compiler_params=_params("arbitrary"), name="moe_dispatch",
    )(pos_flat, hp, jnp.zeros((n_rows, w), I32))


def _swiglu(lo, hi, wg_ref, wu_ref, wd_ref):
    w = lo.shape[1]
    g = (jnp.dot(lo, wg_ref[:w, :], preferred_element_type=F32)
         + jnp.dot(hi, wg_ref[w:, :], preferred_element_type=F32))
    u = (jnp.dot(lo, wu_ref[:w, :], preferred_element_type=F32)
         + jnp.dot(hi, wu_ref[w:, :], preferred_element_type=F32))
    act = g * jax.nn.sigmoid(g) * u
    return jnp.dot(act.astype(BF16), wd_ref[...], preferred_element_type=F32)


def _grouped_kernel(te_ref, nu_ref, x_ref, wg_ref, wu_ref, wd_ref, y_ref):
    del te_ref

    @pl.when(pl.program_id(0) < nu_ref[0])
    def _():
        lo, hi = _unpack_pairs(x_ref[...])
        y_ref[...] = _pack_pairs(_swiglu(lo.astype(BF16), hi.astype(BF16), wg_ref, wu_ref, wd_ref))


def moe_grouped(xs, tile_expert, n_used, wg, wu, wd):
    p, w = xs.shape
    e, d, f = wg.shape
    tm = MOE_TILE
    row = lambda i, te, nu: (jnp.minimum(i, nu[0] - 1), 0)
    return pl.pallas_call(
        _grouped_kernel,
        grid_spec=pltpu.PrefetchScalarGridSpec(
            num_scalar_prefetch=2,
            grid=(p // tm,),
            in_specs=[pl.BlockSpec((tm, w), row),
                      pl.BlockSpec((None, d, f), lambda i, te, nu: (te[i], 0, 0)),
                      pl.BlockSpec((None, d, f), lambda i, te, nu: (te[i], 0, 0)),
                      pl.BlockSpec((None, f, d), lambda i, te, nu: (te[i], 0, 0))],
            out_specs=pl.BlockSpec((tm, w), row)),
        out_shape=jax.ShapeDtypeStruct((p, w), I32),
        compiler_params=_params("arbitrary"), name="moe_grouped",
    )(tile_expert, n_used, xs, wg, wu, wd)


def _shared_kernel(h_ref, wg_ref, wu_ref, wd_ref, o_ref):
    x = h_ref[...]
    w = x.shape[1] // 2
    o_ref[...] = _swiglu(x[:, :w], x[:, w:], wg_ref, wu_ref, wd_ref).astype(o_ref.dtype)


def shared_expert(h, wg, wu, wd):
    n, d = h.shape
    f = wg.shape[1]
    tm = min(512, n)
    return pl.pallas_call(
        _shared_kernel,
        grid=(n // tm,),
        in_specs=[pl.BlockSpec((tm, d), lambda i: (i, 0)),
                  pl.BlockSpec((d, f), lambda i: (0, 0)),
                  pl.BlockSpec((d, f), lambda i: (0, 0)),
                  pl.BlockSpec((f, d), lambda i: (0, 0))],
        out_specs=pl.BlockSpec((tm, d), lambda i: (i, 0)),
        out_shape=jax.ShapeDtypeStruct((n, d), BF16),
        compiler_params=_params("parallel"), name="shared_expert",
    )(h, wg, wu, wd)


def _combine_kernel(pos_ref, ys_ref, w_ref, x_ref, sh_ref, g_ref, o_ref, rows_ref, sem, *, tc, n_tok):
    base = pl.program_id(0) * tc

    def issue(r, carry):
        k, t = r // tc, r % tc
        _row_copy(ys_ref, pos_ref[k * n_tok + base + t], rows_ref.at[k], t, sem).start()
        return carry

    def drain(r, carry):
        _row_copy(ys_ref, 0, rows_ref.at[0], 0, sem).wait()
        return carry

    lax.fori_loop(0, TOP_K * tc, issue, 0)
    lax.fori_loop(0, TOP_K * tc, drain, 0)

    half = rows_ref.shape[2]
    acc_lo = jnp.zeros((tc, half), F32)
    acc_hi = jnp.zeros((tc, half), F32)
    for k in range(TOP_K):
        lo, hi = _unpack_pairs(rows_ref[k])
        wk = w_ref[:, k:k + 1]
        acc_lo = acc_lo + wk * lo
        acc_hi = acc_hi + wk * hi
    o_ref[:, :half] = x_ref[:, :half] + g_ref[:, :half] * (acc_lo + sh_ref[:, :half].astype(F32))
    o_ref[:, half:] = x_ref[:, half:] + g_ref[:, half:] * (acc_hi + sh_ref[:, half:].astype(F32))


def moe_combine(ys, pos_flat, wts, x2, shared, gate, seq):
    n, d = x2.shape
    w = ys.shape[1]
    tc = min(COMBINE_TOKENS, seq)
    per_batch = seq // tc
    kernel = functools.partial(_combine_kernel, tc=tc, n_tok=n)
    return pl.pallas_call(
        kernel,
        grid_spec=pltpu.PrefetchScalarGridSpec(
            num_scalar_prefetch=1,
            grid=(n // tc,),
            in_specs=[pl.BlockSpec(memory_space=pl.ANY),
                      pl.BlockSpec((tc, TOP_K), lambda i, pos: (i, 0)),
                      pl.BlockSpec((tc, d), lambda i, pos: (i, 0)),
                      pl.BlockSpec((tc, d), lambda i, pos: (i, 0)),
                      pl.BlockSpec((None, 1, d), lambda i, pos: (i // per_batch, 0, 0))],
            out_specs=pl.BlockSpec((tc, d), lambda i, pos: (i, 0)),
            scratch_shapes=[pltpu.VMEM((TOP_K, tc, w), I32), pltpu.SemaphoreType.DMA(())]),
        out_shape=jax.ShapeDtypeStruct((n, d), F32),
        compiler_params=_params("arbitrary"), name="moe_combine",
    )(pos_flat, ys, wts, x2, shared, gate)


def moe_layout(ids, rank, counts, n_tiles):
    tiles = (counts + MOE_TILE - 1) // MOE_TILE
    tile_end = jnp.cumsum(tiles)
    offset = (tile_end - tiles) * MOE_TILE
    pos = jnp.take(offset, ids) + rank
    n_used = tile_end[-1]
    tile_ids = jnp.minimum(jnp.arange(n_tiles, dtype=I32), n_used - 1)
    tile_expert = jnp.searchsorted(tile_end, tile_ids, side="right").astype(I32)
    return pos.reshape(-1).astype(I32), tile_expert, n_used.reshape(1).astype(I32)


def _pad_cols(w, width):
    return jnp.pad(w, ((0, 0), (0, width - w.shape[1])))


def kernel(x, c, positions, ab_w_in, ab_w_out, gmlp_ln_g, gmlp_ln_b, gmlp_w_s, gmlp_b_s, sb_w_qkv, sb_w_out,
           norm_mix_g, ada_mix_w, ada_mix_b, norm_ffn_g, ada_ffn_w, ada_ffn_b, router_w, router_bias,
           expert_w_gate, expert_w_up, expert_w_down, shared_w_gate, shared_w_up, shared_w_down, final_norm_g):
    bsz, seq, d = x.shape
    depth = norm_mix_g.shape[0]
    a_width = gmlp_ln_g.shape[1]
    b_width = ab_w_out.shape[1] - a_width
    b_heads = b_width // HEAD_DIM
    c_heads = sb_w_out.shape[1] // HEAD_DIM
    n_sel = min(TOPK_MAX, seq // 4)
    main_width = 2 * a_width + b_width + 2 * HEAD_DIM
    idx_width = IDX_HEADS * IDX_DIM + LANES

    mod_mix = ada_modulation_all(c, ada_mix_w, ada_mix_b)
    mod_ffn = ada_modulation_all(c, ada_ffn_w, ada_ffn_b)
    tables = rope_tables(positions)
    x2 = x.reshape(bsz * seq, d)

    for layer in range(depth):
        j = layer // 2
        shift, scale, gate = mod_mix[layer, :, 0], mod_mix[layer, :, 1], mod_mix[layer, :, 2]
        h = norm_modulate(x2, norm_mix_g[layer], scale, shift, seq)
        if layer % 2 == 0:
            w_in = ab_w_in[j]
            proj = matmul(h, w_in[:, :main_width].astype(BF16), BF16, tn=640 if main_width % 640 == 0 else 128)
            proj_idx = matmul(h, _pad_cols(w_in[:, main_width:], idx_width).astype(BF16), F32, tn=idx_width)
            q, k, qi, kia, kib, w_idx = rope_prepare(proj, proj_idx, tables, 2 * a_width, b_heads)
            buf = gmlp_mixer(proj, a_width, gmlp_ln_g[j], gmlp_ln_b[j], gmlp_w_s[j], gmlp_b_s[j],
                             a_width + b_width)
            v_col_block = (2 * a_width + b_width + HEAD_DIM) // HEAD_DIM
            mixed = dsa_mixer(qi, w_idx, kia, kib, q, k, proj, v_col_block, buf, seq, n_sel)
            x2 = matmul_residual(mixed, ab_w_out[j].astype(BF16), x2, gate, seq)
        else:
            c_width = c_heads * HEAD_DIM
            col_scale = jnp.where(jnp.arange(3 * c_width) < c_width, SB_LOGIT_SCALE, 1.0).astype(F32)
            qkv = matmul(h, (sb_w_qkv[j] * col_scale).astype(BF16), BF16)
            o = stick_breaking_mixer(qkv, seq, c_heads)
            x2 = matmul_residual(o, sb_w_out[j].astype(BF16), x2, gate, seq)

        shift, scale, gate = mod_ffn[layer, :, 0], mod_ffn[layer, :, 1], mod_ffn[layer, :, 2]
        h, h_packed = norm_modulate(x2, norm_ffn_g[layer], scale, shift, seq, pack=True)
        ids, wts, rank, counts = moe_router(h, router_w[layer], router_bias[layer])
        n_tiles = bsz * seq * TOP_K // MOE_TILE + router_w.shape[2]
        pos, tile_expert, n_used = moe_layout(ids, rank, counts, n_tiles)
        xs = moe_dispatch(h_packed, pos, n_tiles * MOE_TILE)
        ys = moe_grouped(xs, tile_expert, n_used, expert_w_gate[layer].astype(BF16),
                         expert_w_up[layer].astype(BF16), expert_w_down[layer].astype(BF16))
        shared = shared_expert(h, shared_w_gate[layer].astype(BF16), shared_w_up[layer].astype(BF16),
                               shared_w_down[layer].astype(BF16))
        x2 = moe_combine(ys, pos, wts.T, x2, shared, gate, seq)

    return final_norm(x2, final_norm_g).reshape(bsz, seq, d)
```

```python
import functools

import jax
import jax.numpy as jnp
from jax import lax
from jax.experimental import pallas as pl
from jax.experimental.pallas import tpu as pltpu

F32 = jnp.float32
BF16 = jnp.bfloat16
I32 = jnp.int32

EPS = 1e-6
ROPE_THETA = 10000.0
LANES = 128
HEAD_DIM = 128
A_GROUPS = 8
A_CHUNK = 128
IDX_HEADS = 16
IDX_DIM = 64
TOPK_MAX = 256
N_GROUPS = 8
TOPK_GROUPS = 4
TOP_K = 8
ROUTED_SCALE = 2.5
VMEM_LIMIT = 56 * 1024 * 1024
INT_MIN = -(2 ** 31)
MASKED = -1e30
LOG2E = 1.4426950408889634


def _params(*sem):
    return pltpu.CompilerParams(dimension_semantics=sem, vmem_limit_bytes=VMEM_LIMIT)


def _mod_kernel(c_ref, w_ref, b_ref, o_ref):
    c = c_ref[...]
    s = c * jax.nn.sigmoid(c)
    o_ref[...] = jnp.dot(s.astype(BF16), w_ref[...].astype(BF16), preferred_element_type=F32) + b_ref[...]


def ada_modulation_all(c, w, b):
    bsz, d = c.shape
    n_layers, _, n_out = w.shape
    rows = 8
    c_pad = jnp.zeros((rows, d), F32).at[:bsz].set(c)
    tn = min(512, n_out)
    out = pl.pallas_call(
        _mod_kernel,
        grid=(n_layers, n_out // tn),
        in_specs=[pl.BlockSpec((rows, d), lambda l, j: (0, 0)),
                  pl.BlockSpec((None, d, tn), lambda l, j: (l, 0, j)),
                  pl.BlockSpec((None, 1, tn), lambda l, j: (l, 0, j))],
        out_specs=pl.BlockSpec((None, rows, tn), lambda l, j: (l, 0, j)),
        out_shape=jax.ShapeDtypeStruct((n_layers, rows, n_out), F32),
        compiler_params=_params("parallel", "parallel"), name="ada_modulation",
    )(c_pad, w, b.reshape(n_layers, 1, n_out))
    return out[:, :bsz].reshape(n_layers, bsz, 3, 1, d)


def _norm_mod_kernel(x_ref, g_ref, sc_ref, sh_ref, o_ref):
    x = x_ref[...]
    y = x * lax.rsqrt(jnp.mean(x * x, axis=-1, keepdims=True) + EPS)
    o_ref[...] = ((y * g_ref[...]) * (1.0 + sc_ref[...]) + sh_ref[...]).astype(o_ref.dtype)


def _norm_mod_pack_kernel(x_ref, g_ref, sc_ref, sh_ref, o_ref, p_ref):
    x = x_ref[...]
    y = x * lax.rsqrt(jnp.mean(x * x, axis=-1, keepdims=True) + EPS)
    h = (y * g_ref[...]) * (1.0 + sc_ref[...]) + sh_ref[...]
    o_ref[...] = h.astype(o_ref.dtype)
    _store_slabs(p_ref, _pack_pairs(h))


def norm_modulate(x2, g, scale, shift, seq, pack=False):
    n, d = x2.shape
    tm = min(256, seq)
    per_batch = seq // tm
    row = lambda w: pl.BlockSpec((tm, w), lambda i: (i, 0))
    out_specs, out_shape = row(d), jax.ShapeDtypeStruct((n, d), BF16)
    if pack:
        slabs = d // 2 // LANES
        out_specs = [out_specs, pl.BlockSpec((tm, slabs, LANES), lambda i: (i, 0, 0))]
        out_shape = [out_shape, jax.ShapeDtypeStruct((n, slabs, LANES), I32)]
    return pl.pallas_call(
        _norm_mod_pack_kernel if pack else _norm_mod_kernel,
        grid=(n // tm,),
        in_specs=[row(d),
                  pl.BlockSpec((1, d), lambda i: (0, 0)),
                  pl.BlockSpec((None, 1, d), lambda i: (i // per_batch, 0, 0)),
                  pl.BlockSpec((None, 1, d), lambda i: (i // per_batch, 0, 0))],
        out_specs=out_specs,
        out_shape=out_shape,
        compiler_params=_params("parallel"), name="norm_modulate",
    )(x2, g.reshape(1, d), scale, shift)


def _final_norm_kernel(x_ref, g_ref, o_ref):
    x = x_ref[...]
    y = x * lax.rsqrt(jnp.mean(x * x, axis=-1, keepdims=True) + EPS)
    o_ref[...] = y * g_ref[...]


def final_norm(x2, g):
    n, d = x2.shape
    tm = min(256, n)
    return pl.pallas_call(
        _final_norm_kernel,
        grid=(n // tm,),
        in_specs=[pl.BlockSpec((tm, d), lambda i: (i, 0)), pl.BlockSpec((1, d), lambda i: (0, 0))],
        out_specs=pl.BlockSpec((tm, d), lambda i: (i, 0)),
        out_shape=jax.ShapeDtypeStruct((n, d), F32),
        compiler_params=_params("parallel"), name="final_norm",
    )(x2, g.reshape(1, d))


def _mm_kernel(a_ref, w_ref, o_ref):
    o_ref[...] = jnp.dot(a_ref[...], w_ref[...], preferred_element_type=F32).astype(o_ref.dtype)


def matmul(a, w, out_dtype, tm=1024, tn=512):
    m, k = a.shape
    n = w.shape[1]
    tm, tn = min(tm, m), min(tn, n)
    return pl.pallas_call(
        _mm_kernel,
        grid=(m // tm, n // tn),
        in_specs=[pl.BlockSpec((tm, k), lambda i, j: (i, 0)),
                  pl.BlockSpec((k, tn), lambda i, j: (0, j))],
        out_specs=pl.BlockSpec((tm, tn), lambda i, j: (i, j)),
        out_shape=jax.ShapeDtypeStruct((m, n), out_dtype),
        compiler_params=_params("parallel", "parallel"), name="projection",
    )(a, w)


def _mm_res_kernel(a_ref, w_ref, r_ref, g_ref, o_ref):
    acc = jnp.dot(a_ref[...], w_ref[...], preferred_element_type=F32)
    o_ref[...] = r_ref[...] + g_ref[...] * acc


def matmul_residual(a, w, res, gate, seq, tm=1024, tn=512):
    m, k = a.shape
    n = w.shape[1]
    tm, tn = min(tm, seq), min(tn, n)
    per_batch = seq // tm
    return pl.pallas_call(
        _mm_res_kernel,
        grid=(m // tm, n // tn),
        in_specs=[pl.BlockSpec((tm, k), lambda i, j: (i, 0)),
                  pl.BlockSpec((k, tn), lambda i, j: (0, j)),
                  pl.BlockSpec((tm, tn), lambda i, j: (i, j)),
                  pl.BlockSpec((None, 1, tn), lambda i, j: (i // per_batch, 0, j))],
        out_specs=pl.BlockSpec((tm, tn), lambda i, j: (i, j)),
        out_shape=jax.ShapeDtypeStruct((m, n), F32),
        compiler_params=_params("parallel", "parallel"), name="projection_residual",
    )(a, w, res, gate)


def _gelu(x):
    return 0.5 * x * (1.0 + lax.erf(x * (2.0 ** -0.5)))


def _gmlp_kernel(u_ref, v_ref, lng_ref, lnb_ref, ws_ref, bs_ref, o_ref, *, groups):
    u = _gelu(u_ref[...].astype(F32))
    v = _gelu(v_ref[...].astype(F32))
    mu = jnp.mean(v, axis=-1, keepdims=True)
    var = jnp.mean(jnp.square(v - mu), axis=-1, keepdims=True)
    vn = ((v - mu) * lax.rsqrt(var + EPS) * lng_ref[...] + lnb_ref[...]).astype(BF16)
    t = ws_ref.shape[1]
    gw = u.shape[1] // groups
    row = lax.broadcasted_iota(I32, (t, t), 0)
    col = lax.broadcasted_iota(I32, (t, t), 1)
    causal = col <= row
    for g in range(groups):
        w = jnp.where(causal, ws_ref[g], 0.0).astype(BF16)
        mixed = jnp.dot(w, vn[:, g * gw:(g + 1) * gw], preferred_element_type=F32) + bs_ref[:, g:g + 1]
        o_ref[:, g * gw:(g + 1) * gw] = (u[:, g * gw:(g + 1) * gw] * mixed).astype(o_ref.dtype)
    o_ref[:, u.shape[1]:] = jnp.zeros((t, o_ref.shape[1] - u.shape[1]), o_ref.dtype)


def gmlp_mixer(proj, a_width, ln_g, ln_b, w_s, b_s, out_width):
    n = proj.shape[0]
    groups, t, _ = w_s.shape
    return pl.pallas_call(
        functools.partial(_gmlp_kernel, groups=groups),
        grid=(n // t,),
        in_specs=[pl.BlockSpec((t, a_width), lambda i: (i, 0)),
                  pl.BlockSpec((t, a_width), lambda i: (i, 1)),
                  pl.BlockSpec((1, a_width), lambda i: (0, 0)),
                  pl.BlockSpec((1, a_width), lambda i: (0, 0)),
                  pl.BlockSpec((groups, t, t), lambda i: (0, 0, 0)),
                  pl.BlockSpec((t, groups), lambda i: (0, 0))],
        out_specs=pl.BlockSpec((t, out_width), lambda i: (i, 0)),
        out_shape=jax.ShapeDtypeStruct((n, out_width), BF16),
        compiler_params=_params("parallel"), name="gmlp_mixer",
    )(proj, proj, ln_g.reshape(1, a_width), ln_b.reshape(1, a_width), w_s, b_s.T)


def _rope_kernel(q_ref, k_ref, x_ref, cb_ref, sb_ref, ci_ref, si_ref, ck_ref, sk_ref,
                 qo_ref, ko_ref, qio_ref, kia_ref, kib_ref, wo_ref, *, heads, idx_pairs, q_scale, w_scale):
    cb, sb = cb_ref[...], sb_ref[...]
    half = HEAD_DIM // 2
    for h in range(heads):
        x = q_ref[:, h * HEAD_DIM:(h + 1) * HEAD_DIM].astype(F32)
        r = x * cb + pltpu.roll(x, half, 1) * sb
        qo_ref[:, h * HEAD_DIM:(h + 1) * HEAD_DIM] = (r * q_scale).astype(qo_ref.dtype)
    x = k_ref[...].astype(F32)
    ko_ref[...] = (x * cb + pltpu.roll(x, half, 1) * sb).astype(ko_ref.dtype)

    lane = lax.broadcasted_iota(I32, cb.shape, 1)
    first_half = (lane % IDX_DIM) < (IDX_DIM // 2)

    def rope_idx(x, c, s):
        rot = jnp.where(first_half, pltpu.roll(x, LANES - IDX_DIM // 2, 1), pltpu.roll(x, IDX_DIM // 2, 1))
        return x * c + rot * s

    ci, si = ci_ref[...], si_ref[...]
    for p in range(idx_pairs):
        x = x_ref[:, p * LANES:(p + 1) * LANES]
        qio_ref[:, p * LANES:(p + 1) * LANES] = rope_idx(x, ci, si).astype(qio_ref.dtype)
    tail = x_ref[:, idx_pairs * LANES:(idx_pairs + 1) * LANES]
    roped = rope_idx(tail, ck_ref[...], sk_ref[...])
    is_k = lane < IDX_DIM
    ka = jnp.where(is_k, roped, 0.0)
    kia_ref[...] = ka.astype(kia_ref.dtype)
    kib_ref[...] = pltpu.roll(ka, IDX_DIM, 1).astype(kib_ref.dtype)
    w = pltpu.roll(tail, LANES - IDX_DIM, 1)
    wo_ref[...] = jnp.where(lane < IDX_HEADS, w * w_scale, 0.0)


def rope_prepare(proj, proj_idx, tables, q_off, heads):
    n = proj.shape[0]
    tm = min(256, n)
    idx_pairs = IDX_HEADS * IDX_DIM // LANES
    qb = q_off // (heads * HEAD_DIM)
    kb = (q_off + heads * HEAD_DIM) // HEAD_DIM
    row = lambda w: pl.BlockSpec((tm, w), lambda i: (i, 0))
    kernel = functools.partial(_rope_kernel, heads=heads, idx_pairs=idx_pairs, q_scale=HEAD_DIM ** -0.5,
                               w_scale=IDX_HEADS ** -0.5 * IDX_DIM ** -0.5)
    return pl.pallas_call(
        kernel,
        grid=(n // tm,),
        in_specs=[pl.BlockSpec((tm, heads * HEAD_DIM), lambda i: (i, qb)),
                  pl.BlockSpec((tm, HEAD_DIM), lambda i: (i, kb)),
                  row(proj_idx.shape[1])] + [row(LANES)] * 6,
        out_specs=[row(heads * HEAD_DIM), row(HEAD_DIM), row(idx_pairs * LANES), row(LANES), row(LANES), row(LANES)],
        out_shape=[jax.ShapeDtypeStruct((n, heads * HEAD_DIM), BF16),
                   jax.ShapeDtypeStruct((n, HEAD_DIM), BF16),
                   jax.ShapeDtypeStruct((n, idx_pairs * LANES), BF16),
                   jax.ShapeDtypeStruct((n, LANES), BF16),
                   jax.ShapeDtypeStruct((n, LANES), BF16),
                   jax.ShapeDtypeStruct((n, LANES), F32)],
        compiler_params=_params("parallel"), name="rope_prepare",
    )(proj, proj, proj_idx, *tables)


def rope_tables(positions):
    pos = positions.reshape(-1).astype(F32)[:, None]

    def cs(dim):
        inv = ROPE_THETA ** (-jnp.arange(0, dim, 2, dtype=F32) / dim)
        ang = pos * inv
        return jnp.cos(ang), jnp.sin(ang)

    cb, sb = cs(HEAD_DIM)
    ci, si = cs(IDX_DIM)
    ones, zeros = jnp.ones_like(cb), jnp.zeros_like(cb)
    return (jnp.concatenate([cb, cb], 1), jnp.concatenate([-sb, sb], 1),
            jnp.concatenate([ci, ci, ci, ci], 1), jnp.concatenate([-si, si, -si, si], 1),
            jnp.concatenate([ci, ci, ones], 1), jnp.concatenate([-si, si, zeros], 1))


def _sort_key(x):
    x = jnp.where(x == 0.0, 0.0, x)
    bits = pltpu.bitcast(x, I32)
    return jnp.where(bits < 0, bits ^ 0x7FFFFFFF, bits)


def _dsa_kernel(qi_ref, w_ref, kia_ref, kib_ref, q_ref, k_ref, v_ref, buf_ref, o_ref, key_ref,
                *, tq, tk, heads, n_sel):
    del buf_ref
    qt = pl.program_id(1)
    q_lo = qt * tq
    n_kb = (q_lo + tq + tk - 1) // tk
    sub = tk // LANES
    row_pos = q_lo + lax.broadcasted_iota(I32, (tq, LANES), 0)
    lane = lax.broadcasted_iota(I32, (tq, LANES), 1)

    w = w_ref[...]
    w_cols = [jnp.broadcast_to(w[:, h:h + 1], (tq, tk)) for h in range(IDX_HEADS)]

    def score_block(kb, carry):
        k0 = pl.multiple_of(kb * tk, tk)
        ka = kia_ref[pl.ds(k0, tk), :]
        kb_ = kib_ref[pl.ds(k0, tk), :]
        acc = jnp.zeros((tq, tk), F32)
        for p in range(IDX_HEADS // 2):
            x = qi_ref[:, p * LANES:(p + 1) * LANES]
            ra = lax.dot_general(x, ka, (((1,), (1,)), ((), ())), preferred_element_type=F32)
            rb = lax.dot_general(x, kb_, (((1,), (1,)), ((), ())), preferred_element_type=F32)
            acc = acc + w_cols[2 * p] * jnp.maximum(ra, 0.0)
            acc = acc + w_cols[2 * p + 1] * jnp.maximum(rb, 0.0)
        key = _sort_key(acc)
        for j in range(sub):
            col_pos = k0 + j * LANES + lane
            key_ref[kb * sub + j] = jnp.where(col_pos <= row_pos, key[:, j * LANES:(j + 1) * LANES], INT_MIN)
        return carry

    lax.fori_loop(0, n_kb, score_block, 0)
    n_slabs = n_kb * sub

    def count(pred):
        def body(kb, acc):
            hits = [jnp.where(pred(key_ref[kb * sub + j], kb * sub + j), 1, 0) for j in range(sub)]
            while len(hits) > 1:
                hits = [a + b for a, b in zip(hits[::2], hits[1::2])]
            return acc + hits[0]
        part = lax.fori_loop(0, n_kb, body, jnp.zeros((tq, LANES), I32))
        return jnp.sum(part.astype(F32), axis=1, keepdims=True).astype(I32)

    def search_bit(i, u):
        cand = u | (1 << (31 - i))
        thr = jnp.broadcast_to(cand ^ INT_MIN, (tq, LANES))
        total = count(lambda kv, c: kv >= thr)
        return jnp.where(total >= n_sel, cand, u)

    u = lax.fori_loop(0, 32, search_bit, jnp.zeros((tq, 1), I32))
    thr = u ^ INT_MIN
    thr_b = jnp.broadcast_to(thr, (tq, LANES))
    n_gt = count(lambda kv, c: kv > thr_b)
    n_ge = count(lambda kv, c: kv >= thr_b)
    need = n_sel - n_gt
    tie_rows = jnp.logical_and(n_ge > n_sel, need > 0)

    seq_bits = max(1, (key_ref.shape[0] * LANES).bit_length())

    def tie_search():
        need_b = need

        def bit_step(i, j):
            cand = j | (1 << (seq_bits - 1 - i))
            cand_b = jnp.broadcast_to(cand, (tq, LANES))
            total = count(lambda kv, c: jnp.logical_and(kv == thr_b, c * LANES + lane < cand_b))
            return jnp.where(total <= need_b, cand, j)

        return lax.fori_loop(0, seq_bits, bit_step, jnp.zeros((tq, 1), I32))

    any_tie = jnp.max(jnp.where(tie_rows, 1.0, 0.0)) > 0.0
    bound = lax.cond(any_tie, tie_search, lambda: jnp.full((tq, 1), 2 ** 30, I32))
    bound = jnp.where(tie_rows, bound, 2 ** 30)
    bound_b = jnp.broadcast_to(bound, (tq, LANES))

    q_all = jnp.concatenate([q_ref[:, h * HEAD_DIM:(h + 1) * HEAD_DIM] for h in range(heads)], axis=0)

    def attn_block(kb, carry):
        m, l, acc = carry
        k0 = pl.multiple_of(kb * tk, tk)
        kk = k_ref[pl.ds(k0, tk), :]
        vv = v_ref[pl.ds(k0, tk), :]
        s = lax.dot_general(q_all, kk, (((1,), (1,)), ((), ())), preferred_element_type=F32)
        bias_cols = []
        for j in range(sub):
            kv = key_ref[kb * sub + j]
            col_pos = k0 + j * LANES + lane
            take = jnp.logical_or(kv > thr_b, jnp.logical_and(kv == thr_b, col_pos < bound_b))
            take = jnp.logical_and(take, col_pos <= row_pos)
            bias_cols.append(jnp.where(take, 0.0, MASKED))
        bias = jnp.concatenate(bias_cols, axis=1)
        s = s.reshape(heads, tq, tk) + bias[None]
        m_new = jnp.maximum(m, jnp.max(s, axis=-1, keepdims=True))
        alpha = jnp.exp(m - m_new)
        p = jnp.exp(s - m_new)
        l = alpha * l + jnp.sum(p, axis=-1, keepdims=True)
        pv = jnp.dot(p.reshape(heads * tq, tk).astype(BF16), vv, preferred_element_type=F32)
        acc = alpha * acc + pv.reshape(heads, tq, HEAD_DIM)
        return m_new, l, acc

    init = (jnp.full((heads, tq, 1), MASKED, F32), jnp.zeros((heads, tq, 1), F32),
            jnp.zeros((heads, tq, HEAD_DIM), F32))
    m, l, acc = lax.fori_loop(0, n_kb, attn_block, init)
    out = acc / l
    for h in range(heads):
        o_ref[:, h * HEAD_DIM:(h + 1) * HEAD_DIM] = out[h].astype(o_ref.dtype)


def dsa_mixer(qi, w_idx, kia, kib, q, k, proj, v_col_block, buf, seq, n_sel):
    n = q.shape[0]
    bsz = n // seq
    heads = q.shape[1] // HEAD_DIM
    tq = min(128, seq)
    tk = min(512, seq)
    nq = seq // tq
    kernel = functools.partial(_dsa_kernel, tq=tq, tk=tk, heads=heads, n_sel=n_sel)
    qrow = lambda w: pl.BlockSpec((tq, w), lambda b, i: (b * nq + i, 0))
    kv = lambda c: pl.BlockSpec((seq, LANES), lambda b, i: (b, c))
    return pl.pallas_call(
        kernel,
        grid=(bsz, nq),
        in_specs=[qrow(qi.shape[1]), qrow(LANES), kv(0), kv(0), qrow(q.shape[1]), kv(0), kv(v_col_block),
                  pl.BlockSpec(memory_space=pl.ANY)],
        out_specs=pl.BlockSpec((tq, heads * HEAD_DIM), lambda b, i: (b * nq + i, 1)),
        out_shape=jax.ShapeDtypeStruct(buf.shape, buf.dtype),
        scratch_shapes=[pltpu.VMEM((seq // LANES, tq, LANES), I32)],
        input_output_aliases={7: 0},
        compiler_params=_params("parallel", "arbitrary"), name="dsa_mixer",
    )(qi, w_idx, kia, kib, q, k, proj, buf)


SB_HEADS_PER_STEP = 4
SB_LOGIT_SCALE = HEAD_DIM ** -0.5 * LOG2E


def _sb_kernel(q_ref, k_ref, v_ref, o_ref, *, t, group):
    qt = pl.program_id(2)
    row = lax.broadcasted_iota(I32, (t, t), 0)
    col = lax.broadcasted_iota(I32, (t, t), 1)
    later = jnp.where(row > col, 1.0, 0.0).astype(BF16)
    strict = col < row

    def block(kb, masked, carry):
        k0 = pl.multiple_of(kb * t, t)
        heads = range(group)
        cols = [slice(h * HEAD_DIM, (h + 1) * HEAD_DIM) for h in heads]
        z = [lax.dot_general(q_ref[:, cols[h]], k_ref[pl.ds(k0, t), cols[h]], (((1,), (1,)), ((), ())),
                             preferred_element_type=F32) for h in heads]
        log_beta = [jnp.minimum(z[h], 0.0) - jnp.log(1.0 + jnp.exp2(-jnp.abs(z[h]))) * LOG2E for h in heads]
        log_fail = [log_beta[h] - z[h] for h in heads]
        if masked:
            log_fail = [jnp.where(strict, log_fail[h], 0.0) for h in heads]
        between = [jnp.dot(log_fail[h].astype(BF16), later, preferred_element_type=F32) + carry[h][1]
                   for h in heads]
        a = [jnp.exp2(log_beta[h] + between[h]) for h in heads]
        if masked:
            a = [jnp.where(strict, a[h], 0.0) for h in heads]
        acc = [carry[h][0] + jnp.dot(a[h].astype(BF16), v_ref[pl.ds(k0, t), cols[h]], preferred_element_type=F32)
               for h in heads]
        run = [between[h][:, 0:1] + log_fail[h][:, 0:1] for h in heads]
        return tuple((acc[h], run[h]) for h in heads)

    init = tuple((jnp.zeros((t, HEAD_DIM), F32), jnp.zeros((t, 1), F32)) for _ in range(group))
    carry = block(qt, True, init)
    carry = lax.fori_loop(0, qt, lambda i, c: block(qt - 1 - i, False, c), carry)
    for h in range(group):
        o_ref[:, h * HEAD_DIM:(h + 1) * HEAD_DIM] = carry[h][0].astype(o_ref.dtype)


def stick_breaking_mixer(qkv, seq, heads):
    n = qkv.shape[0]
    bsz = n // seq
    t = min(256, seq)
    nq = seq // t
    group = SB_HEADS_PER_STEP
    hg = heads // group
    width = group * HEAD_DIM
    kernel = functools.partial(_sb_kernel, t=t, group=group)
    return pl.pallas_call(
        kernel,
        grid=(bsz, hg, nq),
        in_specs=[pl.BlockSpec((t, width), lambda b, h, i: (b * nq + i, h)),
                  pl.BlockSpec((seq, width), lambda b, h, i: (b, hg + h)),
                  pl.BlockSpec((seq, width), lambda b, h, i: (b, 2 * hg + h))],
        out_specs=pl.BlockSpec((t, width), lambda b, h, i: (b * nq + i, h)),
        out_shape=jax.ShapeDtypeStruct((n, heads * HEAD_DIM), BF16),
        compiler_params=_params("parallel", "parallel", "arbitrary"), name="stick_breaking",
    )(qkv, qkv, qkv)


def _router_kernel(h_ref, wr_ref, b_ref, s_ref, w_ref, c_ref, cnt_ref, *, n_experts):
    per_group = n_experts // N_GROUPS
    tm = h_ref.shape[0]
    logits = lax.dot_general(wr_ref[...], h_ref[...], (((1,), (1,)), ((), ())), preferred_element_type=F32)
    scores = jax.nn.sigmoid(logits)
    choice = (scores + b_ref[...]).reshape(N_GROUPS, per_group, tm)
    s3 = scores.reshape(N_GROUPS, per_group, tm)
    neg = -jnp.inf
    in_group = lax.broadcasted_iota(I32, choice.shape, 1)
    m1 = jnp.max(choice, axis=1, keepdims=True)
    first = jnp.min(jnp.where(choice == m1, in_group, per_group), axis=1, keepdims=True)
    m2 = jnp.max(jnp.where(in_group == first, neg, choice), axis=1, keepdims=True)
    group_score = m1 + m2
    gid = lax.broadcasted_iota(I32, group_score.shape, 0)
    group_sel = jnp.zeros(group_score.shape, jnp.bool_)
    for _ in range(TOPK_GROUPS):
        m = jnp.max(group_score, axis=0, keepdims=True)
        f = jnp.min(jnp.where(group_score == m, gid, N_GROUPS), axis=0, keepdims=True)
        hit = gid == f
        group_sel = jnp.logical_or(group_sel, hit)
        group_score = jnp.where(hit, neg, group_score)
    cand = jnp.where(group_sel, choice, neg)
    eid = lax.broadcasted_iota(I32, choice.shape, 0) * per_group + in_group
    sel = jnp.zeros(choice.shape, jnp.bool_)
    hits = []
    for _ in range(TOP_K):
        m = jnp.max(jnp.max(cand, axis=1, keepdims=True), axis=0, keepdims=True)
        f = jnp.min(jnp.min(jnp.where(cand == m, eid, n_experts), axis=1, keepdims=True), axis=0, keepdims=True)
        hit = eid == f
        hits.append((hit, f))
        sel = jnp.logical_or(sel, hit)
        cand = jnp.where(hit, neg, cand)
    top_w = jnp.where(sel, s3, 0.0)
    total = jnp.sum(jnp.sum(top_w, axis=1, keepdims=True), axis=0, keepdims=True)
    gates = top_w / total * ROUTED_SCALE

    @pl.when(pl.program_id(0) == 0)
    def _():
        cnt_ref[...] = jnp.zeros_like(cnt_ref)

    sel_f = jnp.where(sel, 1.0, 0.0).reshape(n_experts, tm)
    row = lax.broadcasted_iota(I32, (tm, tm), 0)
    col = lax.broadcasted_iota(I32, (tm, tm), 1)
    before = jnp.where(row < col, 1.0, 0.0).astype(BF16)
    prefix = jnp.dot(sel_f.astype(BF16), before, preferred_element_type=F32)
    rank_all = (prefix + cnt_ref[:, 0:1]).reshape(choice.shape)

    def pick(hit, val):
        return jnp.sum(jnp.sum(jnp.where(hit, val, 0.0), axis=1, keepdims=True), axis=0, keepdims=True)

    for k, (hit, f) in enumerate(hits):
        rank = pick(hit, rank_all).reshape(1, tm).astype(I32)
        s_ref[k:k + 1, :] = rank | lax.shift_left(f.reshape(1, tm), SLOT_RANK_BITS)
        w_ref[k:k + 1, :] = pick(hit, gates).reshape(1, tm)
    cnt_ref[...] = cnt_ref[...] + jnp.sum(sel_f, axis=1, keepdims=True)
    c_ref[...] = cnt_ref[...].astype(I32)


def moe_router(h, w_router, bias):
    n, d = h.shape
    e = w_router.shape[1]
    tm = min(512, n)
    top = lambda dt: jax.ShapeDtypeStruct((TOP_K, n), dt)
    top_spec = pl.BlockSpec((TOP_K, tm), lambda i: (0, i))
    slots, wts, counts = pl.pallas_call(
        functools.partial(_router_kernel, n_experts=e),
        grid=(n // tm,),
        in_specs=[pl.BlockSpec((tm, d), lambda i: (i, 0)),
                  pl.BlockSpec((e, d), lambda i: (0, 0)),
                  pl.BlockSpec((e, 1), lambda i: (0, 0))],
        out_specs=[top_spec, top_spec, pl.BlockSpec((e, LANES), lambda i: (0, 0))],
        out_shape=[top(I32), top(F32), jax.ShapeDtypeStruct((e, LANES), I32)],
        scratch_shapes=[pltpu.VMEM((e, LANES), F32)],
        compiler_params=_params("arbitrary"), name="moe_router",
    )(h, w_router.T.astype(BF16), bias.reshape(e, 1))
    return slots.reshape(-1), wts, counts[:, 0]


MOE_TILE = 512
DISPATCH_TOKENS = 256
COMBINE_TOKENS = 128
DMA_ISSUE_UNROLL = 8


def _pack_pairs(x):
    w = x.shape[1] // 2
    lo = pltpu.bitcast(x[:, :w].astype(BF16).astype(F32), I32)
    hi = pltpu.bitcast(x[:, w:].astype(BF16).astype(F32), I32)
    return lax.shift_right_logical(lo, 16) | hi


def _unpack_pairs(p):
    lo = pltpu.bitcast(lax.shift_left(p, 16), F32)
    hi = pltpu.bitcast(p & jnp.int32(-65536), F32)
    return lo, hi


def _store_slabs(ref, x):
    r, n_slabs, _ = ref.shape
    flat = ref.reshape(r * n_slabs, LANES)
    for s in range(n_slabs):
        flat[pl.ds(s, r, stride=n_slabs), :] = x[:, s * LANES:(s + 1) * LANES]


def _load_slab_column(ref, s):
    r, n_slabs, _ = ref.shape
    return ref.reshape(r * n_slabs, LANES)[pl.ds(s, r, stride=n_slabs), :]


def _load_slabs(ref):
    return jnp.concatenate([_load_slab_column(ref, s) for s in range(ref.shape[1])], axis=1)


SLOT_RANK_BITS = 20


def _slot_row(code, off_ref):
    return off_ref[lax.shift_right_logical(code, SLOT_RANK_BITS)] + (code & ((1 << SLOT_RANK_BITS) - 1))


def _dispatch_kernel(slot_ref, off_ref, hp_ref, xs_in_ref, xs_ref, sem, *, td, n_tok):
    del xs_in_ref
    base = pl.program_id(0) * td

    for k in range(TOP_K):
        def issue(t, carry, k=k):
            row = _slot_row(slot_ref[k * n_tok + base + t], off_ref)
            pltpu.make_async_copy(hp_ref.at[t], xs_ref.at[row], sem).start()
            return carry

        lax.fori_loop(0, td, issue, 0, unroll=DMA_ISSUE_UNROLL)
    for k in range(TOP_K):
        pltpu.make_async_copy(hp_ref, xs_ref.at[pl.ds(0, td)], sem).wait()


def moe_dispatch(hp, slots, offsets, n_rows):
    n, s, _ = hp.shape
    td = min(DISPATCH_TOKENS, n)
    kernel = functools.partial(_dispatch_kernel, td=td, n_tok=n)
    return pl.pallas_call(
        kernel,
        grid_spec=pltpu.PrefetchScalarGridSpec(
            num_scalar_prefetch=2,
            grid=(n // td,),
            in_specs=[pl.BlockSpec((td, s, LANES), lambda i, *_: (i, 0, 0)),
                      pl.BlockSpec(memory_space=pl.ANY)],
            out_specs=pl.BlockSpec(memory_space=pl.ANY),
            scratch_shapes=[pltpu.SemaphoreType.DMA(())]),
        out_shape=jax.ShapeDtypeStruct((n_rows, s, LANES), I32),
        input_output_aliases={3: 0},
        compiler_params=_params("arbitrary"), name="moe_dispatch",
    )(slots, offsets, hp, jnp.zeros((n_rows, s, LANES), I32))


def _swiglu(lo, hi, wg_ref, wu_ref, wd_ref):
    w = lo.shape[1]
    g = (jnp.dot(lo, wg_ref[:w, :], preferred_element_type=F32)
         + jnp.dot(hi, wg_ref[w:, :], preferred_element_type=F32))
    u = (jnp.dot(lo, wu_ref[:w, :], preferred_element_type=F32)
         + jnp.dot(hi, wu_ref[w:, :], preferred_element_type=F32))
    act = g * jax.nn.sigmoid(g) * u
    return jnp.dot(act.astype(BF16), wd_ref[...], preferred_element_type=F32)


def _grouped_kernel(te_ref, nu_ref, x_ref, wg_ref, wu_ref, wd_ref, y_ref):
    del te_ref

    @pl.when(pl.program_id(0) < nu_ref[0])
    def _():
        lo, hi = _unpack_pairs(_load_slabs(x_ref))
        _store_slabs(y_ref, _pack_pairs(_swiglu(lo.astype(BF16), hi.astype(BF16), wg_ref, wu_ref, wd_ref)))


def moe_grouped(xs, tile_expert, n_used, wg, wu, wd):
    p, s, _ = xs.shape
    e, d, f = wg.shape
    tm = MOE_TILE
    row = lambda i, te, nu: (jnp.minimum(i, nu[0] - 1), 0, 0)
    return pl.pallas_call(
        _grouped_kernel,
        grid_spec=pltpu.PrefetchScalarGridSpec(
            num_scalar_prefetch=2,
            grid=(p // tm,),
            in_specs=[pl.BlockSpec((tm, s, LANES), row),
                      pl.BlockSpec((None, d, f), lambda i, te, nu: (te[i], 0, 0)),
                      pl.BlockSpec((None, d, f), lambda i, te, nu: (te[i], 0, 0)),
                      pl.BlockSpec((None, f, d), lambda i, te, nu: (te[i], 0, 0))],
            out_specs=pl.BlockSpec((tm, s, LANES), row)),
        out_shape=jax.ShapeDtypeStruct((p, s, LANES), I32),
        compiler_params=_params("arbitrary"), name="moe_grouped",
    )(tile_expert, n_used, xs, wg, wu, wd)


def _shared_kernel(h_ref, wg_ref, wu_ref, wd_ref, o_ref):
    x = h_ref[...]
    w = x.shape[1] // 2
    o_ref[...] = _swiglu(x[:, :w], x[:, w:], wg_ref, wu_ref, wd_ref).astype(o_ref.dtype)


def shared_expert(h, wg, wu, wd):
    n, d = h.shape
    f = wg.shape[1]
    tm = min(512, n)
    return pl.pallas_call(
        _shared_kernel,
        grid=(n // tm,),
        in_specs=[pl.BlockSpec((tm, d), lambda i: (i, 0)),
                  pl.BlockSpec((d, f), lambda i: (0, 0)),
                  pl.BlockSpec((d, f), lambda i: (0, 0)),
                  pl.BlockSpec((f, d), lambda i: (0, 0))],
        out_specs=pl.BlockSpec((tm, d), lambda i: (i, 0)),
        out_shape=jax.ShapeDtypeStruct((n, d), BF16),
        compiler_params=_params("parallel"), name="shared_expert",
    )(h, wg, wu, wd)


def _combine_kernel(slot_ref, off_ref, ys_ref, w_ref, x_ref, sh_ref, g_ref, o_ref, rows_ref, sem, *, tc, n_tok):
    base = pl.program_id(0) * tc

    for k in range(TOP_K):
        def issue(t, carry, k=k):
            row = _slot_row(slot_ref[k * n_tok + base + t], off_ref)
            pltpu.make_async_copy(ys_ref.at[row], rows_ref.at[k, t], sem).start()
            return carry

        lax.fori_loop(0, tc, issue, 0, unroll=DMA_ISSUE_UNROLL)
    for k in range(TOP_K):
        pltpu.make_async_copy(ys_ref.at[pl.ds(0, tc)], rows_ref.at[k], sem).wait()

    half = x_ref.shape[1] // 2
    w_cols = [w_ref[:, k:k + 1] for k in range(TOP_K)]
    for s in range(rows_ref.shape[2]):
        acc_lo = jnp.zeros((tc, LANES), F32)
        acc_hi = jnp.zeros((tc, LANES), F32)
        for k in range(TOP_K):
            lo, hi = _unpack_pairs(_load_slab_column(rows_ref.at[k], s))
            acc_lo = acc_lo + w_cols[k] * lo
            acc_hi = acc_hi + w_cols[k] * hi
        for acc, c0 in ((acc_lo, s * LANES), (acc_hi, half + s * LANES)):
            cols = slice(c0, c0 + LANES)
            o_ref[:, cols] = x_ref[:, cols] + g_ref[:, cols] * (acc + sh_ref[:, cols].astype(F32))


def moe_combine(ys, slots, offsets, wts, x2, shared, gate, seq):
    n, d = x2.shape
    s = ys.shape[1]
    tc = min(COMBINE_TOKENS, seq)
    per_batch = seq // tc
    kernel = functools.partial(_combine_kernel, tc=tc, n_tok=n)
    return pl.pallas_call(
        kernel,
        grid_spec=pltpu.PrefetchScalarGridSpec(
            num_scalar_prefetch=2,
            grid=(n // tc,),
            in_specs=[pl.BlockSpec(memory_space=pl.ANY),
                      pl.BlockSpec((tc, TOP_K), lambda i, *_: (i, 0)),
                      pl.BlockSpec((tc, d), lambda i, *_: (i, 0)),
                      pl.BlockSpec((tc, d), lambda i, *_: (i, 0)),
                      pl.BlockSpec((None, 1, d), lambda i, *_: (i // per_batch, 0, 0))],
            out_specs=pl.BlockSpec((tc, d), lambda i, *_: (i, 0)),
            scratch_shapes=[pltpu.VMEM((TOP_K, tc, s, LANES), I32), pltpu.SemaphoreType.DMA(())]),
        out_shape=jax.ShapeDtypeStruct((n, d), F32),
        compiler_params=_params("arbitrary"), name="moe_combine",
    )(slots, offsets, ys, wts, x2, shared, gate)


def moe_layout(counts, n_tiles):
    e = counts.shape[0]
    tiles = (counts + MOE_TILE - 1) // MOE_TILE
    earlier = jnp.arange(e)[None, :] <= jnp.arange(e)[:, None]
    tile_end = jnp.sum(jnp.where(earlier, tiles[None, :], 0), axis=1)
    offsets = (tile_end - tiles) * MOE_TILE
    n_used = tile_end[-1]
    tile_ids = jnp.minimum(jnp.arange(n_tiles, dtype=I32), n_used - 1)
    tile_expert = jnp.sum(tile_end[None, :] <= tile_ids[:, None], axis=1)
    return offsets.astype(I32), tile_expert.astype(I32), n_used.reshape(1).astype(I32)


def _pad_cols(w, width):
    return jnp.pad(w, ((0, 0), (0, width - w.shape[1])))


def kernel(x, c, positions, ab_w_in, ab_w_out, gmlp_ln_g, gmlp_ln_b, gmlp_w_s, gmlp_b_s, sb_w_qkv, sb_w_out,
           norm_mix_g, ada_mix_w, ada_mix_b, norm_ffn_g, ada_ffn_w, ada_ffn_b, router_w, router_bias,
           expert_w_gate, expert_w_up, expert_w_down, shared_w_gate, shared_w_up, shared_w_down, final_norm_g):
    bsz, seq, d = x.shape
    depth = norm_mix_g.shape[0]
    a_width = gmlp_ln_g.shape[1]
    b_width = ab_w_out.shape[1] - a_width
    b_heads = b_width // HEAD_DIM
    c_heads = sb_w_out.shape[1] // HEAD_DIM
    n_sel = min(TOPK_MAX, seq // 4)
    main_width = 2 * a_width + b_width + 2 * HEAD_DIM
    idx_width = IDX_HEADS * IDX_DIM + LANES

    mod_mix = ada_modulation_all(c, ada_mix_w, ada_mix_b)
    mod_ffn = ada_modulation_all(c, ada_ffn_w, ada_ffn_b)
    tables = rope_tables(positions)
    x2 = x.reshape(bsz * seq, d)

    for layer in range(depth):
        j = layer // 2
        shift, scale, gate = mod_mix[layer, :, 0], mod_mix[layer, :, 1], mod_mix[layer, :, 2]
        h = norm_modulate(x2, norm_mix_g[layer], scale, shift, seq)
        if layer % 2 == 0:
            w_in = ab_w_in[j]
            proj = matmul(h, w_in[:, :main_width].astype(BF16), BF16, tn=640 if main_width % 640 == 0 else 128)
            proj_idx = matmul(h, _pad_cols(w_in[:, main_width:], idx_width).astype(BF16), F32, tn=idx_width)
            q, k, qi, kia, kib, w_idx = rope_prepare(proj, proj_idx, tables, 2 * a_width, b_heads)
            buf = gmlp_mixer(proj, a_width, gmlp_ln_g[j], gmlp_ln_b[j], gmlp_w_s[j], gmlp_b_s[j],
                             a_width + b_width)
            v_col_block = (2 * a_width + b_width + HEAD_DIM) // HEAD_DIM
            mixed = dsa_mixer(qi, w_idx, kia, kib, q, k, proj, v_col_block, buf, seq, n_sel)
            x2 = matmul_residual(mixed, ab_w_out[j].astype(BF16), x2, gate, seq)
        else:
            c_width = c_heads * HEAD_DIM
            col_scale = jnp.where(jnp.arange(3 * c_width) < c_width, SB_LOGIT_SCALE, 1.0).astype(F32)
            qkv = matmul(h, (sb_w_qkv[j] * col_scale).astype(BF16), BF16)
            o = stick_breaking_mixer(qkv, seq, c_heads)
            x2 = matmul_residual(o, sb_w_out[j].astype(BF16), x2, gate, seq)

        shift, scale, gate = mod_ffn[layer, :, 0], mod_ffn[layer, :, 1], mod_ffn[layer, :, 2]
        h, h_packed = norm_modulate(x2, norm_ffn_g[layer], scale, shift, seq, pack=True)
        slots, wts, counts = moe_router(h, router_w[layer], router_bias[layer])
        n_tiles = bsz * seq * TOP_K // MOE_TILE + router_w.shape[2]
        offsets, tile_expert, n_used = moe_layout(counts, n_tiles)
        xs = moe_dispatch(h_packed, slots, offsets, n_tiles * MOE_TILE)
        ys = moe_grouped(xs, tile_expert, n_used, expert_w_gate[layer].astype(BF16),
                         expert_w_up[layer].astype(BF16), expert_w_down[layer].astype(BF16))
        shared = shared_expert(h, shared_w_gate[layer].astype(BF16), shared_w_up[layer].astype(BF16),
                               shared_w_down[layer].astype(BF16))
        x2 = moe_combine(ys, slots, offsets, wts.T, x2, shared, gate, seq)

    return final_norm(x2, final_norm_g).reshape(bsz, seq, d)
```

```python
import functools

import jax
import jax.numpy as jnp
from jax import lax
from jax.experimental import pallas as pl
from jax.experimental.pallas import tpu as pltpu

F32 = jnp.float32
BF16 = jnp.bfloat16
I32 = jnp.int32

EPS = 1e-6
ROPE_THETA = 10000.0
LANES = 128
HEAD_DIM = 128
A_GROUPS = 8
A_CHUNK = 128
IDX_HEADS = 16
IDX_DIM = 64
TOPK_MAX = 256
N_GROUPS = 8
TOPK_GROUPS = 4
TOP_K = 8
ROUTED_SCALE = 2.5
VMEM_LIMIT = 56 * 1024 * 1024
INT_MIN = -(2 ** 31)
MASKED = -1e30
LOG2E = 1.4426950408889634


def _params(*sem):
    return pltpu.CompilerParams(dimension_semantics=sem, vmem_limit_bytes=VMEM_LIMIT)


def _mod_kernel(c_ref, w_ref, b_ref, o_ref):
    c = c_ref[...]
    s = c * jax.nn.sigmoid(c)
    o_ref[...] = jnp.dot(s.astype(BF16), w_ref[...].astype(BF16), preferred_element_type=F32) + b_ref[...]


def ada_modulation_all(c, w, b):
    bsz, d = c.shape
    n_layers, _, n_out = w.shape
    rows = 8
    c_pad = jnp.zeros((rows, d), F32).at[:bsz].set(c)
    tn = min(512, n_out)
    out = pl.pallas_call(
        _mod_kernel,
        grid=(n_layers, n_out // tn),
        in_specs=[pl.BlockSpec((rows, d), lambda l, j: (0, 0)),
                  pl.BlockSpec((None, d, tn), lambda l, j: (l, 0, j)),
                  pl.BlockSpec((None, 1, tn), lambda l, j: (l, 0, j))],
        out_specs=pl.BlockSpec((None, rows, tn), lambda l, j: (l, 0, j)),
        out_shape=jax.ShapeDtypeStruct((n_layers, rows, n_out), F32),
        compiler_params=_params("parallel", "parallel"), name="ada_modulation",
    )(c_pad, w, b.reshape(n_layers, 1, n_out))
    return out[:, :bsz].reshape(n_layers, bsz, 3, 1, d)


def _norm_mod_kernel(x_ref, g_ref, sc_ref, sh_ref, o_ref):
    x = x_ref[...]
    y = x * lax.rsqrt(jnp.mean(x * x, axis=-1, keepdims=True) + EPS)
    o_ref[...] = ((y * g_ref[...]) * (1.0 + sc_ref[...]) + sh_ref[...]).astype(o_ref.dtype)


def _norm_mod_pack_kernel(x_ref, g_ref, sc_ref, sh_ref, o_ref, p_ref):
    x = x_ref[...]
    y = x * lax.rsqrt(jnp.mean(x * x, axis=-1, keepdims=True) + EPS)
    h = (y * g_ref[...]) * (1.0 + sc_ref[...]) + sh_ref[...]
    o_ref[...] = h.astype(o_ref.dtype)
    _store_slabs(p_ref, _pack_pairs(h))


def norm_modulate(x2, g, scale, shift, seq, pack=False):
    n, d = x2.shape
    tm = min(256, seq)
    per_batch = seq // tm
    row = lambda w: pl.BlockSpec((tm, w), lambda i: (i, 0))
    out_specs, out_shape = row(d), jax.ShapeDtypeStruct((n, d), BF16)
    if pack:
        slabs = d // 2 // LANES
        out_specs = [out_specs, pl.BlockSpec((tm, slabs, LANES), lambda i: (i, 0, 0))]
        out_shape = [out_shape, jax.ShapeDtypeStruct((n, slabs, LANES), I32)]
    return pl.pallas_call(
        _norm_mod_pack_kernel if pack else _norm_mod_kernel,
        grid=(n // tm,),
        in_specs=[row(d),
                  pl.BlockSpec((1, d), lambda i: (0, 0)),
                  pl.BlockSpec((None, 1, d), lambda i: (i // per_batch, 0, 0)),
                  pl.BlockSpec((None, 1, d), lambda i: (i // per_batch, 0, 0))],
        out_specs=out_specs,
        out_shape=out_shape,
        compiler_params=_params("parallel"), name="norm_modulate",
    )(x2, g.reshape(1, d), scale, shift)


def _final_norm_kernel(x_ref, g_ref, o_ref):
    x = x_ref[...]
    y = x * lax.rsqrt(jnp.mean(x * x, axis=-1, keepdims=True) + EPS)
    o_ref[...] = y * g_ref[...]


def final_norm(x2, g):
    n, d = x2.shape
    tm = min(256, n)
    return pl.pallas_call(
        _final_norm_kernel,
        grid=(n // tm,),
        in_specs=[pl.BlockSpec((tm, d), lambda i: (i, 0)), pl.BlockSpec((1, d), lambda i: (0, 0))],
        out_specs=pl.BlockSpec((tm, d), lambda i: (i, 0)),
        out_shape=jax.ShapeDtypeStruct((n, d), F32),
        compiler_params=_params("parallel"), name="final_norm",
    )(x2, g.reshape(1, d))


def _mm_kernel(a_ref, w_ref, o_ref):
    o_ref[...] = jnp.dot(a_ref[...], w_ref[...], preferred_element_type=F32).astype(o_ref.dtype)


def matmul(a, w, out_dtype, tm=1024, tn=512):
    m, k = a.shape
    n = w.shape[1]
    tm, tn = min(tm, m), min(tn, n)
    return pl.pallas_call(
        _mm_kernel,
        grid=(m // tm, n // tn),
        in_specs=[pl.BlockSpec((tm, k), lambda i, j: (i, 0)),
                  pl.BlockSpec((k, tn), lambda i, j: (0, j))],
        out_specs=pl.BlockSpec((tm, tn), lambda i, j: (i, j)),
        out_shape=jax.ShapeDtypeStruct((m, n), out_dtype),
        compiler_params=_params("parallel", "parallel"), name="projection",
    )(a, w)


def _mm_res_kernel(a_ref, w_ref, r_ref, g_ref, o_ref):
    acc = jnp.dot(a_ref[...], w_ref[...], preferred_element_type=F32)
    o_ref[...] = r_ref[...] + g_ref[...] * acc


def matmul_residual(a, w, res, gate, seq, tm=1024, tn=512):
    m, k = a.shape
    n = w.shape[1]
    tm, tn = min(tm, seq), min(tn, n)
    per_batch = seq // tm
    return pl.pallas_call(
        _mm_res_kernel,
        grid=(m // tm, n // tn),
        in_specs=[pl.BlockSpec((tm, k), lambda i, j: (i, 0)),
                  pl.BlockSpec((k, tn), lambda i, j: (0, j)),
                  pl.BlockSpec((tm, tn), lambda i, j: (i, j)),
                  pl.BlockSpec((None, 1, tn), lambda i, j: (i // per_batch, 0, j))],
        out_specs=pl.BlockSpec((tm, tn), lambda i, j: (i, j)),
        out_shape=jax.ShapeDtypeStruct((m, n), F32),
        compiler_params=_params("parallel", "parallel"), name="projection_residual",
    )(a, w, res, gate)


def _gelu(x):
    return 0.5 * x * (1.0 + lax.erf(x * (2.0 ** -0.5)))


def _gmlp_kernel(u_ref, v_ref, lng_ref, lnb_ref, ws_ref, bs_ref, o_ref, *, groups):
    u = _gelu(u_ref[...].astype(F32))
    v = _gelu(v_ref[...].astype(F32))
    mu = jnp.mean(v, axis=-1, keepdims=True)
    var = jnp.mean(jnp.square(v - mu), axis=-1, keepdims=True)
    vn = ((v - mu) * lax.rsqrt(var + EPS) * lng_ref[...] + lnb_ref[...]).astype(BF16)
    t = ws_ref.shape[1]
    gw = u.shape[1] // groups
    row = lax.broadcasted_iota(I32, (t, t), 0)
    col = lax.broadcasted_iota(I32, (t, t), 1)
    causal = col <= row
    for g in range(groups):
        w = jnp.where(causal, ws_ref[g], 0.0).astype(BF16)
        mixed = jnp.dot(w, vn[:, g * gw:(g + 1) * gw], preferred_element_type=F32) + bs_ref[:, g:g + 1]
        o_ref[:, g * gw:(g + 1) * gw] = (u[:, g * gw:(g + 1) * gw] * mixed).astype(o_ref.dtype)
    o_ref[:, u.shape[1]:] = jnp.zeros((t, o_ref.shape[1] - u.shape[1]), o_ref.dtype)


def gmlp_mixer(proj, a_width, ln_g, ln_b, w_s, b_s, out_width):
    n = proj.shape[0]
    groups, t, _ = w_s.shape
    return pl.pallas_call(
        functools.partial(_gmlp_kernel, groups=groups),
        grid=(n // t,),
        in_specs=[pl.BlockSpec((t, a_width), lambda i: (i, 0)),
                  pl.BlockSpec((t, a_width), lambda i: (i, 1)),
                  pl.BlockSpec((1, a_width), lambda i: (0, 0)),
                  pl.BlockSpec((1, a_width), lambda i: (0, 0)),
                  pl.BlockSpec((groups, t, t), lambda i: (0, 0, 0)),
                  pl.BlockSpec((t, groups), lambda i: (0, 0))],
        out_specs=pl.BlockSpec((t, out_width), lambda i: (i, 0)),
        out_shape=jax.ShapeDtypeStruct((n, out_width), BF16),
        compiler_params=_params("parallel"), name="gmlp_mixer",
    )(proj, proj, ln_g.reshape(1, a_width), ln_b.reshape(1, a_width), w_s, b_s.T)


def _rope_kernel(q_ref, k_ref, x_ref, cb_ref, sb_ref, ci_ref, si_ref, ck_ref, sk_ref,
                 qo_ref, ko_ref, qio_ref, kia_ref, kib_ref, wo_ref, *, heads, idx_pairs, q_scale, w_scale):
    cb, sb = cb_ref[...], sb_ref[...]
    half = HEAD_DIM // 2
    for h in range(heads):
        x = q_ref[:, h * HEAD_DIM:(h + 1) * HEAD_DIM].astype(F32)
        r = x * cb + pltpu.roll(x, half, 1) * sb
        qo_ref[:, h * HEAD_DIM:(h + 1) * HEAD_DIM] = (r * q_scale).astype(qo_ref.dtype)
    x = k_ref[...].astype(F32)
    ko_ref[...] = (x * cb + pltpu.roll(x, half, 1) * sb).astype(ko_ref.dtype)

    lane = lax.broadcasted_iota(I32, cb.shape, 1)
    first_half = (lane % IDX_DIM) < (IDX_DIM // 2)

    def rope_idx(x, c, s):
        rot = jnp.where(first_half, pltpu.roll(x, LANES - IDX_DIM // 2, 1), pltpu.roll(x, IDX_DIM // 2, 1))
        return x * c + rot * s

    ci, si = ci_ref[...], si_ref[...]
    for p in range(idx_pairs):
        x = x_ref[:, p * LANES:(p + 1) * LANES]
        qio_ref[:, p * LANES:(p + 1) * LANES] = rope_idx(x, ci, si).astype(qio_ref.dtype)
    tail = x_ref[:, idx_pairs * LANES:(idx_pairs + 1) * LANES]
    roped = rope_idx(tail, ck_ref[...], sk_ref[...])
    is_k = lane < IDX_DIM
    ka = jnp.where(is_k, roped, 0.0)
    kia_ref[...] = ka.astype(kia_ref.dtype)
    kib_ref[...] = pltpu.roll(ka, IDX_DIM, 1).astype(kib_ref.dtype)
    w = pltpu.roll(tail, LANES - IDX_DIM, 1)
    wo_ref[...] = jnp.where(lane < IDX_HEADS, w * w_scale, 0.0)


def rope_prepare(proj, proj_idx, tables, q_off, heads):
    n = proj.shape[0]
    tm = min(256, n)
    idx_pairs = IDX_HEADS * IDX_DIM // LANES
    qb = q_off // (heads * HEAD_DIM)
    kb = (q_off + heads * HEAD_DIM) // HEAD_DIM
    row = lambda w: pl.BlockSpec((tm, w), lambda i: (i, 0))
    kernel = functools.partial(_rope_kernel, heads=heads, idx_pairs=idx_pairs, q_scale=HEAD_DIM ** -0.5,
                               w_scale=IDX_HEADS ** -0.5 * IDX_DIM ** -0.5)
    return pl.pallas_call(
        kernel,
        grid=(n // tm,),
        in_specs=[pl.BlockSpec((tm, heads * HEAD_DIM), lambda i: (i, qb)),
                  pl.BlockSpec((tm, HEAD_DIM), lambda i: (i, kb)),
                  row(proj_idx.shape[1])] + [row(LANES)] * 6,
        out_specs=[row(heads * HEAD_DIM), row(HEAD_DIM), row(idx_pairs * LANES), row(LANES), row(LANES), row(LANES)],
        out_shape=[jax.ShapeDtypeStruct((n, heads * HEAD_DIM), BF16),
                   jax.ShapeDtypeStruct((n, HEAD_DIM), BF16),
                   jax.ShapeDtypeStruct((n, idx_pairs * LANES), BF16),
                   jax.ShapeDtypeStruct((n, LANES), BF16),
                   jax.ShapeDtypeStruct((n, LANES), BF16),
                   jax.ShapeDtypeStruct((n, LANES), F32)],
        compiler_params=_params("parallel"), name="rope_prepare",
    )(proj, proj, proj_idx, *tables)


def rope_tables(positions):
    pos = positions.reshape(-1).astype(F32)[:, None]

    def cs(dim):
        inv = ROPE_THETA ** (-jnp.arange(0, dim, 2, dtype=F32) / dim)
        ang = pos * inv
        return jnp.cos(ang), jnp.sin(ang)

    cb, sb = cs(HEAD_DIM)
    ci, si = cs(IDX_DIM)
    ones, zeros = jnp.ones_like(cb), jnp.zeros_like(cb)
    return (jnp.concatenate([cb, cb], 1), jnp.concatenate([-sb, sb], 1),
            jnp.concatenate([ci, ci, ci, ci], 1), jnp.concatenate([-si, si, -si, si], 1),
            jnp.concatenate([ci, ci, ones], 1), jnp.concatenate([-si, si, zeros], 1))


DSA_ATTN_TILE = 512


def _sort_key(x):
    x = jnp.where(x == 0.0, 0.0, x)
    bits = pltpu.bitcast(x, I32)
    return jnp.where(bits < 0, bits ^ 0x7FFFFFFF, bits)


def _dsa_kernel(qi_ref, w_ref, kia_ref, kib_ref, q_ref, k_ref, v_ref, buf_ref, o_ref, key_ref,
                *, tq, tk, heads, n_sel):
    del buf_ref
    qt = pl.program_id(1)
    q_lo = qt * tq
    n_kb = (q_lo + tq + tk - 1) // tk
    sub = tk // LANES
    row_pos = q_lo + lax.broadcasted_iota(I32, (tq, LANES), 0)
    lane = lax.broadcasted_iota(I32, (tq, LANES), 1)

    w = w_ref[...]
    w_cols = [jnp.broadcast_to(w[:, h:h + 1], (tq, tk)) for h in range(IDX_HEADS)]

    def score_block(kb, carry):
        k0 = pl.multiple_of(kb * tk, tk)
        ka = kia_ref[pl.ds(k0, tk), :]
        kb_ = kib_ref[pl.ds(k0, tk), :]
        acc = jnp.zeros((tq, tk), F32)
        for p in range(IDX_HEADS // 2):
            x = qi_ref[:, p * LANES:(p + 1) * LANES]
            ra = lax.dot_general(x, ka, (((1,), (1,)), ((), ())), preferred_element_type=F32)
            rb = lax.dot_general(x, kb_, (((1,), (1,)), ((), ())), preferred_element_type=F32)
            acc = acc + w_cols[2 * p] * jnp.maximum(ra, 0.0)
            acc = acc + w_cols[2 * p + 1] * jnp.maximum(rb, 0.0)
        key = _sort_key(acc)
        for j in range(sub):
            col_pos = k0 + j * LANES + lane
            key_ref[kb * sub + j] = jnp.where(col_pos <= row_pos, key[:, j * LANES:(j + 1) * LANES], INT_MIN)
        return carry

    lax.fori_loop(0, n_kb, score_block, 0)
    n_slabs = n_kb * sub

    def count(pred):
        def body(kb, acc):
            hits = [jnp.where(pred(key_ref[kb * sub + j], kb * sub + j), 1, 0) for j in range(sub)]
            while len(hits) > 1:
                hits = [a + b for a, b in zip(hits[::2], hits[1::2])]
            return acc + hits[0]
        part = lax.fori_loop(0, n_kb, body, jnp.zeros((tq, LANES), I32))
        return jnp.sum(part.astype(F32), axis=1, keepdims=True).astype(I32)

    def search_bit(i, u):
        cand = u | (1 << (31 - i))
        thr = jnp.broadcast_to(cand ^ INT_MIN, (tq, LANES))
        total = count(lambda kv, c: kv >= thr)
        return jnp.where(total >= n_sel, cand, u)

    u = lax.fori_loop(0, 32, search_bit, jnp.zeros((tq, 1), I32))
    thr = u ^ INT_MIN
    thr_b = jnp.broadcast_to(thr, (tq, LANES))
    n_gt = count(lambda kv, c: kv > thr_b)
    n_ge = count(lambda kv, c: kv >= thr_b)
    need = n_sel - n_gt
    tie_rows = jnp.logical_and(n_ge > n_sel, need > 0)

    seq_bits = max(1, (key_ref.shape[0] * LANES).bit_length())

    def tie_search():
        need_b = need

        def bit_step(i, j):
            cand = j | (1 << (seq_bits - 1 - i))
            cand_b = jnp.broadcast_to(cand, (tq, LANES))
            total = count(lambda kv, c: jnp.logical_and(kv == thr_b, c * LANES + lane < cand_b))
            return jnp.where(total <= need_b, cand, j)

        return lax.fori_loop(0, seq_bits, bit_step, jnp.zeros((tq, 1), I32))

    any_tie = jnp.max(jnp.where(tie_rows, 1.0, 0.0)) > 0.0
    bound = lax.cond(any_tie, tie_search, lambda: jnp.full((tq, 1), 2 ** 30, I32))
    bound = jnp.where(tie_rows, bound, 2 ** 30)
    bound_b = jnp.broadcast_to(bound, (tq, LANES))

    q_all = jnp.concatenate([q_ref[:, h * HEAD_DIM:(h + 1) * HEAD_DIM] for h in range(heads)], axis=0)

    ta = min(tk, DSA_ATTN_TILE)
    sub_a = ta // LANES

    def attn_block(kb, carry):
        m, l, acc = carry
        k0 = pl.multiple_of(kb * ta, ta)
        kk = k_ref[pl.ds(k0, ta), :]
        vv = v_ref[pl.ds(k0, ta), :]
        s = lax.dot_general(q_all, kk, (((1,), (1,)), ((), ())), preferred_element_type=F32)
        bias_cols = []
        for j in range(sub_a):
            kv = key_ref[kb * sub_a + j]
            col_pos = k0 + j * LANES + lane
            take = jnp.logical_or(kv > thr_b, jnp.logical_and(kv == thr_b, col_pos < bound_b))
            take = jnp.logical_and(take, col_pos <= row_pos)
            bias_cols.append(jnp.where(take, 0.0, MASKED))
        bias = jnp.concatenate(bias_cols, axis=1)
        s = s.reshape(heads, tq, ta) + bias[None]
        m_new = jnp.maximum(m, jnp.max(s, axis=-1, keepdims=True))
        alpha = jnp.exp(m - m_new)
        p = jnp.exp(s - m_new)
        l = alpha * l + jnp.sum(p, axis=-1, keepdims=True)
        pv = jnp.dot(p.reshape(heads * tq, ta).astype(BF16), vv, preferred_element_type=F32)
        acc = alpha * acc + pv.reshape(heads, tq, HEAD_DIM)
        return m_new, l, acc

    init = (jnp.full((heads, tq, 1), MASKED, F32), jnp.zeros((heads, tq, 1), F32),
            jnp.zeros((heads, tq, HEAD_DIM), F32))
    m, l, acc = lax.fori_loop(0, n_kb * (tk // ta), attn_block, init)
    out = acc / l
    for h in range(heads):
        o_ref[:, h * HEAD_DIM:(h + 1) * HEAD_DIM] = out[h].astype(o_ref.dtype)


def dsa_mixer(qi, w_idx, kia, kib, q, k, proj, v_col_block, buf, seq, n_sel):
    n = q.shape[0]
    bsz = n // seq
    heads = q.shape[1] // HEAD_DIM
    tq = min(128, seq)
    tk = min(512, seq)
    nq = seq // tq
    kernel = functools.partial(_dsa_kernel, tq=tq, tk=tk, heads=heads, n_sel=n_sel)
    qrow = lambda w: pl.BlockSpec((tq, w), lambda b, i: (b * nq + i, 0))
    kv = lambda c: pl.BlockSpec((seq, LANES), lambda b, i: (b, c))
    return pl.pallas_call(
        kernel,
        grid=(bsz, nq),
        in_specs=[qrow(qi.shape[1]), qrow(LANES), kv(0), kv(0), qrow(q.shape[1]), kv(0), kv(v_col_block),
                  pl.BlockSpec(memory_space=pl.ANY)],
        out_specs=pl.BlockSpec((tq, heads * HEAD_DIM), lambda b, i: (b * nq + i, 1)),
        out_shape=jax.ShapeDtypeStruct(buf.shape, buf.dtype),
        scratch_shapes=[pltpu.VMEM((seq // LANES, tq, LANES), I32)],
        input_output_aliases={7: 0},
        compiler_params=_params("parallel", "arbitrary"), name="dsa_mixer",
    )(qi, w_idx, kia, kib, q, k, proj, buf)


SB_HEADS_PER_STEP = 4
SB_QUERY_TILE = 512
SB_KEY_TILE = 256
SB_LOGIT_SCALE = HEAD_DIM ** -0.5 * LOG2E


def _sb_kernel(q_ref, k_ref, v_ref, o_ref, *, tq, tk, group):
    qt = pl.program_id(2)
    ratio = tq // tk
    row = lax.broadcasted_iota(I32, (tk, tk), 0)
    col = lax.broadcasted_iota(I32, (tk, tk), 1)
    later = jnp.where(row > col, 1.0, 0.0).astype(BF16)
    q_pos = qt * tq + lax.broadcasted_iota(I32, (tq, tk), 0)
    k_off = lax.broadcasted_iota(I32, (tq, tk), 1)

    def block(kb, masked, carry):
        k0 = pl.multiple_of(kb * tk, tk)
        heads = range(group)
        cols = [slice(h * HEAD_DIM, (h + 1) * HEAD_DIM) for h in heads]
        z = [lax.dot_general(q_ref[:, cols[h]], k_ref[pl.ds(k0, tk), cols[h]], (((1,), (1,)), ((), ())),
                             preferred_element_type=F32).astype(BF16) for h in heads]
        log_beta = [jnp.minimum(z[h], 0.0) - jnp.log(1.0 + jnp.exp2(-jnp.abs(z[h]))) * LOG2E for h in heads]
        log_fail = [log_beta[h] - z[h] for h in heads]
        if masked:
            strict = k0 + k_off < q_pos
            log_fail = [jnp.where(strict, log_fail[h], 0.0) for h in heads]
        between = [jnp.dot(log_fail[h].astype(BF16), later, preferred_element_type=F32) + carry[h][1]
                   for h in heads]
        a = [jnp.exp2(log_beta[h] + between[h]) for h in heads]
        if masked:
            a = [jnp.where(strict, a[h], 0.0) for h in heads]
        acc = [carry[h][0] + jnp.dot(a[h].astype(BF16), v_ref[pl.ds(k0, tk), cols[h]], preferred_element_type=F32)
               for h in heads]
        run = [between[h][:, 0:1] + log_fail[h][:, 0:1] for h in heads]
        return tuple((acc[h], run[h]) for h in heads)

    carry = tuple((jnp.zeros((tq, HEAD_DIM), F32), jnp.zeros((tq, 1), F32)) for _ in range(group))
    first_full = qt * ratio
    for j in reversed(range(ratio)):
        carry = block(first_full + j, True, carry)
    carry = lax.fori_loop(0, first_full, lambda i, c: block(first_full - 1 - i, False, c), carry)
    for h in range(group):
        o_ref[:, h * HEAD_DIM:(h + 1) * HEAD_DIM] = carry[h][0].astype(o_ref.dtype)


def stick_breaking_mixer(qkv, seq, heads):
    n = qkv.shape[0]
    bsz = n // seq
    tq = min(SB_QUERY_TILE, seq)
    tk = min(SB_KEY_TILE, seq)
    nq = seq // tq
    group = SB_HEADS_PER_STEP
    hg = heads // group
    width = group * HEAD_DIM
    kernel = functools.partial(_sb_kernel, tq=tq, tk=tk, group=group)
    return pl.pallas_call(
        kernel,
        grid=(bsz, hg, nq),
        in_specs=[pl.BlockSpec((tq, width), lambda b, h, i: (b * nq + i, h)),
                  pl.BlockSpec((seq, width), lambda b, h, i: (b, hg + h)),
                  pl.BlockSpec((seq, width), lambda b, h, i: (b, 2 * hg + h))],
        out_specs=pl.BlockSpec((tq, width), lambda b, h, i: (b * nq + i, h)),
        out_shape=jax.ShapeDtypeStruct((n, heads * HEAD_DIM), BF16),
        compiler_params=_params("parallel", "parallel", "arbitrary"), name="stick_breaking",
    )(qkv, qkv, qkv)


def _router_kernel(h_ref, wr_ref, b_ref, s_ref, w_ref, c_ref, cnt_ref, *, n_experts):
    per_group = n_experts // N_GROUPS
    tm = h_ref.shape[0]
    logits = lax.dot_general(wr_ref[...], h_ref[...], (((1,), (1,)), ((), ())), preferred_element_type=F32)
    scores = jax.nn.sigmoid(logits)
    choice = (scores + b_ref[...]).reshape(N_GROUPS, per_group, tm)
    s3 = scores.reshape(N_GROUPS, per_group, tm)
    neg = -jnp.inf
    in_group = lax.broadcasted_iota(I32, choice.shape, 1)
    m1 = jnp.max(choice, axis=1, keepdims=True)
    first = jnp.min(jnp.where(choice == m1, in_group, per_group), axis=1, keepdims=True)
    m2 = jnp.max(jnp.where(in_group == first, neg, choice), axis=1, keepdims=True)
    group_score = m1 + m2
    gid = lax.broadcasted_iota(I32, group_score.shape, 0)
    group_sel = jnp.zeros(group_score.shape, jnp.bool_)
    for _ in range(TOPK_GROUPS):
        m = jnp.max(group_score, axis=0, keepdims=True)
        f = jnp.min(jnp.where(group_score == m, gid, N_GROUPS), axis=0, keepdims=True)
        hit = gid == f
        group_sel = jnp.logical_or(group_sel, hit)
        group_score = jnp.where(hit, neg, group_score)
    cand = jnp.where(group_sel, choice, neg)
    eid = lax.broadcasted_iota(I32, choice.shape, 0) * per_group + in_group
    sel = jnp.zeros(choice.shape, jnp.bool_)
    hits = []
    for _ in range(TOP_K):
        m = jnp.max(jnp.max(cand, axis=1, keepdims=True), axis=0, keepdims=True)
        f = jnp.min(jnp.min(jnp.where(cand == m, eid, n_experts), axis=1, keepdims=True), axis=0, keepdims=True)
        hit = eid == f
        hits.append((hit, f))
        sel = jnp.logical_or(sel, hit)
        cand = jnp.where(hit, neg, cand)
    top_w = jnp.where(sel, s3, 0.0)
    total = jnp.sum(jnp.sum(top_w, axis=1, keepdims=True), axis=0, keepdims=True)
    gates = top_w / total * ROUTED_SCALE

    @pl.when(pl.program_id(0) == 0)
    def _():
        cnt_ref[...] = jnp.zeros_like(cnt_ref)

    sel_f = jnp.where(sel, 1.0, 0.0).reshape(n_experts, tm)
    row = lax.broadcasted_iota(I32, (tm, tm), 0)
    col = lax.broadcasted_iota(I32, (tm, tm), 1)
    before = jnp.where(row < col, 1.0, 0.0).astype(BF16)
    prefix = jnp.dot(sel_f.astype(BF16), before, preferred_element_type=F32)
    rank_all = (prefix + cnt_ref[:, 0:1]).reshape(choice.shape)

    def pick(hit, val):
        return jnp.sum(jnp.sum(jnp.where(hit, val, 0.0), axis=1, keepdims=True), axis=0, keepdims=True)

    for k, (hit, f) in enumerate(hits):
        rank = pick(hit, rank_all).reshape(1, tm).astype(I32)
        s_ref[k:k + 1, :] = rank | lax.shift_left(f.reshape(1, tm), SLOT_RANK_BITS)
        w_ref[k:k + 1, :] = pick(hit, gates).reshape(1, tm)
    cnt_ref[...] = cnt_ref[...] + jnp.sum(sel_f, axis=1, keepdims=True)
    c_ref[...] = cnt_ref[...].astype(I32)


def moe_router(h, w_router, bias):
    n, d = h.shape
    e = w_router.shape[1]
    tm = min(512, n)
    top = lambda dt: jax.ShapeDtypeStruct((TOP_K, n), dt)
    top_spec = pl.BlockSpec((TOP_K, tm), lambda i: (0, i))
    slots, wts, counts = pl.pallas_call(
        functools.partial(_router_kernel, n_experts=e),
        grid=(n // tm,),
        in_specs=[pl.BlockSpec((tm, d), lambda i: (i, 0)),
                  pl.BlockSpec((e, d), lambda i: (0, 0)),
                  pl.BlockSpec((e, 1), lambda i: (0, 0))],
        out_specs=[top_spec, top_spec, pl.BlockSpec((e, LANES), lambda i: (0, 0))],
        out_shape=[top(I32), top(F32), jax.ShapeDtypeStruct((e, LANES), I32)],
        scratch_shapes=[pltpu.VMEM((e, LANES), F32)],
        compiler_params=_params("arbitrary"), name="moe_router",
    )(h, w_router.T.astype(BF16), bias.reshape(e, 1))
    return slots.reshape(-1), wts, counts[:, 0]


MOE_TILE = 512
DISPATCH_TOKENS = 256
COMBINE_TOKENS = 128
DMA_ISSUE_UNROLL = 8
PAD_CHUNK = 1024


def _pack_pairs(x):
    w = x.shape[1] // 2
    lo = pltpu.bitcast(x[:, :w].astype(BF16).astype(F32), I32)
    hi = pltpu.bitcast(x[:, w:].astype(BF16).astype(F32), I32)
    return lax.shift_right_logical(lo, 16) | hi


def _unpack_pairs(p):
    lo = pltpu.bitcast(lax.shift_left(p, 16), F32)
    hi = pltpu.bitcast(p & jnp.int32(-65536), F32)
    return lo, hi


def _store_slabs(ref, x):
    r, n_slabs, _ = ref.shape
    flat = ref.reshape(r * n_slabs, LANES)
    for s in range(n_slabs):
        flat[pl.ds(s, r, stride=n_slabs), :] = x[:, s * LANES:(s + 1) * LANES]


def _load_slab_column(ref, s):
    r, n_slabs, _ = ref.shape
    return ref.reshape(r * n_slabs, LANES)[pl.ds(s, r, stride=n_slabs), :]


def _load_slabs(ref):
    return jnp.concatenate([_load_slab_column(ref, s) for s in range(ref.shape[1])], axis=1)


SLOT_RANK_BITS = 20


def _slot_row(code, off_ref):
    return off_ref[lax.shift_right_logical(code, SLOT_RANK_BITS)] + (code & ((1 << SLOT_RANK_BITS) - 1))


def _dispatch_kernel(slot_ref, off_ref, cnt_ref, hp_ref, xs_ref, zero_ref, sem, pad_sem, *, td, n_tok, n_rows):
    base = pl.program_id(0) * td

    @pl.when(pl.program_id(0) == 0)
    def _():
        zero_ref[...] = jnp.zeros(zero_ref.shape, zero_ref.dtype)
        n_experts = cnt_ref.shape[0]

        def zero_row(r, carry):
            pltpu.make_async_copy(zero_ref, xs_ref.at[r], pad_sem).start()
            return carry

        def drain(r, carry):
            pltpu.make_async_copy(zero_ref, xs_ref.at[0], pad_sem).wait()
            return carry

        def pad_chunk(c, bounds):
            start, end = bounds
            lo = start + c * PAD_CHUNK
            hi = jnp.minimum(lo + PAD_CHUNK, end)
            lax.fori_loop(lo, hi, zero_row, 0)
            lax.fori_loop(lo, hi, drain, 0)
            return bounds

        def pad_expert(e, carry):
            start = off_ref[e] + cnt_ref[e]
            end = jnp.where(e + 1 < n_experts, off_ref[jnp.minimum(e + 1, n_experts - 1)], n_rows)
            lax.fori_loop(0, (end - start + PAD_CHUNK - 1) // PAD_CHUNK, pad_chunk, (start, end))
            return carry

        lax.fori_loop(0, n_experts, pad_expert, 0)

    for k in range(TOP_K):
        def issue(t, carry, k=k):
            row = _slot_row(slot_ref[k * n_tok + base + t], off_ref)
            pltpu.make_async_copy(hp_ref.at[t], xs_ref.at[row], sem).start()
            return carry

        lax.fori_loop(0, td, issue, 0, unroll=DMA_ISSUE_UNROLL)
    for k in range(TOP_K):
        pltpu.make_async_copy(hp_ref, xs_ref.at[pl.ds(0, td)], sem).wait()


def moe_dispatch(hp, slots, offsets, counts, n_rows):
    n, s, _ = hp.shape
    td = min(DISPATCH_TOKENS, n)
    kernel = functools.partial(_dispatch_kernel, td=td, n_tok=n, n_rows=n_rows)
    return pl.pallas_call(
        kernel,
        grid_spec=pltpu.PrefetchScalarGridSpec(
            num_scalar_prefetch=3,
            grid=(n // td,),
            in_specs=[pl.BlockSpec((td, s, LANES), lambda i, *_: (i, 0, 0))],
            out_specs=pl.BlockSpec(memory_space=pl.ANY),
            scratch_shapes=[pltpu.VMEM((s, LANES), I32), pltpu.SemaphoreType.DMA(()),
                            pltpu.SemaphoreType.DMA(())]),
        out_shape=jax.ShapeDtypeStruct((n_rows, s, LANES), I32),
        compiler_params=_params("arbitrary"), name="moe_dispatch",
    )(slots, offsets, counts, hp)


def _swiglu(lo, hi, wg_ref, wu_ref, wd_ref):
    w = lo.shape[1]
    g = (jnp.dot(lo, wg_ref[:w, :], preferred_element_type=F32)
         + jnp.dot(hi, wg_ref[w:, :], preferred_element_type=F32))
    u = (jnp.dot(lo, wu_ref[:w, :], preferred_element_type=F32)
         + jnp.dot(hi, wu_ref[w:, :], preferred_element_type=F32))
    act = g * jax.nn.sigmoid(g) * u
    return jnp.dot(act.astype(BF16), wd_ref[...], preferred_element_type=F32)


def _grouped_kernel(te_ref, nu_ref, x_ref, wg_ref, wu_ref, wd_ref, y_ref):
    del te_ref

    @pl.when(pl.program_id(0) < nu_ref[0])
    def _():
        lo, hi = _unpack_pairs(_load_slabs(x_ref))
        _store_slabs(y_ref, _pack_pairs(_swiglu(lo.astype(BF16), hi.astype(BF16), wg_ref, wu_ref, wd_ref)))


def moe_grouped(xs, tile_expert, n_used, wg, wu, wd):
    p, s, _ = xs.shape
    e, d, f = wg.shape
    tm = MOE_TILE
    row = lambda i, te, nu: (jnp.minimum(i, nu[0] - 1), 0, 0)
    return pl.pallas_call(
        _grouped_kernel,
        grid_spec=pltpu.PrefetchScalarGridSpec(
            num_scalar_prefetch=2,
            grid=(p // tm,),
            in_specs=[pl.BlockSpec((tm, s, LANES), row),
                      pl.BlockSpec((None, d, f), lambda i, te, nu: (te[i], 0, 0)),
                      pl.BlockSpec((None, d, f), lambda i, te, nu: (te[i], 0, 0)),
                      pl.BlockSpec((None, f, d), lambda i, te, nu: (te[i], 0, 0))],
            out_specs=pl.BlockSpec((tm, s, LANES), row)),
        out_shape=jax.ShapeDtypeStruct((p, s, LANES), I32),
        compiler_params=_params("arbitrary"), name="moe_grouped",
    )(tile_expert, n_used, xs, wg, wu, wd)


def _shared_kernel(h_ref, wg_ref, wu_ref, wd_ref, o_ref):
    x = h_ref[...]
    w = x.shape[1] // 2
    o_ref[...] = _swiglu(x[:, :w], x[:, w:], wg_ref, wu_ref, wd_ref).astype(o_ref.dtype)


def shared_expert(h, wg, wu, wd):
    n, d = h.shape
    f = wg.shape[1]
    tm = min(512, n)
    return pl.pallas_call(
        _shared_kernel,
        grid=(n // tm,),
        in_specs=[pl.BlockSpec((tm, d), lambda i: (i, 0)),
                  pl.BlockSpec((d, f), lambda i: (0, 0)),
                  pl.BlockSpec((d, f), lambda i: (0, 0)),
                  pl.BlockSpec((f, d), lambda i: (0, 0))],
        out_specs=pl.BlockSpec((tm, d), lambda i: (i, 0)),
        out_shape=jax.ShapeDtypeStruct((n, d), BF16),
        compiler_params=_params("parallel"), name="shared_expert",
    )(h, wg, wu, wd)


def _combine_kernel(slot_ref, off_ref, ys_ref, w_ref, x_ref, sh_ref, g_ref, o_ref, rows_ref, sem, *, tc, n_tok):
    base = pl.program_id(0) * tc

    for k in range(TOP_K):
        def issue(t, carry, k=k):
            row = _slot_row(slot_ref[k * n_tok + base + t], off_ref)
            pltpu.make_async_copy(ys_ref.at[row], rows_ref.at[k, t], sem).start()
            return carry

        lax.fori_loop(0, tc, issue, 0, unroll=DMA_ISSUE_UNROLL)
    for k in range(TOP_K):
        pltpu.make_async_copy(ys_ref.at[pl.ds(0, tc)], rows_ref.at[k], sem).wait()

    half = x_ref.shape[1] // 2
    w_cols = [w_ref[:, k:k + 1] for k in range(TOP_K)]
    for s in range(rows_ref.shape[2]):
        acc_lo = jnp.zeros((tc, LANES), F32)
        acc_hi = jnp.zeros((tc, LANES), F32)
        for k in range(TOP_K):
            lo, hi = _unpack_pairs(_load_slab_column(rows_ref.at[k], s))
            acc_lo = acc_lo + w_cols[k] * lo
            acc_hi = acc_hi + w_cols[k] * hi
        for acc, c0 in ((acc_lo, s * LANES), (acc_hi, half + s * LANES)):
            cols = slice(c0, c0 + LANES)
            o_ref[:, cols] = x_ref[:, cols] + g_ref[:, cols] * (acc + sh_ref[:, cols].astype(F32))


def moe_combine(ys, slots, offsets, wts, x2, shared, gate, seq):
    n, d = x2.shape
    s = ys.shape[1]
    tc = min(COMBINE_TOKENS, seq)
    per_batch = seq // tc
    kernel = functools.partial(_combine_kernel, tc=tc, n_tok=n)
    return pl.pallas_call(
        kernel,
        grid_spec=pltpu.PrefetchScalarGridSpec(
            num_scalar_prefetch=2,
            grid=(n // tc,),
            in_specs=[pl.BlockSpec(memory_space=pl.ANY),
                      pl.BlockSpec((tc, TOP_K), lambda i, *_: (i, 0)),
                      pl.BlockSpec((tc, d), lambda i, *_: (i, 0)),
                      pl.BlockSpec((tc, d), lambda i, *_: (i, 0)),
                      pl.BlockSpec((None, 1, d), lambda i, *_: (i // per_batch, 0, 0))],
            out_specs=pl.BlockSpec((tc, d), lambda i, *_: (i, 0)),
            scratch_shapes=[pltpu.VMEM((TOP_K, tc, s, LANES), I32), pltpu.SemaphoreType.DMA(())]),
        out_shape=jax.ShapeDtypeStruct((n, d), F32),
        compiler_params=_params("arbitrary"), name="moe_combine",
    )(slots, offsets, ys, wts, x2, shared, gate)


def moe_layout(counts, n_tiles):
    e = counts.shape[0]
    tiles = (counts + MOE_TILE - 1) // MOE_TILE
    earlier = jnp.arange(e)[None, :] <= jnp.arange(e)[:, None]
    tile_end = jnp.sum(jnp.where(earlier, tiles[None, :], 0), axis=1)
    offsets = (tile_end - tiles) * MOE_TILE
    n_used = tile_end[-1]
    tile_ids = jnp.minimum(jnp.arange(n_tiles, dtype=I32), n_used - 1)
    tile_expert = jnp.sum(tile_end[None, :] <= tile_ids[:, None], axis=1)
    return offsets.astype(I32), tile_expert.astype(I32), n_used.reshape(1).astype(I32)


def _pad_cols(w, width):
    return jnp.pad(w, ((0, 0), (0, width - w.shape[1])))


def kernel(x, c, positions, ab_w_in, ab_w_out, gmlp_ln_g, gmlp_ln_b, gmlp_w_s, gmlp_b_s, sb_w_qkv, sb_w_out,
           norm_mix_g, ada_mix_w, ada_mix_b, norm_ffn_g, ada_ffn_w, ada_ffn_b, router_w, router_bias,
           expert_w_gate, expert_w_up, expert_w_down, shared_w_gate, shared_w_up, shared_w_down, final_norm_g):
    bsz, seq, d = x.shape
    depth = norm_mix_g.shape[0]
    a_width = gmlp_ln_g.shape[1]
    b_width = ab_w_out.shape[1] - a_width
    b_heads = b_width // HEAD_DIM
    c_heads = sb_w_out.shape[1] // HEAD_DIM
    n_sel = min(TOPK_MAX, seq // 4)
    main_width = 2 * a_width + b_width + 2 * HEAD_DIM
    idx_width = IDX_HEADS * IDX_DIM + LANES

    mod_mix = ada_modulation_all(c, ada_mix_w, ada_mix_b)
    mod_ffn = ada_modulation_all(c, ada_ffn_w, ada_ffn_b)
    tables = rope_tables(positions)
    x2 = x.reshape(bsz * seq, d)

    for layer in range(depth):
        j = layer // 2
        shift, scale, gate = mod_mix[layer, :, 0], mod_mix[layer, :, 1], mod_mix[layer, :, 2]
        h = norm_modulate(x2, norm_mix_g[layer], scale, shift, seq)
        if layer % 2 == 0:
            w_in = ab_w_in[j]
            proj = matmul(h, w_in[:, :main_width].astype(BF16), BF16, tn=640 if main_width % 640 == 0 else 128)
            proj_idx = matmul(h, _pad_cols(w_in[:, main_width:], idx_width).astype(BF16), F32, tn=idx_width)
            q, k, qi, kia, kib, w_idx = rope_prepare(proj, proj_idx, tables, 2 * a_width, b_heads)
            buf = gmlp_mixer(proj, a_width, gmlp_ln_g[j], gmlp_ln_b[j], gmlp_w_s[j], gmlp_b_s[j],
                             a_width + b_width)
            v_col_block = (2 * a_width + b_width + HEAD_DIM) // HEAD_DIM
            mixed = dsa_mixer(qi, w_idx, kia, kib, q, k, proj, v_col_block, buf, seq, n_sel)
            x2 = matmul_residual(mixed, ab_w_out[j].astype(BF16), x2, gate, seq)
        else:
            c_width = c_heads * HEAD_DIM
            col_scale = jnp.where(jnp.arange(3 * c_width) < c_width, SB_LOGIT_SCALE, 1.0).astype(F32)
            qkv = matmul(h, (sb_w_qkv[j] * col_scale).astype(BF16), BF16)
            o = stick_breaking_mixer(qkv, seq, c_heads)
            x2 = matmul_residual(o, sb_w_out[j].astype(BF16), x2, gate, seq)

        shift, scale, gate = mod_ffn[layer, :, 0], mod_ffn[layer, :, 1], mod_ffn[layer, :, 2]
        h, h_packed = norm_modulate(x2, norm_ffn_g[layer], scale, shift, seq, pack=True)
        slots, wts, counts = moe_router(h, router_w[layer], router_bias[layer])
        n_tiles = bsz * seq * TOP_K // MOE_TILE + router_w.shape[2]
        offsets, tile_expert, n_used = moe_layout(counts, n_tiles)
        xs = moe_dispatch(h_packed, slots, offsets, counts, n_tiles * MOE_TILE)
        ys = moe_grouped(xs, tile_expert, n_used, expert_w_gate[layer].astype(BF16),
                         expert_w_up[layer].astype(BF16), expert_w_down[layer].astype(BF16))
        shared = shared_expert(h, shared_w_gate[layer].astype(BF16), shared_w_up[layer].astype(BF16),
                               shared_w_down[layer].astype(BF16))
        x2 = moe_combine(ys, slots, offsets, wts.T, x2, shared, gate, seq)

    return final_norm(x2, final_norm_g).reshape(bsz, seq, d)
```

```python
import functools

import jax
import jax.numpy as jnp
from jax import lax
from jax.experimental import pallas as pl
from jax.experimental.pallas import tpu as pltpu

F32 = jnp.float32
BF16 = jnp.bfloat16
I32 = jnp.int32

EPS = 1e-6
ROPE_THETA = 10000.0
LANES = 128
HEAD_DIM = 128
A_GROUPS = 8
A_CHUNK = 128
IDX_HEADS = 16
IDX_DIM = 64
TOPK_MAX = 256
N_GROUPS = 8
TOPK_GROUPS = 4
TOP_K = 8
ROUTED_SCALE = 2.5
VMEM_LIMIT = 56 * 1024 * 1024
INT_MIN = -(2 ** 31)
MASKED = -1e30
LOG2E = 1.4426950408889634


def _params(*sem):
    return pltpu.CompilerParams(dimension_semantics=sem, vmem_limit_bytes=VMEM_LIMIT)


def _mod_kernel(c_ref, w_ref, b_ref, o_ref):
    c = c_ref[...]
    s = c * jax.nn.sigmoid(c)
    o_ref[...] = jnp.dot(s.astype(BF16), w_ref[...].astype(BF16), preferred_element_type=F32) + b_ref[...]


def ada_modulation_all(c, w, b):
    bsz, d = c.shape
    n_layers, _, n_out = w.shape
    rows = 8
    c_pad = jnp.zeros((rows, d), F32).at[:bsz].set(c)
    tn = min(512, n_out)
    out = pl.pallas_call(
        _mod_kernel,
        grid=(n_layers, n_out // tn),
        in_specs=[pl.BlockSpec((rows, d), lambda l, j: (0, 0)),
                  pl.BlockSpec((None, d, tn), lambda l, j: (l, 0, j)),
                  pl.BlockSpec((None, 1, tn), lambda l, j: (l, 0, j))],
        out_specs=pl.BlockSpec((None, rows, tn), lambda l, j: (l, 0, j)),
        out_shape=jax.ShapeDtypeStruct((n_layers, rows, n_out), F32),
        compiler_params=_params("parallel", "parallel"), name="ada_modulation",
    )(c_pad, w, b.reshape(n_layers, 1, n_out))
    return out[:, :bsz].reshape(n_layers, bsz, 3, 1, d)


def _norm_mod_kernel(x_ref, g_ref, sc_ref, sh_ref, o_ref):
    x = x_ref[...]
    y = x * lax.rsqrt(jnp.mean(x * x, axis=-1, keepdims=True) + EPS)
    o_ref[...] = ((y * g_ref[...]) * (1.0 + sc_ref[...]) + sh_ref[...]).astype(o_ref.dtype)


def _norm_mod_pack_kernel(x_ref, g_ref, sc_ref, sh_ref, o_ref, p_ref):
    x = x_ref[...]
    y = x * lax.rsqrt(jnp.mean(x * x, axis=-1, keepdims=True) + EPS)
    h = (y * g_ref[...]) * (1.0 + sc_ref[...]) + sh_ref[...]
    o_ref[...] = h.astype(o_ref.dtype)
    _store_slabs(p_ref, _pack_pairs(h))


def norm_modulate(x2, g, scale, shift, seq, pack=False):
    n, d = x2.shape
    tm = min(256, seq)
    per_batch = seq // tm
    row = lambda w: pl.BlockSpec((tm, w), lambda i: (i, 0))
    out_specs, out_shape = row(d), jax.ShapeDtypeStruct((n, d), BF16)
    if pack:
        slabs = d // 2 // LANES
        out_specs = [out_specs, pl.BlockSpec((tm, slabs, LANES), lambda i: (i, 0, 0))]
        out_shape = [out_shape, jax.ShapeDtypeStruct((n, slabs, LANES), I32)]
    return pl.pallas_call(
        _norm_mod_pack_kernel if pack else _norm_mod_kernel,
        grid=(n // tm,),
        in_specs=[row(d),
                  pl.BlockSpec((1, d), lambda i: (0, 0)),
                  pl.BlockSpec((None, 1, d), lambda i: (i // per_batch, 0, 0)),
                  pl.BlockSpec((None, 1, d), lambda i: (i // per_batch, 0, 0))],
        out_specs=out_specs,
        out_shape=out_shape,
        compiler_params=_params("parallel"), name="norm_modulate",
    )(x2, g.reshape(1, d), scale, shift)


def _final_norm_kernel(x_ref, g_ref, o_ref):
    x = x_ref[...]
    y = x * lax.rsqrt(jnp.mean(x * x, axis=-1, keepdims=True) + EPS)
    o_ref[...] = y * g_ref[...]


def final_norm(x2, g):
    n, d = x2.shape
    tm = min(256, n)
    return pl.pallas_call(
        _final_norm_kernel,
        grid=(n // tm,),
        in_specs=[pl.BlockSpec((tm, d), lambda i: (i, 0)), pl.BlockSpec((1, d), lambda i: (0, 0))],
        out_specs=pl.BlockSpec((tm, d), lambda i: (i, 0)),
        out_shape=jax.ShapeDtypeStruct((n, d), F32),
        compiler_params=_params("parallel"), name="final_norm",
    )(x2, g.reshape(1, d))


def _mm_kernel(a_ref, w_ref, o_ref):
    o_ref[...] = jnp.dot(a_ref[...], w_ref[...].astype(BF16), preferred_element_type=F32).astype(o_ref.dtype)


def _mm_colscale_kernel(a_ref, w_ref, s_ref, o_ref):
    w = (w_ref[...] * s_ref[...]).astype(BF16)
    o_ref[...] = jnp.dot(a_ref[...], w, preferred_element_type=F32).astype(o_ref.dtype)


def matmul(a, w, out_dtype, tm=1024, tn=512, n_out=None, col_scale=None):
    m, k = a.shape
    n = w.shape[1] if n_out is None else n_out
    tm, tn = min(tm, m), min(tn, n)
    in_specs = [pl.BlockSpec((tm, k), lambda i, j: (i, 0)),
                pl.BlockSpec((k, tn), lambda i, j: (0, j))]
    args = (a, w)
    if col_scale is not None:
        in_specs.append(pl.BlockSpec((1, tn), lambda i, j: (0, j)))
        args += (col_scale.reshape(1, -1),)
    return pl.pallas_call(
        _mm_kernel if col_scale is None else _mm_colscale_kernel,
        grid=(m // tm, n // tn),
        in_specs=in_specs,
        out_specs=pl.BlockSpec((tm, tn), lambda i, j: (i, j)),
        out_shape=jax.ShapeDtypeStruct((m, n), out_dtype),
        compiler_params=_params("parallel", "parallel"), name="projection",
    )(*args)


def _mm_res_kernel(a_ref, w_ref, r_ref, g_ref, o_ref):
    acc = jnp.dot(a_ref[...], w_ref[...].astype(BF16), preferred_element_type=F32)
    o_ref[...] = r_ref[...] + g_ref[...] * acc


def matmul_residual(a, w, res, gate, seq, tm=1024, tn=512):
    m, k = a.shape
    n = w.shape[1]
    tm, tn = min(tm, seq), min(tn, n)
    per_batch = seq // tm
    return pl.pallas_call(
        _mm_res_kernel,
        grid=(m // tm, n // tn),
        in_specs=[pl.BlockSpec((tm, k), lambda i, j: (i, 0)),
                  pl.BlockSpec((k, tn), lambda i, j: (0, j)),
                  pl.BlockSpec((tm, tn), lambda i, j: (i, j)),
                  pl.BlockSpec((None, 1, tn), lambda i, j: (i // per_batch, 0, j))],
        out_specs=pl.BlockSpec((tm, tn), lambda i, j: (i, j)),
        out_shape=jax.ShapeDtypeStruct((m, n), F32),
        compiler_params=_params("parallel", "parallel"), name="projection_residual",
    )(a, w, res, gate)


def _gelu(x):
    return 0.5 * x * (1.0 + lax.erf(x * (2.0 ** -0.5)))


def _gmlp_kernel(u_ref, v_ref, lng_ref, lnb_ref, ws_ref, bs_ref, o_ref, *, groups):
    u = _gelu(u_ref[...].astype(F32))
    v = _gelu(v_ref[...].astype(F32))
    mu = jnp.mean(v, axis=-1, keepdims=True)
    var = jnp.mean(jnp.square(v - mu), axis=-1, keepdims=True)
    vn = ((v - mu) * lax.rsqrt(var + EPS) * lng_ref[...] + lnb_ref[...]).astype(BF16)
    t = ws_ref.shape[1]
    gw = u.shape[1] // groups
    row = lax.broadcasted_iota(I32, (t, t), 0)
    col = lax.broadcasted_iota(I32, (t, t), 1)
    causal = col <= row
    for g in range(groups):
        w = jnp.where(causal, ws_ref[g], 0.0).astype(BF16)
        mixed = jnp.dot(w, vn[:, g * gw:(g + 1) * gw], preferred_element_type=F32) + bs_ref[:, g:g + 1]
        o_ref[:, g * gw:(g + 1) * gw] = (u[:, g * gw:(g + 1) * gw] * mixed).astype(o_ref.dtype)
    o_ref[:, u.shape[1]:] = jnp.zeros((t, o_ref.shape[1] - u.shape[1]), o_ref.dtype)


def gmlp_mixer(proj, a_width, ln_g, ln_b, w_s, b_s, out_width):
    n = proj.shape[0]
    groups, t, _ = w_s.shape
    return pl.pallas_call(
        functools.partial(_gmlp_kernel, groups=groups),
        grid=(n // t,),
        in_specs=[pl.BlockSpec((t, a_width), lambda i: (i, 0)),
                  pl.BlockSpec((t, a_width), lambda i: (i, 1)),
                  pl.BlockSpec((1, a_width), lambda i: (0, 0)),
                  pl.BlockSpec((1, a_width), lambda i: (0, 0)),
                  pl.BlockSpec((groups, t, t), lambda i: (0, 0, 0)),
                  pl.BlockSpec((t, groups), lambda i: (0, 0))],
        out_specs=pl.BlockSpec((t, out_width), lambda i: (i, 0)),
        out_shape=jax.ShapeDtypeStruct((n, out_width), BF16),
        compiler_params=_params("parallel"), name="gmlp_mixer",
    )(proj, proj, ln_g.reshape(1, a_width), ln_b.reshape(1, a_width), w_s, b_s.T)


def _rope_kernel(q_ref, k_ref, x_ref, cb_ref, sb_ref, ci_ref, si_ref, ck_ref, sk_ref,
                 qo_ref, ko_ref, qio_ref, kia_ref, kib_ref, wo_ref, *, heads, idx_pairs, q_scale, w_scale):
    cb, sb = cb_ref[...], sb_ref[...]
    half = HEAD_DIM // 2
    for h in range(heads):
        x = q_ref[:, h * HEAD_DIM:(h + 1) * HEAD_DIM].astype(F32)
        r = x * cb + pltpu.roll(x, half, 1) * sb
        qo_ref[:, h * HEAD_DIM:(h + 1) * HEAD_DIM] = (r * q_scale).astype(qo_ref.dtype)
    x = k_ref[...].astype(F32)
    ko_ref[...] = (x * cb + pltpu.roll(x, half, 1) * sb).astype(ko_ref.dtype)

    lane = lax.broadcasted_iota(I32, cb.shape, 1)
    first_half = (lane % IDX_DIM) < (IDX_DIM // 2)

    def rope_idx(x, c, s):
        rot = jnp.where(first_half, pltpu.roll(x, LANES - IDX_DIM // 2, 1), pltpu.roll(x, IDX_DIM // 2, 1))
        return x * c + rot * s

    ci, si = ci_ref[...], si_ref[...]
    for p in range(idx_pairs):
        x = x_ref[:, p * LANES:(p + 1) * LANES]
        qio_ref[:, p * LANES:(p + 1) * LANES] = rope_idx(x, ci, si).astype(qio_ref.dtype)
    tail = x_ref[:, idx_pairs * LANES:(idx_pairs + 1) * LANES]
    roped = rope_idx(tail, ck_ref[...], sk_ref[...])
    is_k = lane < IDX_DIM
    ka = jnp.where(is_k, roped, 0.0)
    kia_ref[...] = ka.astype(kia_ref.dtype)
    kib_ref[...] = pltpu.roll(ka, IDX_DIM, 1).astype(kib_ref.dtype)
    w = pltpu.roll(tail, LANES - IDX_DIM, 1)
    wo_ref[...] = jnp.where(lane < IDX_HEADS, w * w_scale, 0.0)


def rope_prepare(proj, proj_idx, tables, q_off, heads):
    n = proj.shape[0]
    tm = min(256, n)
    idx_pairs = IDX_HEADS * IDX_DIM // LANES
    qb = q_off // (heads * HEAD_DIM)
    kb = (q_off + heads * HEAD_DIM) // HEAD_DIM
    row = lambda w: pl.BlockSpec((tm, w), lambda i: (i, 0))
    kernel = functools.partial(_rope_kernel, heads=heads, idx_pairs=idx_pairs, q_scale=HEAD_DIM ** -0.5,
                               w_scale=IDX_HEADS ** -0.5 * IDX_DIM ** -0.5)
    return pl.pallas_call(
        kernel,
        grid=(n // tm,),
        in_specs=[pl.BlockSpec((tm, heads * HEAD_DIM), lambda i: (i, qb)),
                  pl.BlockSpec((tm, HEAD_DIM), lambda i: (i, kb)),
                  row(proj_idx.shape[1])] + [row(LANES)] * 6,
        out_specs=[row(heads * HEAD_DIM), row(HEAD_DIM), row(idx_pairs * LANES), row(LANES), row(LANES), row(LANES)],
        out_shape=[jax.ShapeDtypeStruct((n, heads * HEAD_DIM), BF16),
                   jax.ShapeDtypeStruct((n, HEAD_DIM), BF16),
                   jax.ShapeDtypeStruct((n, idx_pairs * LANES), BF16),
                   jax.ShapeDtypeStruct((n, LANES), BF16),
                   jax.ShapeDtypeStruct((n, LANES), BF16),
                   jax.ShapeDtypeStruct((n, LANES), F32)],
        compiler_params=_params("parallel"), name="rope_prepare",
    )(proj, proj, proj_idx, *tables)


def rope_tables(positions):
    pos = positions.reshape(-1).astype(F32)[:, None]

    def cs(dim):
        inv = ROPE_THETA ** (-jnp.arange(0, dim, 2, dtype=F32) / dim)
        ang = pos * inv
        return jnp.cos(ang), jnp.sin(ang)

    cb, sb = cs(HEAD_DIM)
    ci, si = cs(IDX_DIM)
    ones, zeros = jnp.ones_like(cb), jnp.zeros_like(cb)
    return (jnp.concatenate([cb, cb], 1), jnp.concatenate([-sb, sb], 1),
            jnp.concatenate([ci, ci, ci, ci], 1), jnp.concatenate([-si, si, -si, si], 1),
            jnp.concatenate([ci, ci, ones], 1), jnp.concatenate([-si, si, zeros], 1))


DSA_ATTN_TILE = 512


def _sort_key(x):
    x = jnp.where(x == 0.0, 0.0, x)
    bits = pltpu.bitcast(x, I32)
    return jnp.where(bits < 0, bits ^ 0x7FFFFFFF, bits)


def _dsa_kernel(qi_ref, w_ref, kia_ref, kib_ref, q_ref, k_ref, v_ref, buf_ref, o_ref, key_ref,
                *, tq, tk, heads, n_sel):
    del buf_ref
    qt = pl.program_id(1)
    q_lo = qt * tq
    n_kb = (q_lo + tq + tk - 1) // tk
    sub = tk // LANES
    row_pos = q_lo + lax.broadcasted_iota(I32, (tq, LANES), 0)
    lane = lax.broadcasted_iota(I32, (tq, LANES), 1)

    w = w_ref[...]
    w_cols = [jnp.broadcast_to(w[:, h:h + 1], (tq, tk)) for h in range(IDX_HEADS)]

    def score_block(kb, carry):
        k0 = pl.multiple_of(kb * tk, tk)
        ka = kia_ref[pl.ds(k0, tk), :]
        kb_ = kib_ref[pl.ds(k0, tk), :]
        acc = jnp.zeros((tq, tk), F32)
        for p in range(IDX_HEADS // 2):
            x = qi_ref[:, p * LANES:(p + 1) * LANES]
            ra = lax.dot_general(x, ka, (((1,), (1,)), ((), ())), preferred_element_type=F32)
            rb = lax.dot_general(x, kb_, (((1,), (1,)), ((), ())), preferred_element_type=F32)
            acc = acc + w_cols[2 * p] * jnp.maximum(ra, 0.0)
            acc = acc + w_cols[2 * p + 1] * jnp.maximum(rb, 0.0)
        key = _sort_key(acc)
        for j in range(sub):
            col_pos = k0 + j * LANES + lane
            key_ref[kb * sub + j] = jnp.where(col_pos <= row_pos, key[:, j * LANES:(j + 1) * LANES], INT_MIN)
        return carry

    lax.fori_loop(0, n_kb, score_block, 0)
    n_slabs = n_kb * sub

    def count(pred):
        def body(kb, acc):
            hits = [jnp.where(pred(key_ref[kb * sub + j], kb * sub + j), 1, 0) for j in range(sub)]
            while len(hits) > 1:
                hits = [a + b for a, b in zip(hits[::2], hits[1::2])]
            return acc + hits[0]
        part = lax.fori_loop(0, n_kb, body, jnp.zeros((tq, LANES), I32))
        return jnp.sum(part.astype(F32), axis=1, keepdims=True).astype(I32)

    def search_bit(i, u):
        cand = u | (1 << (31 - i))
        thr = jnp.broadcast_to(cand ^ INT_MIN, (tq, LANES))
        total = count(lambda kv, c: kv >= thr)
        return jnp.where(total >= n_sel, cand, u)

    u = lax.fori_loop(0, 32, search_bit, jnp.zeros((tq, 1), I32))
    thr = u ^ INT_MIN
    thr_b = jnp.broadcast_to(thr, (tq, LANES))
    n_gt = count(lambda kv, c: kv > thr_b)
    n_ge = count(lambda kv, c: kv >= thr_b)
    need = n_sel - n_gt
    tie_rows = jnp.logical_and(n_ge > n_sel, need > 0)

    seq_bits = max(1, (key_ref.shape[0] * LANES).bit_length())

    def tie_search():
        need_b = need

        def bit_step(i, j):
            cand = j | (1 << (seq_bits - 1 - i))
            cand_b = jnp.broadcast_to(cand, (tq, LANES))
            total = count(lambda kv, c: jnp.logical_and(kv == thr_b, c * LANES + lane < cand_b))
            return jnp.where(total <= need_b, cand, j)

        return lax.fori_loop(0, seq_bits, bit_step, jnp.zeros((tq, 1), I32))

    any_tie = jnp.max(jnp.where(tie_rows, 1.0, 0.0)) > 0.0
    bound = lax.cond(any_tie, tie_search, lambda: jnp.full((tq, 1), 2 ** 30, I32))
    bound = jnp.where(tie_rows, bound, 2 ** 30)
    bound_b = jnp.broadcast_to(bound, (tq, LANES))

    q_all = jnp.concatenate([q_ref[:, h * HEAD_DIM:(h + 1) * HEAD_DIM] for h in range(heads)], axis=0)

    ta = min(tk, DSA_ATTN_TILE)
    sub_a = ta // LANES

    def attn_block(kb, carry):
        m, l, acc = carry
        k0 = pl.multiple_of(kb * ta, ta)
        kk = k_ref[pl.ds(k0, ta), :]
        vv = v_ref[pl.ds(k0, ta), :]
        s = lax.dot_general(q_all, kk, (((1,), (1,)), ((), ())), preferred_element_type=F32)
        bias_cols = []
        for j in range(sub_a):
            kv = key_ref[kb * sub_a + j]
            col_pos = k0 + j * LANES + lane
            take = jnp.logical_or(kv > thr_b, jnp.logical_and(kv == thr_b, col_pos < bound_b))
            take = jnp.logical_and(take, col_pos <= row_pos)
            bias_cols.append(jnp.where(take, 0.0, MASKED))
        bias = jnp.concatenate(bias_cols, axis=1)
        s = s.reshape(heads, tq, ta) + bias[None]
        m_new = jnp.maximum(m, jnp.max(s, axis=-1, keepdims=True))
        alpha = jnp.exp(m - m_new)
        p = jnp.exp((s - m_new).astype(BF16))
        l = alpha * l + jnp.sum(p.astype(F32), axis=-1, keepdims=True)
        pv = jnp.dot(p.reshape(heads * tq, ta), vv, preferred_element_type=F32)
        acc = alpha * acc + pv.reshape(heads, tq, HEAD_DIM)
        return m_new, l, acc

    init = (jnp.full((heads, tq, 1), MASKED, F32), jnp.zeros((heads, tq, 1), F32),
            jnp.zeros((heads, tq, HEAD_DIM), F32))
    m, l, acc = lax.fori_loop(0, n_kb * (tk // ta), attn_block, init)
    out = acc / l
    for h in range(heads):
        o_ref[:, h * HEAD_DIM:(h + 1) * HEAD_DIM] = out[h].astype(o_ref.dtype)


def dsa_mixer(qi, w_idx, kia, kib, q, k, proj, v_col_block, buf, seq, n_sel):
    n = q.shape[0]
    bsz = n // seq
    heads = q.shape[1] // HEAD_DIM
    tq = min(128, seq)
    tk = min(512, seq)
    nq = seq // tq
    kernel = functools.partial(_dsa_kernel, tq=tq, tk=tk, heads=heads, n_sel=n_sel)
    qrow = lambda w: pl.BlockSpec((tq, w), lambda b, i: (b * nq + i, 0))
    kv = lambda c: pl.BlockSpec((seq, LANES), lambda b, i: (b, c))
    return pl.pallas_call(
        kernel,
        grid=(bsz, nq),
        in_specs=[qrow(qi.shape[1]), qrow(LANES), kv(0), kv(0), qrow(q.shape[1]), kv(0), kv(v_col_block),
                  pl.BlockSpec(memory_space=pl.ANY)],
        out_specs=pl.BlockSpec((tq, heads * HEAD_DIM), lambda b, i: (b * nq + i, 1)),
        out_shape=jax.ShapeDtypeStruct(buf.shape, buf.dtype),
        scratch_shapes=[pltpu.VMEM((seq // LANES, tq, LANES), I32)],
        input_output_aliases={7: 0},
        compiler_params=_params("parallel", "arbitrary"), name="dsa_mixer",
    )(qi, w_idx, kia, kib, q, k, proj, buf)


SB_HEADS_PER_STEP = 4
SB_QUERY_TILE = 512
SB_KEY_TILE = 256
SB_LOGIT_SCALE = HEAD_DIM ** -0.5 * LOG2E


def _sb_kernel(q_ref, k_ref, v_ref, o_ref, *, tq, tk, group):
    qt = pl.program_id(2)
    ratio = tq // tk
    row = lax.broadcasted_iota(I32, (tk, tk), 0)
    col = lax.broadcasted_iota(I32, (tk, tk), 1)
    later = jnp.where(row > col, 1.0, 0.0).astype(BF16)
    q_pos = qt * tq + lax.broadcasted_iota(I32, (tq, tk), 0)
    k_off = lax.broadcasted_iota(I32, (tq, tk), 1)

    def block(kb, masked, carry):
        k0 = pl.multiple_of(kb * tk, tk)
        heads = range(group)
        cols = [slice(h * HEAD_DIM, (h + 1) * HEAD_DIM) for h in heads]
        z = [lax.dot_general(q_ref[:, cols[h]], k_ref[pl.ds(k0, tk), cols[h]], (((1,), (1,)), ((), ())),
                             preferred_element_type=F32).astype(BF16) for h in heads]
        log_beta = [jnp.minimum(z[h], 0.0) - jnp.log(1.0 + jnp.exp2(-jnp.abs(z[h]))) * LOG2E for h in heads]
        log_fail = [log_beta[h] - z[h] for h in heads]
        if masked:
            strict = k0 + k_off < q_pos
            log_fail = [jnp.where(strict, log_fail[h], 0.0) for h in heads]
        between = [jnp.dot(log_fail[h].astype(BF16), later, preferred_element_type=F32) + carry[h][1]
                   for h in heads]
        a = [jnp.exp2(log_beta[h] + between[h]) for h in heads]
        if masked:
            a = [jnp.where(strict, a[h], 0.0) for h in heads]
        acc = [carry[h][0] + jnp.dot(a[h].astype(BF16), v_ref[pl.ds(k0, tk), cols[h]], preferred_element_type=F32)
               for h in heads]
        run = [between[h][:, 0:1] + log_fail[h][:, 0:1] for h in heads]
        return tuple((acc[h], run[h]) for h in heads)

    carry = tuple((jnp.zeros((tq, HEAD_DIM), F32), jnp.zeros((tq, 1), F32)) for _ in range(group))
    first_full = qt * ratio
    for j in reversed(range(ratio)):
        carry = block(first_full + j, True, carry)
    carry = lax.fori_loop(0, first_full, lambda i, c: block(first_full - 1 - i, False, c), carry)
    for h in range(group):
        o_ref[:, h * HEAD_DIM:(h + 1) * HEAD_DIM] = carry[h][0].astype(o_ref.dtype)


def stick_breaking_mixer(qkv, seq, heads):
    n = qkv.shape[0]
    bsz = n // seq
    tq = min(SB_QUERY_TILE, seq)
    tk = min(SB_KEY_TILE, seq)
    nq = seq // tq
    group = SB_HEADS_PER_STEP
    hg = heads // group
    width = group * HEAD_DIM
    kernel = functools.partial(_sb_kernel, tq=tq, tk=tk, group=group)
    return pl.pallas_call(
        kernel,
        grid=(bsz, hg, nq),
        in_specs=[pl.BlockSpec((tq, width), lambda b, h, i: (b * nq + i, h)),
                  pl.BlockSpec((seq, width), lambda b, h, i: (b, hg + h)),
                  pl.BlockSpec((seq, width), lambda b, h, i: (b, 2 * hg + h))],
        out_specs=pl.BlockSpec((tq, width), lambda b, h, i: (b * nq + i, h)),
        out_shape=jax.ShapeDtypeStruct((n, heads * HEAD_DIM), BF16),
        compiler_params=_params("parallel", "parallel", "arbitrary"), name="stick_breaking",
    )(qkv, qkv, qkv)


def _router_kernel(h_ref, wr_ref, b_ref, s_ref, w_ref, c_ref, cnt_ref, *, n_experts):
    per_group = n_experts // N_GROUPS
    tm = h_ref.shape[0]
    logits = lax.dot_general(wr_ref[...], h_ref[...], (((1,), (1,)), ((), ())), preferred_element_type=F32)
    scores = jax.nn.sigmoid(logits)
    choice = (scores + b_ref[...]).reshape(N_GROUPS, per_group, tm)
    s3 = scores.reshape(N_GROUPS, per_group, tm)
    neg = -jnp.inf
    in_group = lax.broadcasted_iota(I32, choice.shape, 1)
    m1 = jnp.max(choice, axis=1, keepdims=True)
    first = jnp.min(jnp.where(choice == m1, in_group, per_group), axis=1, keepdims=True)
    m2 = jnp.max(jnp.where(in_group == first, neg, choice), axis=1, keepdims=True)
    group_score = m1 + m2
    gid = lax.broadcasted_iota(I32, group_score.shape, 0)
    group_sel = jnp.zeros(group_score.shape, jnp.bool_)
    for _ in range(TOPK_GROUPS):
        m = jnp.max(group_score, axis=0, keepdims=True)
        f = jnp.min(jnp.where(group_score == m, gid, N_GROUPS), axis=0, keepdims=True)
        hit = gid == f
        group_sel = jnp.logical_or(group_sel, hit)
        group_score = jnp.where(hit, neg, group_score)
    cand = jnp.where(group_sel, choice, neg)
    eid = lax.broadcasted_iota(I32, choice.shape, 0) * per_group + in_group
    sel = jnp.zeros(choice.shape, jnp.bool_)
    hits = []
    for _ in range(TOP_K):
        m = jnp.max(jnp.max(cand, axis=1, keepdims=True), axis=0, keepdims=True)
        f = jnp.min(jnp.min(jnp.where(cand == m, eid, n_experts), axis=1, keepdims=True), axis=0, keepdims=True)
        hit = eid == f
        hits.append((hit, f))
        sel = jnp.logical_or(sel, hit)
        cand = jnp.where(hit, neg, cand)
    top_w = jnp.where(sel, s3, 0.0)
    total = jnp.sum(jnp.sum(top_w, axis=1, keepdims=True), axis=0, keepdims=True)
    gates = top_w / total * ROUTED_SCALE

    @pl.when(pl.program_id(0) == 0)
    def _():
        cnt_ref[...] = jnp.zeros_like(cnt_ref)

    sel_f = jnp.where(sel, 1.0, 0.0).reshape(n_experts, tm)
    row = lax.broadcasted_iota(I32, (tm, tm), 0)
    col = lax.broadcasted_iota(I32, (tm, tm), 1)
    before = jnp.where(row < col, 1.0, 0.0).astype(BF16)
    prefix = jnp.dot(sel_f.astype(BF16), before, preferred_element_type=F32)
    rank_all = (prefix + cnt_ref[:, 0:1]).reshape(choice.shape)

    def pick(hit, val):
        return jnp.sum(jnp.sum(jnp.where(hit, val, 0.0), axis=1, keepdims=True), axis=0, keepdims=True)

    for k, (hit, f) in enumerate(hits):
        rank = pick(hit, rank_all).reshape(1, tm).astype(I32)
        s_ref[k:k + 1, :] = rank | lax.shift_left(f.reshape(1, tm), SLOT_RANK_BITS)
        w_ref[k:k + 1, :] = pick(hit, gates).reshape(1, tm)
    cnt_ref[...] = cnt_ref[...] + jnp.sum(sel_f, axis=1, keepdims=True)
    c_ref[...] = cnt_ref[...].astype(I32)


def moe_router(h, w_router, bias):
    n, d = h.shape
    e = w_router.shape[1]
    tm = min(512, n)
    top = lambda dt: jax.ShapeDtypeStruct((TOP_K, n), dt)
    top_spec = pl.BlockSpec((TOP_K, tm), lambda i: (0, i))
    slots, wts, counts = pl.pallas_call(
        functools.partial(_router_kernel, n_experts=e),
        grid=(n // tm,),
        in_specs=[pl.BlockSpec((tm, d), lambda i: (i, 0)),
                  pl.BlockSpec((e, d), lambda i: (0, 0)),
                  pl.BlockSpec((e, 1), lambda i: (0, 0))],
        out_specs=[top_spec, top_spec, pl.BlockSpec((e, LANES), lambda i: (0, 0))],
        out_shape=[top(I32), top(F32), jax.ShapeDtypeStruct((e, LANES), I32)],
        scratch_shapes=[pltpu.VMEM((e, LANES), F32)],
        compiler_params=_params("arbitrary"), name="moe_router",
    )(h, w_router.T.astype(BF16), bias.reshape(e, 1))
    return slots.reshape(-1), wts, counts[:, 0]


MOE_TILE = 512
DISPATCH_TOKENS = 256
COMBINE_TOKENS = 128
DMA_ISSUE_UNROLL = 8
PAD_CHUNK = 1024


def _pack_pairs(x):
    w = x.shape[1] // 2
    lo = pltpu.bitcast(x[:, :w].astype(BF16).astype(F32), I32)
    hi = pltpu.bitcast(x[:, w:].astype(BF16).astype(F32), I32)
    return lax.shift_right_logical(lo, 16) | hi


def _unpack_pairs(p):
    lo = pltpu.bitcast(lax.shift_left(p, 16), F32)
    hi = pltpu.bitcast(p & jnp.int32(-65536), F32)
    return lo, hi


def _store_slabs(ref, x):
    r, n_slabs, _ = ref.shape
    flat = ref.reshape(r * n_slabs, LANES)
    for s in range(n_slabs):
        flat[pl.ds(s, r, stride=n_slabs), :] = x[:, s * LANES:(s + 1) * LANES]


def _load_slab_column(ref, s):
    r, n_slabs, _ = ref.shape
    return ref.reshape(r * n_slabs, LANES)[pl.ds(s, r, stride=n_slabs), :]


def _load_slabs(ref):
    return jnp.concatenate([_load_slab_column(ref, s) for s in range(ref.shape[1])], axis=1)


SLOT_RANK_BITS = 20


def _slot_row(code, off_ref):
    return off_ref[lax.shift_right_logical(code, SLOT_RANK_BITS)] + (code & ((1 << SLOT_RANK_BITS) - 1))


def _dispatch_kernel(slot_ref, off_ref, cnt_ref, hp_ref, xs_ref, zero_ref, sem, pad_sem, *, td, n_tok, n_rows):
    base = pl.program_id(0) * td

    @pl.when(pl.program_id(0) == 0)
    def _():
        zero_ref[...] = jnp.zeros(zero_ref.shape, zero_ref.dtype)
        n_experts = cnt_ref.shape[0]

        def zero_row(r, carry):
            pltpu.make_async_copy(zero_ref, xs_ref.at[r], pad_sem).start()
            return carry

        def drain(r, carry):
            pltpu.make_async_copy(zero_ref, xs_ref.at[0], pad_sem).wait()
            return carry

        def pad_chunk(c, bounds):
            start, end = bounds
            lo = start + c * PAD_CHUNK
            hi = jnp.minimum(lo + PAD_CHUNK, end)
            lax.fori_loop(lo, hi, zero_row, 0)
            lax.fori_loop(lo, hi, drain, 0)
            return bounds

        def pad_expert(e, carry):
            start = off_ref[e] + cnt_ref[e]
            end = jnp.where(e + 1 < n_experts, off_ref[jnp.minimum(e + 1, n_experts - 1)], n_rows)
            lax.fori_loop(0, (end - start + PAD_CHUNK - 1) // PAD_CHUNK, pad_chunk, (start, end))
            return carry

        lax.fori_loop(0, n_experts, pad_expert, 0)

    for k in range(TOP_K):
        def issue(t, carry, k=k):
            row = _slot_row(slot_ref[k * n_tok + base + t], off_ref)
            pltpu.make_async_copy(hp_ref.at[t], xs_ref.at[row], sem).start()
            return carry

        lax.fori_loop(0, td, issue, 0, unroll=DMA_ISSUE_UNROLL)
    for k in range(TOP_K):
        pltpu.make_async_copy(hp_ref, xs_ref.at[pl.ds(0, td)], sem).wait()


def moe_dispatch(hp, slots, offsets, counts, n_rows):
    n, s, _ = hp.shape
    td = min(DISPATCH_TOKENS, n)
    kernel = functools.partial(_dispatch_kernel, td=td, n_tok=n, n_rows=n_rows)
    return pl.pallas_call(
        kernel,
        grid_spec=pltpu.PrefetchScalarGridSpec(
            num_scalar_prefetch=3,
            grid=(n // td,),
            in_specs=[pl.BlockSpec((td, s, LANES), lambda i, *_: (i, 0, 0))],
            out_specs=pl.BlockSpec(memory_space=pl.ANY),
            scratch_shapes=[pltpu.VMEM((s, LANES), I32), pltpu.SemaphoreType.DMA(()),
                            pltpu.SemaphoreType.DMA(())]),
        out_shape=jax.ShapeDtypeStruct((n_rows, s, LANES), I32),
        compiler_params=_params("arbitrary"), name="moe_dispatch",
    )(slots, offsets, counts, hp)


def _swiglu(lo, hi, wg_ref, wu_ref, wd_ref):
    w = lo.shape[1]
    g = (jnp.dot(lo, wg_ref[:w, :].astype(BF16), preferred_element_type=F32)
         + jnp.dot(hi, wg_ref[w:, :].astype(BF16), preferred_element_type=F32))
    u = (jnp.dot(lo, wu_ref[:w, :].astype(BF16), preferred_element_type=F32)
         + jnp.dot(hi, wu_ref[w:, :].astype(BF16), preferred_element_type=F32))
    act = g * jax.nn.sigmoid(g) * u
    return jnp.dot(act.astype(BF16), wd_ref[...].astype(BF16), preferred_element_type=F32)


def _grouped_kernel(te_ref, nu_ref, x_ref, wg_ref, wu_ref, wd_ref, y_ref):
    del te_ref

    @pl.when(pl.program_id(0) < nu_ref[0])
    def _():
        lo, hi = _unpack_pairs(_load_slabs(x_ref))
        _store_slabs(y_ref, _pack_pairs(_swiglu(lo.astype(BF16), hi.astype(BF16), wg_ref, wu_ref, wd_ref)))


def moe_grouped(xs, tile_expert, n_used, wg, wu, wd):
    p, s, _ = xs.shape
    e, d, f = wg.shape
    tm = MOE_TILE
    row = lambda i, te, nu: (jnp.minimum(i, nu[0] - 1), 0, 0)
    return pl.pallas_call(
        _grouped_kernel,
        grid_spec=pltpu.PrefetchScalarGridSpec(
            num_scalar_prefetch=2,
            grid=(p // tm,),
            in_specs=[pl.BlockSpec((tm, s, LANES), row),
                      pl.BlockSpec((None, d, f), lambda i, te, nu: (te[i], 0, 0)),
                      pl.BlockSpec((None, d, f), lambda i, te, nu: (te[i], 0, 0)),
                      pl.BlockSpec((None, f, d), lambda i, te, nu: (te[i], 0, 0))],
            out_specs=pl.BlockSpec((tm, s, LANES), row)),
        out_shape=jax.ShapeDtypeStruct((p, s, LANES), I32),
        compiler_params=_params("arbitrary"), name="moe_grouped",
    )(tile_expert, n_used, xs, wg, wu, wd)


def _shared_kernel(h_ref, wg_ref, wu_ref, wd_ref, o_ref):
    x = h_ref[...]
    w = x.shape[1] // 2
    o_ref[...] = _swiglu(x[:, :w], x[:, w:], wg_ref, wu_ref, wd_ref).astype(o_ref.dtype)


def shared_expert(h, wg, wu, wd):
    n, d = h.shape
    f = wg.shape[1]
    tm = min(512, n)
    return pl.pallas_call(
        _shared_kernel,
        grid=(n // tm,),
        in_specs=[pl.BlockSpec((tm, d), lambda i: (i, 0)),
                  pl.BlockSpec((d, f), lambda i: (0, 0)),
                  pl.BlockSpec((d, f), lambda i: (0, 0)),
                  pl.BlockSpec((f, d), lambda i: (0, 0))],
        out_specs=pl.BlockSpec((tm, d), lambda i: (i, 0)),
        out_shape=jax.ShapeDtypeStruct((n, d), BF16),
        compiler_params=_params("parallel"), name="shared_expert",
    )(h, wg, wu, wd)


def _combine_kernel(slot_ref, off_ref, ys_ref, w_ref, x_ref, sh_ref, g_ref, o_ref, rows_ref, sem, *, tc, n_tok):
    base = pl.program_id(0) * tc

    for k in range(TOP_K):
        def issue(t, carry, k=k):
            row = _slot_row(slot_ref[k * n_tok + base + t], off_ref)
            pltpu.make_async_copy(ys_ref.at[row], rows_ref.at[k, t], sem).start()
            return carry

        lax.fori_loop(0, tc, issue, 0, unroll=DMA_ISSUE_UNROLL)
    for k in range(TOP_K):
        pltpu.make_async_copy(ys_ref.at[pl.ds(0, tc)], rows_ref.at[k], sem).wait()

    half = x_ref.shape[1] // 2
    w_cols = [w_ref[:, k:k + 1] for k in range(TOP_K)]
    for s in range(rows_ref.shape[2]):
        acc_lo = jnp.zeros((tc, LANES), F32)
        acc_hi = jnp.zeros((tc, LANES), F32)
        for k in range(TOP_K):
            lo, hi = _unpack_pairs(_load_slab_column(rows_ref.at[k], s))
            acc_lo = acc_lo + w_cols[k] * lo
            acc_hi = acc_hi + w_cols[k] * hi
        for acc, c0 in ((acc_lo, s * LANES), (acc_hi, half + s * LANES)):
            cols = slice(c0, c0 + LANES)
            o_ref[:, cols] = x_ref[:, cols] + g_ref[:, cols] * (acc + sh_ref[:, cols].astype(F32))


def moe_combine(ys, slots, offsets, wts, x2, shared, gate, seq):
    n, d = x2.shape
    s = ys.shape[1]
    tc = min(COMBINE_TOKENS, seq)
    per_batch = seq // tc
    kernel = functools.partial(_combine_kernel, tc=tc, n_tok=n)
    return pl.pallas_call(
        kernel,
        grid_spec=pltpu.PrefetchScalarGridSpec(
            num_scalar_prefetch=2,
            grid=(n // tc,),
            in_specs=[pl.BlockSpec(memory_space=pl.ANY),
                      pl.BlockSpec((tc, TOP_K), lambda i, *_: (i, 0)),
                      pl.BlockSpec((tc, d), lambda i, *_: (i, 0)),
                      pl.BlockSpec((tc, d), lambda i, *_: (i, 0)),
                      pl.BlockSpec((None, 1, d), lambda i, *_: (i // per_batch, 0, 0))],
            out_specs=pl.BlockSpec((tc, d), lambda i, *_: (i, 0)),
            scratch_shapes=[pltpu.VMEM((TOP_K, tc, s, LANES), I32), pltpu.SemaphoreType.DMA(())]),
        out_shape=jax.ShapeDtypeStruct((n, d), F32),
        compiler_params=_params("arbitrary"), name="moe_combine",
    )(slots, offsets, ys, wts, x2, shared, gate)


def moe_layout(counts, n_tiles):
    e = counts.shape[0]
    tiles = (counts + MOE_TILE - 1) // MOE_TILE
    earlier = jnp.arange(e)[None, :] <= jnp.arange(e)[:, None]
    tile_end = jnp.sum(jnp.where(earlier, tiles[None, :], 0), axis=1)
    offsets = (tile_end - tiles) * MOE_TILE
    n_used = tile_end[-1]
    tile_ids = jnp.minimum(jnp.arange(n_tiles, dtype=I32), n_used - 1)
    tile_expert = jnp.sum(tile_end[None, :] <= tile_ids[:, None], axis=1)
    return offsets.astype(I32), tile_expert.astype(I32), n_used.reshape(1).astype(I32)


def _pad_cols(w, width):
    return jnp.pad(w, ((0, 0), (0, width - w.shape[1])))


def kernel(x, c, positions, ab_w_in, ab_w_out, gmlp_ln_g, gmlp_ln_b, gmlp_w_s, gmlp_b_s, sb_w_qkv, sb_w_out,
           norm_mix_g, ada_mix_w, ada_mix_b, norm_ffn_g, ada_ffn_w, ada_ffn_b, router_w, router_bias,
           expert_w_gate, expert_w_up, expert_w_down, shared_w_gate, shared_w_up, shared_w_down, final_norm_g):
    bsz, seq, d = x.shape
    depth = norm_mix_g.shape[0]
    a_width = gmlp_ln_g.shape[1]
    b_width = ab_w_out.shape[1] - a_width
    b_heads = b_width // HEAD_DIM
    c_heads = sb_w_out.shape[1] // HEAD_DIM
    n_sel = min(TOPK_MAX, seq // 4)
    main_width = 2 * a_width + b_width + 2 * HEAD_DIM
    idx_width = IDX_HEADS * IDX_DIM + LANES

    mod_mix = ada_modulation_all(c, ada_mix_w, ada_mix_b)
    mod_ffn = ada_modulation_all(c, ada_ffn_w, ada_ffn_b)
    tables = rope_tables(positions)
    x2 = x.reshape(bsz * seq, d)

    for layer in range(depth):
        j = layer // 2
        shift, scale, gate = mod_mix[layer, :, 0], mod_mix[layer, :, 1], mod_mix[layer, :, 2]
        h = norm_modulate(x2, norm_mix_g[layer], scale, shift, seq)
        if layer % 2 == 0:
            w_in = ab_w_in[j]
            proj = matmul(h, w_in, BF16, tn=640 if main_width % 640 == 0 else 128, n_out=main_width)
            proj_idx = matmul(h, _pad_cols(w_in[:, main_width:], idx_width).astype(BF16), F32, tn=idx_width)
            q, k, qi, kia, kib, w_idx = rope_prepare(proj, proj_idx, tables, 2 * a_width, b_heads)
            buf = gmlp_mixer(proj, a_width, gmlp_ln_g[j], gmlp_ln_b[j], gmlp_w_s[j], gmlp_b_s[j],
                             a_width + b_width)
            v_col_block = (2 * a_width + b_width + HEAD_DIM) // HEAD_DIM
            mixed = dsa_mixer(qi, w_idx, kia, kib, q, k, proj, v_col_block, buf, seq, n_sel)
            x2 = matmul_residual(mixed, ab_w_out[j], x2, gate, seq)
        else:
            c_width = c_heads * HEAD_DIM
            col_scale = jnp.where(jnp.arange(3 * c_width) < c_width, SB_LOGIT_SCALE, 1.0).astype(F32)
            qkv = matmul(h, sb_w_qkv[j], BF16, col_scale=col_scale)
            o = stick_breaking_mixer(qkv, seq, c_heads)
            x2 = matmul_residual(o, sb_w_out[j], x2, gate, seq)

        shift, scale, gate = mod_ffn[layer, :, 0], mod_ffn[layer, :, 1], mod_ffn[layer, :, 2]
        h, h_packed = norm_modulate(x2, norm_ffn_g[layer], scale, shift, seq, pack=True)
        slots, wts, counts = moe_router(h, router_w[layer], router_bias[layer])
        n_tiles = bsz * seq * TOP_K // MOE_TILE + router_w.shape[2]
        offsets, tile_expert, n_used = moe_layout(counts, n_tiles)
        xs = moe_dispatch(h_packed, slots, offsets, counts, n_tiles * MOE_TILE)
        ys = moe_grouped(xs, tile_expert, n_used, expert_w_gate[layer], expert_w_up[layer], expert_w_down[layer])
        shared = shared_expert(h, shared_w_gate[layer], shared_w_up[layer], shared_w_down[layer])
        x2 = moe_combine(ys, slots, offsets, wts.T, x2, shared, gate, seq)

    return final_norm(x2, final_norm_g).reshape(bsz, seq, d)
```

```python
import functools

import jax
import jax.numpy as jnp
from jax import lax
from jax.experimental import pallas as pl
from jax.experimental.pallas import tpu as pltpu

F32 = jnp.float32
BF16 = jnp.bfloat16
I32 = jnp.int32

EPS = 1e-6
ROPE_THETA = 10000.0
LANES = 128
HEAD_DIM = 128
A_GROUPS = 8
A_CHUNK = 128
IDX_HEADS = 16
IDX_DIM = 64
TOPK_MAX = 256
N_GROUPS = 8
TOPK_GROUPS = 4
TOP_K = 8
ROUTED_SCALE = 2.5
VMEM_LIMIT = 56 * 1024 * 1024
INT_MIN = -(2 ** 31)
MASKED = -1e30
LOG2E = 1.4426950408889634


def _params(*sem):
    return pltpu.CompilerParams(dimension_semantics=sem, vmem_limit_bytes=VMEM_LIMIT)


def _mod_kernel(c_ref, w_ref, b_ref, o_ref):
    c = c_ref[...]
    s = c * jax.nn.sigmoid(c)
    o_ref[...] = jnp.dot(s.astype(BF16), w_ref[...].astype(BF16), preferred_element_type=F32) + b_ref[...]


def ada_modulation_all(c, w, b):
    bsz, d = c.shape
    n_layers, _, n_out = w.shape
    rows = 8
    c_pad = jnp.zeros((rows, d), F32).at[:bsz].set(c)
    tn = min(512, n_out)
    out = pl.pallas_call(
        _mod_kernel,
        grid=(n_layers, n_out // tn),
        in_specs=[pl.BlockSpec((rows, d), lambda l, j: (0, 0)),
                  pl.BlockSpec((None, d, tn), lambda l, j: (l, 0, j)),
                  pl.BlockSpec((None, 1, tn), lambda l, j: (l, 0, j))],
        out_specs=pl.BlockSpec((None, rows, tn), lambda l, j: (l, 0, j)),
        out_shape=jax.ShapeDtypeStruct((n_layers, rows, n_out), F32),
        compiler_params=_params("parallel", "parallel"), name="ada_modulation",
    )(c_pad, w, b.reshape(n_layers, 1, n_out))
    return out[:, :bsz].reshape(n_layers, bsz, 3, 1, d)


def _norm_mod_kernel(x_ref, g_ref, sc_ref, sh_ref, o_ref):
    x = x_ref[...]
    y = x * lax.rsqrt(jnp.mean(x * x, axis=-1, keepdims=True) + EPS)
    o_ref[...] = ((y * g_ref[...]) * (1.0 + sc_ref[...]) + sh_ref[...]).astype(o_ref.dtype)


def _norm_mod_pack_kernel(x_ref, g_ref, sc_ref, sh_ref, o_ref, p_ref):
    x = x_ref[...]
    y = x * lax.rsqrt(jnp.mean(x * x, axis=-1, keepdims=True) + EPS)
    h = (y * g_ref[...]) * (1.0 + sc_ref[...]) + sh_ref[...]
    o_ref[...] = h.astype(o_ref.dtype)
    _store_slabs(p_ref, _pack_pairs(h))


def norm_modulate(x2, g, scale, shift, seq, pack=False):
    n, d = x2.shape
    tm = min(256, seq)
    per_batch = seq // tm
    row = lambda w: pl.BlockSpec((tm, w), lambda i: (i, 0))
    out_specs, out_shape = row(d), jax.ShapeDtypeStruct((n, d), BF16)
    if pack:
        slabs = d // 2 // LANES
        out_specs = [out_specs, pl.BlockSpec((tm, slabs, LANES), lambda i: (i, 0, 0))]
        out_shape = [out_shape, jax.ShapeDtypeStruct((n, slabs, LANES), I32)]
    return pl.pallas_call(
        _norm_mod_pack_kernel if pack else _norm_mod_kernel,
        grid=(n // tm,),
        in_specs=[row(d),
                  pl.BlockSpec((1, d), lambda i: (0, 0)),
                  pl.BlockSpec((None, 1, d), lambda i: (i // per_batch, 0, 0)),
                  pl.BlockSpec((None, 1, d), lambda i: (i // per_batch, 0, 0))],
        out_specs=out_specs,
        out_shape=out_shape,
        compiler_params=_params("parallel"), name="norm_modulate",
    )(x2, g.reshape(1, d), scale, shift)


def _final_norm_kernel(x_ref, g_ref, o_ref):
    x = x_ref[...]
    y = x * lax.rsqrt(jnp.mean(x * x, axis=-1, keepdims=True) + EPS)
    o_ref[...] = y * g_ref[...]


def final_norm(x2, g):
    n, d = x2.shape
    tm = min(256, n)
    return pl.pallas_call(
        _final_norm_kernel,
        grid=(n // tm,),
        in_specs=[pl.BlockSpec((tm, d), lambda i: (i, 0)), pl.BlockSpec((1, d), lambda i: (0, 0))],
        out_specs=pl.BlockSpec((tm, d), lambda i: (i, 0)),
        out_shape=jax.ShapeDtypeStruct((n, d), F32),
        compiler_params=_params("parallel"), name="final_norm",
    )(x2, g.reshape(1, d))


def _mm_kernel(a_ref, w_ref, o_ref):
    o_ref[...] = jnp.dot(a_ref[...], w_ref[...].astype(BF16), preferred_element_type=F32).astype(o_ref.dtype)


def _mm_colscale_kernel(a_ref, w_ref, s_ref, o_ref):
    w = (w_ref[...] * s_ref[...]).astype(BF16)
    o_ref[...] = jnp.dot(a_ref[...], w, preferred_element_type=F32).astype(o_ref.dtype)


def matmul(a, w, out_dtype, tm=1024, tn=512, n_out=None, col_scale=None):
    m, k = a.shape
    n = w.shape[1] if n_out is None else n_out
    tm, tn = min(tm, m), min(tn, n)
    in_specs = [pl.BlockSpec((tm, k), lambda i, j: (i, 0)),
                pl.BlockSpec((k, tn), lambda i, j: (0, j))]
    args = (a, w)
    if col_scale is not None:
        in_specs.append(pl.BlockSpec((1, tn), lambda i, j: (0, j)))
        args += (col_scale.reshape(1, -1),)
    return pl.pallas_call(
        _mm_kernel if col_scale is None else _mm_colscale_kernel,
        grid=(m // tm, n // tn),
        in_specs=in_specs,
        out_specs=pl.BlockSpec((tm, tn), lambda i, j: (i, j)),
        out_shape=jax.ShapeDtypeStruct((m, n), out_dtype),
        compiler_params=_params("parallel", "parallel"), name="projection",
    )(*args)


def _mm_res_kernel(a_ref, w_ref, r_ref, g_ref, o_ref):
    acc = jnp.dot(a_ref[...], w_ref[...].astype(BF16), preferred_element_type=F32)
    o_ref[...] = r_ref[...] + g_ref[...] * acc


def matmul_residual(a, w, res, gate, seq, tm=1024, tn=512):
    m, k = a.shape
    n = w.shape[1]
    tm, tn = min(tm, seq), min(tn, n)
    per_batch = seq // tm
    return pl.pallas_call(
        _mm_res_kernel,
        grid=(m // tm, n // tn),
        in_specs=[pl.BlockSpec((tm, k), lambda i, j: (i, 0)),
                  pl.BlockSpec((k, tn), lambda i, j: (0, j)),
                  pl.BlockSpec((tm, tn), lambda i, j: (i, j)),
                  pl.BlockSpec((None, 1, tn), lambda i, j: (i // per_batch, 0, j))],
        out_specs=pl.BlockSpec((tm, tn), lambda i, j: (i, j)),
        out_shape=jax.ShapeDtypeStruct((m, n), F32),
        compiler_params=_params("parallel", "parallel"), name="projection_residual",
    )(a, w, res, gate)


def _gelu(x):
    return 0.5 * x * (1.0 + lax.erf(x * (2.0 ** -0.5)))


def _gmlp_kernel(u_ref, v_ref, lng_ref, lnb_ref, ws_ref, bs_ref, o_ref, *, groups):
    u = _gelu(u_ref[...].astype(F32))
    v = _gelu(v_ref[...].astype(F32))
    mu = jnp.mean(v, axis=-1, keepdims=True)
    var = jnp.mean(jnp.square(v - mu), axis=-1, keepdims=True)
    vn = ((v - mu) * lax.rsqrt(var + EPS) * lng_ref[...] + lnb_ref[...]).astype(BF16)
    t = ws_ref.shape[1]
    gw = u.shape[1] // groups
    row = lax.broadcasted_iota(I32, (t, t), 0)
    col = lax.broadcasted_iota(I32, (t, t), 1)
    causal = col <= row
    for g in range(groups):
        w = jnp.where(causal, ws_ref[g], 0.0).astype(BF16)
        mixed = jnp.dot(w, vn[:, g * gw:(g + 1) * gw], preferred_element_type=F32) + bs_ref[:, g:g + 1]
        o_ref[:, g * gw:(g + 1) * gw] = (u[:, g * gw:(g + 1) * gw] * mixed).astype(o_ref.dtype)
    o_ref[:, u.shape[1]:] = jnp.zeros((t, o_ref.shape[1] - u.shape[1]), o_ref.dtype)


def gmlp_mixer(proj, a_width, ln_g, ln_b, w_s, b_s, out_width):
    n = proj.shape[0]
    groups, t, _ = w_s.shape
    return pl.pallas_call(
        functools.partial(_gmlp_kernel, groups=groups),
        grid=(n // t,),
        in_specs=[pl.BlockSpec((t, a_width), lambda i: (i, 0)),
                  pl.BlockSpec((t, a_width), lambda i: (i, 1)),
                  pl.BlockSpec((1, a_width), lambda i: (0, 0)),
                  pl.BlockSpec((1, a_width), lambda i: (0, 0)),
                  pl.BlockSpec((groups, t, t), lambda i: (0, 0, 0)),
                  pl.BlockSpec((t, groups), lambda i: (0, 0))],
        out_specs=pl.BlockSpec((t, out_width), lambda i: (i, 0)),
        out_shape=jax.ShapeDtypeStruct((n, out_width), BF16),
        compiler_params=_params("parallel"), name="gmlp_mixer",
    )(proj, proj, ln_g.reshape(1, a_width), ln_b.reshape(1, a_width), w_s, b_s.T)


def _rope_kernel(q_ref, k_ref, x_ref, cb_ref, sb_ref, ci_ref, si_ref, ck_ref, sk_ref,
                 qo_ref, ko_ref, qio_ref, kia_ref, kib_ref, wo_ref, *, heads, idx_pairs, q_scale, w_scale):
    cb, sb = cb_ref[...], sb_ref[...]
    half = HEAD_DIM // 2
    for h in range(heads):
        x = q_ref[:, h * HEAD_DIM:(h + 1) * HEAD_DIM].astype(F32)
        r = x * cb + pltpu.roll(x, half, 1) * sb
        qo_ref[:, h * HEAD_DIM:(h + 1) * HEAD_DIM] = (r * q_scale).astype(qo_ref.dtype)
    x = k_ref[...].astype(F32)
    ko_ref[...] = (x * cb + pltpu.roll(x, half, 1) * sb).astype(ko_ref.dtype)

    lane = lax.broadcasted_iota(I32, cb.shape, 1)
    first_half = (lane % IDX_DIM) < (IDX_DIM // 2)

    def rope_idx(x, c, s):
        rot = jnp.where(first_half, pltpu.roll(x, LANES - IDX_DIM // 2, 1), pltpu.roll(x, IDX_DIM // 2, 1))
        return x * c + rot * s

    ci, si = ci_ref[...], si_ref[...]
    for p in range(idx_pairs):
        x = x_ref[:, p * LANES:(p + 1) * LANES]
        qio_ref[:, p * LANES:(p + 1) * LANES] = rope_idx(x, ci, si).astype(qio_ref.dtype)
    tail = x_ref[:, idx_pairs * LANES:(idx_pairs + 1) * LANES]
    roped = rope_idx(tail, ck_ref[...], sk_ref[...])
    is_k = lane < IDX_DIM
    ka = jnp.where(is_k, roped, 0.0)
    kia_ref[...] = ka.astype(kia_ref.dtype)
    kib_ref[...] = pltpu.roll(ka, IDX_DIM, 1).astype(kib_ref.dtype)
    w = pltpu.roll(tail, LANES - IDX_DIM, 1)
    wo_ref[...] = jnp.where(lane < IDX_HEADS, w * w_scale, 0.0)


def rope_prepare(proj, proj_idx, tables, q_off, heads):
    n = proj.shape[0]
    tm = min(256, n)
    idx_pairs = IDX_HEADS * IDX_DIM // LANES
    qb = q_off // (heads * HEAD_DIM)
    kb = (q_off + heads * HEAD_DIM) // HEAD_DIM
    row = lambda w: pl.BlockSpec((tm, w), lambda i: (i, 0))
    kernel = functools.partial(_rope_kernel, heads=heads, idx_pairs=idx_pairs, q_scale=HEAD_DIM ** -0.5,
                               w_scale=IDX_HEADS ** -0.5 * IDX_DIM ** -0.5)
    return pl.pallas_call(
        kernel,
        grid=(n // tm,),
        in_specs=[pl.BlockSpec((tm, heads * HEAD_DIM), lambda i: (i, qb)),
                  pl.BlockSpec((tm, HEAD_DIM), lambda i: (i, kb)),
                  row(proj_idx.shape[1])] + [row(LANES)] * 6,
        out_specs=[row(heads * HEAD_DIM), row(HEAD_DIM), row(idx_pairs * LANES), row(LANES), row(LANES), row(LANES)],
        out_shape=[jax.ShapeDtypeStruct((n, heads * HEAD_DIM), BF16),
                   jax.ShapeDtypeStruct((n, HEAD_DIM), BF16),
                   jax.ShapeDtypeStruct((n, idx_pairs * LANES), BF16),
                   jax.ShapeDtypeStruct((n, LANES), BF16),
                   jax.ShapeDtypeStruct((n, LANES), BF16),
                   jax.ShapeDtypeStruct((n, LANES), F32)],
        compiler_params=_params("parallel"), name="rope_prepare",
    )(proj, proj, proj_idx, *tables)


def rope_tables(positions):
    pos = positions.reshape(-1).astype(F32)[:, None]

    def cs(dim):
        inv = ROPE_THETA ** (-jnp.arange(0, dim, 2, dtype=F32) / dim)
        ang = pos * inv
        return jnp.cos(ang), jnp.sin(ang)

    cb, sb = cs(HEAD_DIM)
    ci, si = cs(IDX_DIM)
    ones, zeros = jnp.ones_like(cb), jnp.zeros_like(cb)
    return (jnp.concatenate([cb, cb], 1), jnp.concatenate([-sb, sb], 1),
            jnp.concatenate([ci, ci, ci, ci], 1), jnp.concatenate([-si, si, -si, si], 1),
            jnp.concatenate([ci, ci, ones], 1), jnp.concatenate([-si, si, zeros], 1))


DSA_ATTN_TILE = 512


def _sort_key(x):
    x = jnp.where(x == 0.0, 0.0, x)
    bits = pltpu.bitcast(x, I32)
    return jnp.where(bits < 0, bits ^ 0x7FFFFFFF, bits)


def _dsa_kernel(qi_ref, w_ref, kia_ref, kib_ref, q_ref, k_ref, v_ref, buf_ref, o_ref, key_ref,
                *, tq, tk, heads, n_sel):
    del buf_ref
    qt = pl.program_id(1)
    q_lo = qt * tq
    n_kb = (q_lo + tq + tk - 1) // tk
    sub = tk // LANES
    row_pos = q_lo + lax.broadcasted_iota(I32, (tq, LANES), 0)
    lane = lax.broadcasted_iota(I32, (tq, LANES), 1)

    w = w_ref[...]
    w_cols = [jnp.broadcast_to(w[:, h:h + 1], (tq, tk)) for h in range(IDX_HEADS)]

    def score_block(kb, carry):
        k0 = pl.multiple_of(kb * tk, tk)
        ka = kia_ref[pl.ds(k0, tk), :]
        kb_ = kib_ref[pl.ds(k0, tk), :]
        acc = jnp.zeros((tq, tk), F32)
        for p in range(IDX_HEADS // 2):
            x = qi_ref[:, p * LANES:(p + 1) * LANES]
            ra = lax.dot_general(x, ka, (((1,), (1,)), ((), ())), preferred_element_type=F32)
            rb = lax.dot_general(x, kb_, (((1,), (1,)), ((), ())), preferred_element_type=F32)
            acc = acc + w_cols[2 * p] * jnp.maximum(ra, 0.0)
            acc = acc + w_cols[2 * p + 1] * jnp.maximum(rb, 0.0)
        key = _sort_key(acc)
        for j in range(sub):
            col_pos = k0 + j * LANES + lane
            key_ref[kb * sub + j] = jnp.where(col_pos <= row_pos, key[:, j * LANES:(j + 1) * LANES], INT_MIN)
        return carry

    lax.fori_loop(0, n_kb, score_block, 0)
    n_slabs = n_kb * sub

    def count(pred):
        def body(kb, acc):
            hits = [jnp.where(pred(key_ref[kb * sub + j], kb * sub + j), 1, 0) for j in range(sub)]
            while len(hits) > 1:
                hits = [a + b for a, b in zip(hits[::2], hits[1::2])]
            return acc + hits[0]
        part = lax.fori_loop(0, n_kb, body, jnp.zeros((tq, LANES), I32))
        return jnp.sum(part.astype(F32), axis=1, keepdims=True).astype(I32)

    def search_bit(i, u):
        cand = u | (1 << (31 - i))
        thr = jnp.broadcast_to(cand ^ INT_MIN, (tq, LANES))
        total = count(lambda kv, c: kv >= thr)
        return jnp.where(total >= n_sel, cand, u)

    u = lax.fori_loop(0, 32, search_bit, jnp.zeros((tq, 1), I32))
    thr = u ^ INT_MIN
    thr_b = jnp.broadcast_to(thr, (tq, LANES))
    n_gt = count(lambda kv, c: kv > thr_b)
    n_ge = count(lambda kv, c: kv >= thr_b)
    need = n_sel - n_gt
    tie_rows = jnp.logical_and(n_ge > n_sel, need > 0)

    seq_bits = max(1, (key_ref.shape[0] * LANES).bit_length())

    def tie_search():
        need_b = need

        def bit_step(i, j):
            cand = j | (1 << (seq_bits - 1 - i))
            cand_b = jnp.broadcast_to(cand, (tq, LANES))
            total = count(lambda kv, c: jnp.logical_and(kv == thr_b, c * LANES + lane < cand_b))
            return jnp.where(total <= need_b, cand, j)

        return lax.fori_loop(0, seq_bits, bit_step, jnp.zeros((tq, 1), I32))

    any_tie = jnp.max(jnp.where(tie_rows, 1.0, 0.0)) > 0.0
    bound = lax.cond(any_tie, tie_search, lambda: jnp.full((tq, 1), 2 ** 30, I32))
    bound = jnp.where(tie_rows, bound, 2 ** 30)
    bound_b = jnp.broadcast_to(bound, (tq, LANES))

    q_all = jnp.concatenate([q_ref[:, h * HEAD_DIM:(h + 1) * HEAD_DIM] for h in range(heads)], axis=0)

    ta = min(tk, DSA_ATTN_TILE)
    sub_a = ta // LANES

    def attn_block(kb, carry):
        m, l, acc = carry
        k0 = pl.multiple_of(kb * ta, ta)
        kk = k_ref[pl.ds(k0, ta), :]
        vv = v_ref[pl.ds(k0, ta), :]
        s = lax.dot_general(q_all, kk, (((1,), (1,)), ((), ())), preferred_element_type=F32)
        bias_cols = []
        for j in range(sub_a):
            kv = key_ref[kb * sub_a + j]
            col_pos = k0 + j * LANES + lane
            take = jnp.logical_or(kv > thr_b, jnp.logical_and(kv == thr_b, col_pos < bound_b))
            take = jnp.logical_and(take, col_pos <= row_pos)
            bias_cols.append(jnp.where(take, 0.0, MASKED))
        bias = jnp.concatenate(bias_cols, axis=1)
        s = s.reshape(heads, tq, ta) + bias[None]
        m_new = jnp.maximum(m, jnp.max(s, axis=-1, keepdims=True))
        alpha = jnp.exp(m - m_new)
        p = jnp.exp((s - m_new).astype(BF16))
        l = alpha * l + jnp.sum(p.astype(F32), axis=-1, keepdims=True)
        pv = jnp.dot(p.reshape(heads * tq, ta), vv, preferred_element_type=F32)
        acc = alpha * acc + pv.reshape(heads, tq, HEAD_DIM)
        return m_new, l, acc

    init = (jnp.full((heads, tq, 1), MASKED, F32), jnp.zeros((heads, tq, 1), F32),
            jnp.zeros((heads, tq, HEAD_DIM), F32))
    m, l, acc = lax.fori_loop(0, n_kb * (tk // ta), attn_block, init)
    out = acc / l
    for h in range(heads):
        o_ref[:, h * HEAD_DIM:(h + 1) * HEAD_DIM] = out[h].astype(o_ref.dtype)


def dsa_mixer(qi, w_idx, kia, kib, q, k, proj, v_col_block, buf, seq, n_sel):
    n = q.shape[0]
    bsz = n // seq
    heads = q.shape[1] // HEAD_DIM
    tq = min(128, seq)
    tk = min(512, seq)
    nq = seq // tq
    kernel = functools.partial(_dsa_kernel, tq=tq, tk=tk, heads=heads, n_sel=n_sel)
    qrow = lambda w: pl.BlockSpec((tq, w), lambda b, i: (b * nq + i, 0))
    kv = lambda c: pl.BlockSpec((seq, LANES), lambda b, i: (b, c))
    return pl.pallas_call(
        kernel,
        grid=(bsz, nq),
        in_specs=[qrow(qi.shape[1]), qrow(LANES), kv(0), kv(0), qrow(q.shape[1]), kv(0), kv(v_col_block),
                  pl.BlockSpec(memory_space=pl.ANY)],
        out_specs=pl.BlockSpec((tq, heads * HEAD_DIM), lambda b, i: (b * nq + i, 1)),
        out_shape=jax.ShapeDtypeStruct(buf.shape, buf.dtype),
        scratch_shapes=[pltpu.VMEM((seq // LANES, tq, LANES), I32)],
        input_output_aliases={7: 0},
        compiler_params=_params("parallel", "arbitrary"), name="dsa_mixer",
    )(qi, w_idx, kia, kib, q, k, proj, buf)


SB_HEADS_PER_STEP = 4
SB_QUERY_TILE = 512
SB_KEY_TILE = 256
SB_LOGIT_SCALE = HEAD_DIM ** -0.5 * LOG2E


def _sb_kernel(q_ref, k_ref, v_ref, o_ref, *, tq, tk, group):
    qt = pl.program_id(2)
    ratio = tq // tk
    row = lax.broadcasted_iota(I32, (tk, tk), 0)
    col = lax.broadcasted_iota(I32, (tk, tk), 1)
    later = jnp.where(row > col, 1.0, 0.0).astype(BF16)
    q_pos = qt * tq + lax.broadcasted_iota(I32, (tq, tk), 0)
    k_off = lax.broadcasted_iota(I32, (tq, tk), 1)

    def block(kb, masked, carry):
        k0 = pl.multiple_of(kb * tk, tk)
        heads = range(group)
        cols = [slice(h * HEAD_DIM, (h + 1) * HEAD_DIM) for h in heads]
        z = [lax.dot_general(q_ref[:, cols[h]], k_ref[pl.ds(k0, tk), cols[h]], (((1,), (1,)), ((), ())),
                             preferred_element_type=F32).astype(BF16) for h in heads]
        log_beta = [jnp.minimum(z[h], 0.0) - jnp.log(1.0 + jnp.exp2(-jnp.abs(z[h]))) * LOG2E for h in heads]
        log_fail = [log_beta[h] - z[h] for h in heads]
        if masked:
            strict = k0 + k_off < q_pos
            log_fail = [jnp.where(strict, log_fail[h], 0.0) for h in heads]
        between = [jnp.dot(log_fail[h].astype(BF16), later, preferred_element_type=F32) + carry[h][1]
                   for h in heads]
        a = [jnp.exp2(log_beta[h] + between[h]) for h in heads]
        if masked:
            a = [jnp.where(strict, a[h], 0.0) for h in heads]
        acc = [carry[h][0] + jnp.dot(a[h].astype(BF16), v_ref[pl.ds(k0, tk), cols[h]], preferred_element_type=F32)
               for h in heads]
        run = [between[h][:, 0:1] + log_fail[h][:, 0:1] for h in heads]
        return tuple((acc[h], run[h]) for h in heads)

    carry = tuple((jnp.zeros((tq, HEAD_DIM), F32), jnp.zeros((tq, 1), F32)) for _ in range(group))
    first_full = qt * ratio
    for j in reversed(range(ratio)):
        carry = block(first_full + j, True, carry)
    carry = lax.fori_loop(0, first_full, lambda i, c: block(first_full - 1 - i, False, c), carry)
    for h in range(group):
        o_ref[:, h * HEAD_DIM:(h + 1) * HEAD_DIM] = carry[h][0].astype(o_ref.dtype)


def stick_breaking_mixer(qkv, seq, heads):
    n = qkv.shape[0]
    bsz = n // seq
    tq = min(SB_QUERY_TILE, seq)
    tk = min(SB_KEY_TILE, seq)
    nq = seq // tq
    group = SB_HEADS_PER_STEP
    hg = heads // group
    width = group * HEAD_DIM
    kernel = functools.partial(_sb_kernel, tq=tq, tk=tk, group=group)
    return pl.pallas_call(
        kernel,
        grid=(bsz, hg, nq),
        in_specs=[pl.BlockSpec((tq, width), lambda b, h, i: (b * nq + i, h)),
                  pl.BlockSpec((seq, width), lambda b, h, i: (b, hg + h)),
                  pl.BlockSpec((seq, width), lambda b, h, i: (b, 2 * hg + h))],
        out_specs=pl.BlockSpec((tq, width), lambda b, h, i: (b * nq + i, h)),
        out_shape=jax.ShapeDtypeStruct((n, heads * HEAD_DIM), BF16),
        compiler_params=_params("parallel", "parallel", "arbitrary"), name="stick_breaking",
    )(qkv, qkv, qkv)


def _router_kernel(h_ref, wr_ref, b_ref, s_ref, w_ref, c_ref, cnt_ref, *, n_experts):
    per_group = n_experts // N_GROUPS
    tm = h_ref.shape[0]
    logits = lax.dot_general(wr_ref[...], h_ref[...], (((1,), (1,)), ((), ())), preferred_element_type=F32)
    scores = jax.nn.sigmoid(logits)
    choice = (scores + b_ref[...]).reshape(N_GROUPS, per_group, tm)
    s3 = scores.reshape(N_GROUPS, per_group, tm)
    neg = -jnp.inf
    in_group = lax.broadcasted_iota(I32, choice.shape, 1)
    m1 = jnp.max(choice, axis=1, keepdims=True)
    first = jnp.min(jnp.where(choice == m1, in_group, per_group), axis=1, keepdims=True)
    m2 = jnp.max(jnp.where(in_group == first, neg, choice), axis=1, keepdims=True)
    group_score = m1 + m2
    gid = lax.broadcasted_iota(I32, group_score.shape, 0)
    group_sel = jnp.zeros(group_score.shape, jnp.bool_)
    for _ in range(TOPK_GROUPS):
        m = jnp.max(group_score, axis=0, keepdims=True)
        f = jnp.min(jnp.where(group_score == m, gid, N_GROUPS), axis=0, keepdims=True)
        hit = gid == f
        group_sel = jnp.logical_or(group_sel, hit)
        group_score = jnp.where(hit, neg, group_score)
    cand = jnp.where(group_sel, choice, neg)
    eid = lax.broadcasted_iota(I32, choice.shape, 0) * per_group + in_group
    sel = jnp.zeros(choice.shape, jnp.bool_)
    hits = []
    for _ in range(TOP_K):
        m = jnp.max(jnp.max(cand, axis=1, keepdims=True), axis=0, keepdims=True)
        f = jnp.min(jnp.min(jnp.where(cand == m, eid, n_experts), axis=1, keepdims=True), axis=0, keepdims=True)
        hit = eid == f
        hits.append((hit, f))
        sel = jnp.logical_or(sel, hit)
        cand = jnp.where(hit, neg, cand)
    top_w = jnp.where(sel, s3, 0.0)
    total = jnp.sum(jnp.sum(top_w, axis=1, keepdims=True), axis=0, keepdims=True)
    gates = top_w / total * ROUTED_SCALE

    @pl.when(pl.program_id(0) == 0)
    def _():
        cnt_ref[...] = jnp.zeros_like(cnt_ref)

    sel_f = jnp.where(sel, 1.0, 0.0).reshape(n_experts, tm)
    row = lax.broadcasted_iota(I32, (tm, tm), 0)
    col = lax.broadcasted_iota(I32, (tm, tm), 1)
    before = jnp.where(row < col, 1.0, 0.0).astype(BF16)
    prefix = jnp.dot(sel_f.astype(BF16), before, preferred_element_type=F32)
    rank_all = (prefix + cnt_ref[:, 0:1]).reshape(choice.shape)

    def pick(hit, val):
        return jnp.sum(jnp.sum(jnp.where(hit, val, 0.0), axis=1, keepdims=True), axis=0, keepdims=True)

    for k, (hit, f) in enumerate(hits):
        rank = pick(hit, rank_all).reshape(1, tm).astype(I32)
        s_ref[k:k + 1, :] = rank | lax.shift_left(f.reshape(1, tm), SLOT_RANK_BITS)
        w_ref[k:k + 1, :] = pick(hit, gates).reshape(1, tm)
    cnt_ref[...] = cnt_ref[...] + jnp.sum(sel_f, axis=1, keepdims=True)
    c_ref[...] = cnt_ref[...].astype(I32)


def moe_router(h, w_router, bias):
    n, d = h.shape
    e = w_router.shape[1]
    tm = min(512, n)
    top = lambda dt: jax.ShapeDtypeStruct((TOP_K, n), dt)
    top_spec = pl.BlockSpec((TOP_K, tm), lambda i: (0, i))
    slots, wts, counts = pl.pallas_call(
        functools.partial(_router_kernel, n_experts=e),
        grid=(n // tm,),
        in_specs=[pl.BlockSpec((tm, d), lambda i: (i, 0)),
                  pl.BlockSpec((e, d), lambda i: (0, 0)),
                  pl.BlockSpec((e, 1), lambda i: (0, 0))],
        out_specs=[top_spec, top_spec, pl.BlockSpec((e, LANES), lambda i: (0, 0))],
        out_shape=[top(I32), top(F32), jax.ShapeDtypeStruct((e, LANES), I32)],
        scratch_shapes=[pltpu.VMEM((e, LANES), F32)],
        compiler_params=_params("arbitrary"), name="moe_router",
    )(h, w_router.T.astype(BF16), bias.reshape(e, 1))
    return slots.reshape(-1), wts, counts[:, 0]


MOE_TILE = 512
DISPATCH_TOKENS = 256
COMBINE_TOKENS = 128
DMA_ISSUE_UNROLL = 8
PAD_CHUNK = 512


def _pack_pairs(x):
    w = x.shape[1] // 2
    lo = pltpu.bitcast(x[:, :w].astype(BF16).astype(F32), I32)
    hi = pltpu.bitcast(x[:, w:].astype(BF16).astype(F32), I32)
    return lax.shift_right_logical(lo, 16) | hi


def _unpack_pairs(p):
    lo = pltpu.bitcast(lax.shift_left(p, 16), F32)
    hi = pltpu.bitcast(p & jnp.int32(-65536), F32)
    return lo, hi


def _store_slabs(ref, x):
    r, n_slabs, _ = ref.shape
    flat = ref.reshape(r * n_slabs, LANES)
    for s in range(n_slabs):
        flat[pl.ds(s, r, stride=n_slabs), :] = x[:, s * LANES:(s + 1) * LANES]


def _load_slab_column(ref, s):
    r, n_slabs, _ = ref.shape
    return ref.reshape(r * n_slabs, LANES)[pl.ds(s, r, stride=n_slabs), :]


def _load_slabs(ref):
    return jnp.concatenate([_load_slab_column(ref, s) for s in range(ref.shape[1])], axis=1)


SLOT_RANK_BITS = 20


def _slot_row(code, off_ref):
    return off_ref[lax.shift_right_logical(code, SLOT_RANK_BITS)] + (code & ((1 << SLOT_RANK_BITS) - 1))


def _dispatch_kernel(slot_ref, off_ref, cnt_ref, hp_ref, xs_ref, zero_ref, sem, pad_sem, *, td, n_tok, n_rows):
    base = pl.program_id(0) * td

    @pl.when(pl.program_id(0) == 0)
    def _():
        zero_ref[...] = jnp.zeros(zero_ref.shape, zero_ref.dtype)
        n_experts = cnt_ref.shape[0]

        chunk = zero_ref.shape[0]
        bits = chunk.bit_length() - 1

        def zero_copy(row, size):
            return pltpu.make_async_copy(zero_ref.at[pl.ds(0, size)], xs_ref.at[pl.ds(row, size)], pad_sem)

        def full_chunk(c, row):
            zero_copy(row, chunk).start()
            zero_copy(row, chunk).wait()
            return row + chunk

        def pad_expert(e, carry):
            start = off_ref[e] + cnt_ref[e]
            end = jnp.where(e + 1 < n_experts, off_ref[jnp.minimum(e + 1, n_experts - 1)], n_rows)
            row = lax.fori_loop(0, (end - start) // chunk, full_chunk, start)
            rem = (end - start) % chunk
            pieces = [(rem & (1 << b), row + lax.shift_left(lax.shift_right_logical(rem, b + 1), b + 1), 1 << b)
                      for b in reversed(range(bits))]
            for go, at, size in pieces:
                @pl.when(go != 0)
                def _(at=at, size=size):
                    zero_copy(at, size).start()
            for go, at, size in pieces:
                @pl.when(go != 0)
                def _(at=at, size=size):
                    zero_copy(at, size).wait()
            return carry

        lax.fori_loop(0, n_experts, pad_expert, 0)

    for k in range(TOP_K):
        def issue(t, carry, k=k):
            row = _slot_row(slot_ref[k * n_tok + base + t], off_ref)
            pltpu.make_async_copy(hp_ref.at[t], xs_ref.at[row], sem).start()
            return carry

        lax.fori_loop(0, td, issue, 0, unroll=DMA_ISSUE_UNROLL)
    for k in range(TOP_K):
        pltpu.make_async_copy(hp_ref, xs_ref.at[pl.ds(0, td)], sem).wait()


def moe_dispatch(hp, slots, offsets, counts, n_rows):
    n, s, _ = hp.shape
    td = min(DISPATCH_TOKENS, n)
    kernel = functools.partial(_dispatch_kernel, td=td, n_tok=n, n_rows=n_rows)
    return pl.pallas_call(
        kernel,
        grid_spec=pltpu.PrefetchScalarGridSpec(
            num_scalar_prefetch=3,
            grid=(n // td,),
            in_specs=[pl.BlockSpec((td, s, LANES), lambda i, *_: (i, 0, 0))],
            out_specs=pl.BlockSpec(memory_space=pl.ANY),
            scratch_shapes=[pltpu.VMEM((PAD_CHUNK, s, LANES), I32), pltpu.SemaphoreType.DMA(()),
                            pltpu.SemaphoreType.DMA(())]),
        out_shape=jax.ShapeDtypeStruct((n_rows, s, LANES), I32),
        compiler_params=_params("arbitrary"), name="moe_dispatch",
    )(slots, offsets, counts, hp)


def _swiglu(lo, hi, wg_ref, wu_ref, wd_ref):
    w = lo.shape[1]
    g = (jnp.dot(lo, wg_ref[:w, :].astype(BF16), preferred_element_type=F32)
         + jnp.dot(hi, wg_ref[w:, :].astype(BF16), preferred_element_type=F32))
    u = (jnp.dot(lo, wu_ref[:w, :].astype(BF16), preferred_element_type=F32)
         + jnp.dot(hi, wu_ref[w:, :].astype(BF16), preferred_element_type=F32))
    act = g * jax.nn.sigmoid(g) * u
    return jnp.dot(act.astype(BF16), wd_ref[...].astype(BF16), preferred_element_type=F32)


def _grouped_kernel(te_ref, nu_ref, x_ref, wg_ref, wu_ref, wd_ref, y_ref, wg_bf, wu_bf, wd_bf):
    i = pl.program_id(0)

    @pl.when(i < nu_ref[0])
    def _():
        @pl.when(jnp.logical_or(i == 0, te_ref[i] != te_ref[jnp.maximum(i - 1, 0)]))
        def _():
            wg_bf[...] = wg_ref[...].astype(BF16)
            wu_bf[...] = wu_ref[...].astype(BF16)
            wd_bf[...] = wd_ref[...].astype(BF16)

        lo, hi = _unpack_pairs(_load_slabs(x_ref))
        _store_slabs(y_ref, _pack_pairs(_swiglu(lo.astype(BF16), hi.astype(BF16), wg_bf, wu_bf, wd_bf)))


def moe_grouped(xs, tile_expert, n_used, wg, wu, wd, layer):
    p, s, _ = xs.shape
    _, e, d, f = wg.shape
    tm = MOE_TILE
    row = lambda i, te, nu: (jnp.minimum(i, nu[0] - 1), 0, 0)
    expert = lambda i, te, nu: (layer, te[i], 0, 0)
    return pl.pallas_call(
        _grouped_kernel,
        grid_spec=pltpu.PrefetchScalarGridSpec(
            num_scalar_prefetch=2,
            grid=(p // tm,),
            in_specs=[pl.BlockSpec((tm, s, LANES), row),
                      pl.BlockSpec((None, None, d, f), expert),
                      pl.BlockSpec((None, None, d, f), expert),
                      pl.BlockSpec((None, None, f, d), expert)],
            out_specs=pl.BlockSpec((tm, s, LANES), row),
            scratch_shapes=[pltpu.VMEM((d, f), BF16), pltpu.VMEM((d, f), BF16), pltpu.VMEM((f, d), BF16)]),
        out_shape=jax.ShapeDtypeStruct((p, s, LANES), I32),
        compiler_params=_params("arbitrary"), name="moe_grouped",
    )(tile_expert, n_used, xs, wg, wu, wd)


def _shared_kernel(h_ref, wg_ref, wu_ref, wd_ref, o_ref):
    x = h_ref[...]
    w = x.shape[1] // 2
    o_ref[...] = _swiglu(x[:, :w], x[:, w:], wg_ref, wu_ref, wd_ref).astype(o_ref.dtype)


def shared_expert(h, wg, wu, wd):
    n, d = h.shape
    f = wg.shape[1]
    tm = min(512, n)
    return pl.pallas_call(
        _shared_kernel,
        grid=(n // tm,),
        in_specs=[pl.BlockSpec((tm, d), lambda i: (i, 0)),
                  pl.BlockSpec((d, f), lambda i: (0, 0)),
                  pl.BlockSpec((d, f), lambda i: (0, 0)),
                  pl.BlockSpec((f, d), lambda i: (0, 0))],
        out_specs=pl.BlockSpec((tm, d), lambda i: (i, 0)),
        out_shape=jax.ShapeDtypeStruct((n, d), BF16),
        compiler_params=_params("parallel"), name="shared_expert",
    )(h, wg, wu, wd)


def _combine_kernel(slot_ref, off_ref, ys_ref, w_ref, x_ref, sh_ref, g_ref, o_ref, rows_ref, sem, *, tc, n_tok):
    base = pl.program_id(0) * tc

    for k in range(TOP_K):
        def issue(t, carry, k=k):
            row = _slot_row(slot_ref[k * n_tok + base + t], off_ref)
            pltpu.make_async_copy(ys_ref.at[row], rows_ref.at[k, t], sem).start()
            return carry

        lax.fori_loop(0, tc, issue, 0, unroll=DMA_ISSUE_UNROLL)
    for k in range(TOP_K):
        pltpu.make_async_copy(ys_ref.at[pl.ds(0, tc)], rows_ref.at[k], sem).wait()

    half = x_ref.shape[1] // 2
    w_cols = [w_ref[:, k:k + 1] for k in range(TOP_K)]
    for s in range(rows_ref.shape[2]):
        acc_lo = jnp.zeros((tc, LANES), F32)
        acc_hi = jnp.zeros((tc, LANES), F32)
        for k in range(TOP_K):
            lo, hi = _unpack_pairs(_load_slab_column(rows_ref.at[k], s))
            acc_lo = acc_lo + w_cols[k] * lo
            acc_hi = acc_hi + w_cols[k] * hi
        for acc, c0 in ((acc_lo, s * LANES), (acc_hi, half + s * LANES)):
            cols = slice(c0, c0 + LANES)
            o_ref[:, cols] = x_ref[:, cols] + g_ref[:, cols] * (acc + sh_ref[:, cols].astype(F32))


def moe_combine(ys, slots, offsets, wts, x2, shared, gate, seq):
    n, d = x2.shape
    s = ys.shape[1]
    tc = min(COMBINE_TOKENS, seq)
    per_batch = seq // tc
    kernel = functools.partial(_combine_kernel, tc=tc, n_tok=n)
    return pl.pallas_call(
        kernel,
        grid_spec=pltpu.PrefetchScalarGridSpec(
            num_scalar_prefetch=2,
            grid=(n // tc,),
            in_specs=[pl.BlockSpec(memory_space=pl.ANY),
                      pl.BlockSpec((tc, TOP_K), lambda i, *_: (i, 0)),
                      pl.BlockSpec((tc, d), lambda i, *_: (i, 0)),
                      pl.BlockSpec((tc, d), lambda i, *_: (i, 0)),
                      pl.BlockSpec((None, 1, d), lambda i, *_: (i // per_batch, 0, 0))],
            out_specs=pl.BlockSpec((tc, d), lambda i, *_: (i, 0)),
            scratch_shapes=[pltpu.VMEM((TOP_K, tc, s, LANES), I32), pltpu.SemaphoreType.DMA(())]),
        out_shape=jax.ShapeDtypeStruct((n, d), F32),
        compiler_params=_params("arbitrary"), name="moe_combine",
    )(slots, offsets, ys, wts, x2, shared, gate)


def moe_layout(counts, n_tiles):
    e = counts.shape[0]
    tiles = (counts + MOE_TILE - 1) // MOE_TILE
    earlier = jnp.arange(e)[None, :] <= jnp.arange(e)[:, None]
    tile_end = jnp.sum(jnp.where(earlier, tiles[None, :], 0), axis=1)
    offsets = (tile_end - tiles) * MOE_TILE
    n_used = tile_end[-1]
    tile_ids = jnp.minimum(jnp.arange(n_tiles, dtype=I32), n_used - 1)
    tile_expert = jnp.sum(tile_end[None, :] <= tile_ids[:, None], axis=1)
    return offsets.astype(I32), tile_expert.astype(I32), n_used.reshape(1).astype(I32)


def _pad_cols(w, width):
    return jnp.pad(w, ((0, 0), (0, width - w.shape[1])))


def kernel(x, c, positions, ab_w_in, ab_w_out, gmlp_ln_g, gmlp_ln_b, gmlp_w_s, gmlp_b_s, sb_w_qkv, sb_w_out,
           norm_mix_g, ada_mix_w, ada_mix_b, norm_ffn_g, ada_ffn_w, ada_ffn_b, router_w, router_bias,
           expert_w_gate, expert_w_up, expert_w_down, shared_w_gate, shared_w_up, shared_w_down, final_norm_g):
    bsz, seq, d = x.shape
    depth = norm_mix_g.shape[0]
    a_width = gmlp_ln_g.shape[1]
    b_width = ab_w_out.shape[1] - a_width
    b_heads = b_width // HEAD_DIM
    c_heads = sb_w_out.shape[1] // HEAD_DIM
    n_sel = min(TOPK_MAX, seq // 4)
    main_width = 2 * a_width + b_width + 2 * HEAD_DIM
    idx_width = IDX_HEADS * IDX_DIM + LANES

    mod_mix = ada_modulation_all(c, ada_mix_w, ada_mix_b)
    mod_ffn = ada_modulation_all(c, ada_ffn_w, ada_ffn_b)
    tables = rope_tables(positions)
    x2 = x.reshape(bsz * seq, d)

    for layer in range(depth):
        j = layer // 2
        shift, scale, gate = mod_mix[layer, :, 0], mod_mix[layer, :, 1], mod_mix[layer, :, 2]
        h = norm_modulate(x2, norm_mix_g[layer], scale, shift, seq)
        if layer % 2 == 0:
            w_in = ab_w_in[j]
            proj = matmul(h, w_in, BF16, tn=640 if main_width % 640 == 0 else 128, n_out=main_width)
            proj_idx = matmul(h, _pad_cols(w_in[:, main_width:], idx_width).astype(BF16), F32, tn=idx_width)
            q, k, qi, kia, kib, w_idx = rope_prepare(proj, proj_idx, tables, 2 * a_width, b_heads)
            buf = gmlp_mixer(proj, a_width, gmlp_ln_g[j], gmlp_ln_b[j], gmlp_w_s[j], gmlp_b_s[j],
                             a_width + b_width)
            v_col_block = (2 * a_width + b_width + HEAD_DIM) // HEAD_DIM
            mixed = dsa_mixer(qi, w_idx, kia, kib, q, k, proj, v_col_block, buf, seq, n_sel)
            x2 = matmul_residual(mixed, ab_w_out[j], x2, gate, seq)
        else:
            c_width = c_heads * HEAD_DIM
            col_scale = jnp.where(jnp.arange(3 * c_width) < c_width, SB_LOGIT_SCALE, 1.0).astype(F32)
            qkv = matmul(h, sb_w_qkv[j], BF16, col_scale=col_scale)
            o = stick_breaking_mixer(qkv, seq, c_heads)
            x2 = matmul_residual(o, sb_w_out[j], x2, gate, seq)

        shift, scale, gate = mod_ffn[layer, :, 0], mod_ffn[layer, :, 1], mod_ffn[layer, :, 2]
        h, h_packed = norm_modulate(x2, norm_ffn_g[layer], scale, shift, seq, pack=True)
        slots, wts, counts = moe_router(h, router_w[layer], router_bias[layer])
        n_tiles = bsz * seq * TOP_K // MOE_TILE + router_w.shape[2]
        offsets, tile_expert, n_used = moe_layout(counts, n_tiles)
        xs = moe_dispatch(h_packed, slots, offsets, counts, n_tiles * MOE_TILE)
        ys = moe_grouped(xs, tile_expert, n_used, expert_w_gate, expert_w_up, expert_w_down, layer)
        shared = shared_expert(h, shared_w_gate[layer], shared_w_up[layer], shared_w_down[layer])
        x2 = moe_combine(ys, slots, offsets, wts.T, x2, shared, gate, seq)

    return final_norm(x2, final_norm_g).reshape(bsz, seq, d)
```

```python
import functools

import jax
import jax.numpy as jnp
from jax import lax
from jax.experimental import pallas as pl
from jax.experimental.pallas import tpu as pltpu

F32 = jnp.float32
BF16 = jnp.bfloat16
I32 = jnp.int32

EPS = 1e-6
ROPE_THETA = 10000.0
LANES = 128
HEAD_DIM = 128
A_GROUPS = 8
A_CHUNK = 128
IDX_HEADS = 16
IDX_DIM = 64
TOPK_MAX = 256
N_GROUPS = 8
TOPK_GROUPS = 4
TOP_K = 8
ROUTED_SCALE = 2.5
VMEM_LIMIT = 56 * 1024 * 1024
INT_MIN = -(2 ** 31)
MASKED = -1e30
LOG2E = 1.4426950408889634


def _params(*sem):
    return pltpu.CompilerParams(dimension_semantics=sem, vmem_limit_bytes=VMEM_LIMIT)


def _mod_kernel(c_ref, w_ref, b_ref, o_ref):
    c = c_ref[...]
    s = c * jax.nn.sigmoid(c)
    o_ref[...] = jnp.dot(s.astype(BF16), w_ref[...].astype(BF16), preferred_element_type=F32) + b_ref[...]


def ada_modulation_all(c, w, b):
    bsz, d = c.shape
    n_layers, _, n_out = w.shape
    rows = 8
    c_pad = jnp.zeros((rows, d), F32).at[:bsz].set(c)
    tn = min(512, n_out)
    out = pl.pallas_call(
        _mod_kernel,
        grid=(n_layers, n_out // tn),
        in_specs=[pl.BlockSpec((rows, d), lambda l, j: (0, 0)),
                  pl.BlockSpec((None, d, tn), lambda l, j: (l, 0, j)),
                  pl.BlockSpec((None, 1, tn), lambda l, j: (l, 0, j))],
        out_specs=pl.BlockSpec((None, rows, tn), lambda l, j: (l, 0, j)),
        out_shape=jax.ShapeDtypeStruct((n_layers, rows, n_out), F32),
        compiler_params=_params("parallel", "parallel"), name="ada_modulation",
    )(c_pad, w, b.reshape(n_layers, 1, n_out))
    return out[:, :bsz].reshape(n_layers, bsz, 3, 1, d)


def _norm_mod_kernel(x_ref, g_ref, sc_ref, sh_ref, o_ref):
    x = x_ref[...]
    y = x * lax.rsqrt(jnp.mean(x * x, axis=-1, keepdims=True) + EPS)
    o_ref[...] = ((y * g_ref[...]) * (1.0 + sc_ref[...]) + sh_ref[...]).astype(o_ref.dtype)


def _norm_mod_pack_kernel(x_ref, g_ref, sc_ref, sh_ref, o_ref, p_ref):
    x = x_ref[...]
    y = x * lax.rsqrt(jnp.mean(x * x, axis=-1, keepdims=True) + EPS)
    h = (y * g_ref[...]) * (1.0 + sc_ref[...]) + sh_ref[...]
    o_ref[...] = h.astype(o_ref.dtype)
    _store_slabs(p_ref, _pack_pairs(h))


def norm_modulate(x2, g, scale, shift, seq, pack=False):
    n, d = x2.shape
    tm = min(256, seq)
    per_batch = seq // tm
    row = lambda w: pl.BlockSpec((tm, w), lambda i: (i, 0))
    out_specs, out_shape = row(d), jax.ShapeDtypeStruct((n, d), BF16)
    if pack:
        slabs = d // 2 // LANES
        out_specs = [out_specs, pl.BlockSpec((tm, slabs, LANES), lambda i: (i, 0, 0))]
        out_shape = [out_shape, jax.ShapeDtypeStruct((n, slabs, LANES), I32)]
    return pl.pallas_call(
        _norm_mod_pack_kernel if pack else _norm_mod_kernel,
        grid=(n // tm,),
        in_specs=[row(d),
                  pl.BlockSpec((1, d), lambda i: (0, 0)),
                  pl.BlockSpec((None, 1, d), lambda i: (i // per_batch, 0, 0)),
                  pl.BlockSpec((None, 1, d), lambda i: (i // per_batch, 0, 0))],
        out_specs=out_specs,
        out_shape=out_shape,
        compiler_params=_params("parallel"), name="norm_modulate",
    )(x2, g.reshape(1, d), scale, shift)


def _final_norm_kernel(x_ref, g_ref, o_ref):
    x = x_ref[...]
    y = x * lax.rsqrt(jnp.mean(x * x, axis=-1, keepdims=True) + EPS)
    o_ref[...] = y * g_ref[...]


def final_norm(x2, g):
    n, d = x2.shape
    tm = min(256, n)
    return pl.pallas_call(
        _final_norm_kernel,
        grid=(n // tm,),
        in_specs=[pl.BlockSpec((tm, d), lambda i: (i, 0)), pl.BlockSpec((1, d), lambda i: (0, 0))],
        out_specs=pl.BlockSpec((tm, d), lambda i: (i, 0)),
        out_shape=jax.ShapeDtypeStruct((n, d), F32),
        compiler_params=_params("parallel"), name="final_norm",
    )(x2, g.reshape(1, d))


def _mm_kernel(a_ref, w_ref, o_ref):
    o_ref[...] = jnp.dot(a_ref[...], w_ref[...].astype(BF16), preferred_element_type=F32).astype(o_ref.dtype)


def _mm_colscale_kernel(a_ref, w_ref, s_ref, o_ref):
    w = (w_ref[...] * s_ref[...]).astype(BF16)
    o_ref[...] = jnp.dot(a_ref[...], w, preferred_element_type=F32).astype(o_ref.dtype)


def matmul(a, w, out_dtype, tm=1024, tn=512, n_out=None, col_scale=None):
    m, k = a.shape
    n = w.shape[1] if n_out is None else n_out
    tm, tn = min(tm, m), min(tn, n)
    in_specs = [pl.BlockSpec((tm, k), lambda i, j: (i, 0)),
                pl.BlockSpec((k, tn), lambda i, j: (0, j))]
    args = (a, w)
    if col_scale is not None:
        in_specs.append(pl.BlockSpec((1, tn), lambda i, j: (0, j)))
        args += (col_scale.reshape(1, -1),)
    return pl.pallas_call(
        _mm_kernel if col_scale is None else _mm_colscale_kernel,
        grid=(m // tm, n // tn),
        in_specs=in_specs,
        out_specs=pl.BlockSpec((tm, tn), lambda i, j: (i, j)),
        out_shape=jax.ShapeDtypeStruct((m, n), out_dtype),
        compiler_params=_params("parallel", "parallel"), name="projection",
    )(*args)


def _mm_res_kernel(a_ref, w_ref, r_ref, g_ref, o_ref):
    acc = jnp.dot(a_ref[...], w_ref[...].astype(BF16), preferred_element_type=F32)
    o_ref[...] = r_ref[...] + g_ref[...] * acc


def matmul_residual(a, w, res, gate, seq, tm=1024, tn=512):
    m, k = a.shape
    n = w.shape[1]
    tm, tn = min(tm, seq), min(tn, n)
    per_batch = seq // tm
    return pl.pallas_call(
        _mm_res_kernel,
        grid=(m // tm, n // tn),
        in_specs=[pl.BlockSpec((tm, k), lambda i, j: (i, 0)),
                  pl.BlockSpec((k, tn), lambda i, j: (0, j)),
                  pl.BlockSpec((tm, tn), lambda i, j: (i, j)),
                  pl.BlockSpec((None, 1, tn), lambda i, j: (i // per_batch, 0, j))],
        out_specs=pl.BlockSpec((tm, tn), lambda i, j: (i, j)),
        out_shape=jax.ShapeDtypeStruct((m, n), F32),
        compiler_params=_params("parallel", "parallel"), name="projection_residual",
    )(a, w, res, gate)


def _gelu(x):
    return 0.5 * x * (1.0 + lax.erf(x * (2.0 ** -0.5)))


def _gmlp_kernel(u_ref, v_ref, lng_ref, lnb_ref, ws_ref, bs_ref, o_ref, *, groups):
    u = _gelu(u_ref[...].astype(F32))
    v = _gelu(v_ref[...].astype(F32))
    mu = jnp.mean(v, axis=-1, keepdims=True)
    var = jnp.mean(jnp.square(v - mu), axis=-1, keepdims=True)
    vn = ((v - mu) * lax.rsqrt(var + EPS) * lng_ref[...] + lnb_ref[...]).astype(BF16)
    t = ws_ref.shape[1]
    gw = u.shape[1] // groups
    row = lax.broadcasted_iota(I32, (t, t), 0)
    col = lax.broadcasted_iota(I32, (t, t), 1)
    causal = col <= row
    for g in range(groups):
        w = jnp.where(causal, ws_ref[g], 0.0).astype(BF16)
        mixed = jnp.dot(w, vn[:, g * gw:(g + 1) * gw], preferred_element_type=F32) + bs_ref[:, g:g + 1]
        o_ref[:, g * gw:(g + 1) * gw] = (u[:, g * gw:(g + 1) * gw] * mixed).astype(o_ref.dtype)
    o_ref[:, u.shape[1]:] = jnp.zeros((t, o_ref.shape[1] - u.shape[1]), o_ref.dtype)


def gmlp_mixer(proj, a_width, ln_g, ln_b, w_s, b_s, out_width):
    n = proj.shape[0]
    groups, t, _ = w_s.shape
    return pl.pallas_call(
        functools.partial(_gmlp_kernel, groups=groups),
        grid=(n // t,),
        in_specs=[pl.BlockSpec((t, a_width), lambda i: (i, 0)),
                  pl.BlockSpec((t, a_width), lambda i: (i, 1)),
                  pl.BlockSpec((1, a_width), lambda i: (0, 0)),
                  pl.BlockSpec((1, a_width), lambda i: (0, 0)),
                  pl.BlockSpec((groups, t, t), lambda i: (0, 0, 0)),
                  pl.BlockSpec((t, groups), lambda i: (0, 0))],
        out_specs=pl.BlockSpec((t, out_width), lambda i: (i, 0)),
        out_shape=jax.ShapeDtypeStruct((n, out_width), BF16),
        compiler_params=_params("parallel"), name="gmlp_mixer",
    )(proj, proj, ln_g.reshape(1, a_width), ln_b.reshape(1, a_width), w_s, b_s.T)


def _rope_kernel(q_ref, k_ref, x_ref, cb_ref, sb_ref, ci_ref, si_ref, ck_ref, sk_ref,
                 qo_ref, ko_ref, qio_ref, kia_ref, kib_ref, wo_ref, *, heads, idx_pairs, q_scale, w_scale):
    cb, sb = cb_ref[...], sb_ref[...]
    half = HEAD_DIM // 2
    for h in range(heads):
        x = q_ref[:, h * HEAD_DIM:(h + 1) * HEAD_DIM].astype(F32)
        r = x * cb + pltpu.roll(x, half, 1) * sb
        qo_ref[:, h * HEAD_DIM:(h + 1) * HEAD_DIM] = (r * q_scale).astype(qo_ref.dtype)
    x = k_ref[...].astype(F32)
    ko_ref[...] = (x * cb + pltpu.roll(x, half, 1) * sb).astype(ko_ref.dtype)

    lane = lax.broadcasted_iota(I32, cb.shape, 1)
    first_half = (lane % IDX_DIM) < (IDX_DIM // 2)

    def rope_idx(x, c, s):
        rot = jnp.where(first_half, pltpu.roll(x, LANES - IDX_DIM // 2, 1), pltpu.roll(x, IDX_DIM // 2, 1))
        return x * c + rot * s

    ci, si = ci_ref[...], si_ref[...]
    for p in range(idx_pairs):
        x = x_ref[:, p * LANES:(p + 1) * LANES]
        qio_ref[:, p * LANES:(p + 1) * LANES] = rope_idx(x, ci, si).astype(qio_ref.dtype)
    tail = x_ref[:, idx_pairs * LANES:(idx_pairs + 1) * LANES]
    roped = rope_idx(tail, ck_ref[...], sk_ref[...])
    is_k = lane < IDX_DIM
    ka = jnp.where(is_k, roped, 0.0)
    kia_ref[...] = ka.astype(kia_ref.dtype)
    kib_ref[...] = pltpu.roll(ka, IDX_DIM, 1).astype(kib_ref.dtype)
    w = pltpu.roll(tail, LANES - IDX_DIM, 1)
    wo_ref[...] = jnp.where(lane < IDX_HEADS, w * w_scale, 0.0)


def rope_prepare(proj, proj_idx, tables, q_off, heads):
    n = proj.shape[0]
    tm = min(256, n)
    idx_pairs = IDX_HEADS * IDX_DIM // LANES
    qb = q_off // (heads * HEAD_DIM)
    kb = (q_off + heads * HEAD_DIM) // HEAD_DIM
    row = lambda w: pl.BlockSpec((tm, w), lambda i: (i, 0))
    kernel = functools.partial(_rope_kernel, heads=heads, idx_pairs=idx_pairs, q_scale=HEAD_DIM ** -0.5,
                               w_scale=IDX_HEADS ** -0.5 * IDX_DIM ** -0.5)
    return pl.pallas_call(
        kernel,
        grid=(n // tm,),
        in_specs=[pl.BlockSpec((tm, heads * HEAD_DIM), lambda i: (i, qb)),
                  pl.BlockSpec((tm, HEAD_DIM), lambda i: (i, kb)),
                  row(proj_idx.shape[1])] + [row(LANES)] * 6,
        out_specs=[row(heads * HEAD_DIM), row(HEAD_DIM), row(idx_pairs * LANES), row(LANES), row(LANES), row(LANES)],
        out_shape=[jax.ShapeDtypeStruct((n, heads * HEAD_DIM), BF16),
                   jax.ShapeDtypeStruct((n, HEAD_DIM), BF16),
                   jax.ShapeDtypeStruct((n, idx_pairs * LANES), BF16),
                   jax.ShapeDtypeStruct((n, LANES), BF16),
                   jax.ShapeDtypeStruct((n, LANES), BF16),
                   jax.ShapeDtypeStruct((n, LANES), F32)],
        compiler_params=_params("parallel"), name="rope_prepare",
    )(proj, proj, proj_idx, *tables)


def rope_tables(positions):
    pos = positions.reshape(-1).astype(F32)[:, None]

    def cs(dim):
        inv = ROPE_THETA ** (-jnp.arange(0, dim, 2, dtype=F32) / dim)
        ang = pos * inv
        return jnp.cos(ang), jnp.sin(ang)

    cb, sb = cs(HEAD_DIM)
    ci, si = cs(IDX_DIM)
    ones, zeros = jnp.ones_like(cb), jnp.zeros_like(cb)
    return (jnp.concatenate([cb, cb], 1), jnp.concatenate([-sb, sb], 1),
            jnp.concatenate([ci, ci, ci, ci], 1), jnp.concatenate([-si, si, -si, si], 1),
            jnp.concatenate([ci, ci, ones], 1), jnp.concatenate([-si, si, zeros], 1))


DSA_ATTN_TILE = 512


def _sort_key(x):
    x = jnp.where(x == 0.0, 0.0, x)
    bits = pltpu.bitcast(x, I32)
    return jnp.where(bits < 0, bits ^ 0x7FFFFFFF, bits)


def _dsa_kernel(qi_ref, w_ref, kia_ref, kib_ref, q_ref, k_ref, v_ref, buf_ref, o_ref, key_ref,
                *, tq, tk, heads, n_sel):
    del buf_ref
    qt = pl.program_id(1)
    q_lo = qt * tq
    n_kb = (q_lo + tq + tk - 1) // tk
    sub = tk // LANES
    row_pos = q_lo + lax.broadcasted_iota(I32, (tq, LANES), 0)
    lane = lax.broadcasted_iota(I32, (tq, LANES), 1)

    w = w_ref[...]
    w_cols = [jnp.broadcast_to(w[:, h:h + 1], (tq, tk)) for h in range(IDX_HEADS)]

    def score_block(kb, carry):
        k0 = pl.multiple_of(kb * tk, tk)
        ka = kia_ref[pl.ds(k0, tk), :]
        kb_ = kib_ref[pl.ds(k0, tk), :]
        acc = jnp.zeros((tq, tk), F32)
        for p in range(IDX_HEADS // 2):
            x = qi_ref[:, p * LANES:(p + 1) * LANES]
            ra = lax.dot_general(x, ka, (((1,), (1,)), ((), ())), preferred_element_type=F32)
            rb = lax.dot_general(x, kb_, (((1,), (1,)), ((), ())), preferred_element_type=F32)
            acc = acc + w_cols[2 * p] * jnp.maximum(ra, 0.0)
            acc = acc + w_cols[2 * p + 1] * jnp.maximum(rb, 0.0)
        key = _sort_key(acc)
        for j in range(sub):
            col_pos = k0 + j * LANES + lane
            key_ref[kb * sub + j] = jnp.where(col_pos <= row_pos, key[:, j * LANES:(j + 1) * LANES], INT_MIN)
        return carry

    lax.fori_loop(0, n_kb, score_block, 0)
    n_slabs = n_kb * sub

    def count(pred):
        def body(kb, acc):
            hits = [jnp.where(pred(key_ref[kb * sub + j], kb * sub + j), 1, 0) for j in range(sub)]
            while len(hits) > 1:
                hits = [a + b for a, b in zip(hits[::2], hits[1::2])]
            return acc + hits[0]
        part = lax.fori_loop(0, n_kb, body, jnp.zeros((tq, LANES), I32))
        return jnp.sum(part.astype(F32), axis=1, keepdims=True).astype(I32)

    def search_bit(i, u):
        cand = u | (1 << (31 - i))
        thr = jnp.broadcast_to(cand ^ INT_MIN, (tq, LANES))
        total = count(lambda kv, c: kv >= thr)
        return jnp.where(total >= n_sel, cand, u)

    u = lax.fori_loop(0, 32, search_bit, jnp.zeros((tq, 1), I32))
    thr = u ^ INT_MIN
    thr_b = jnp.broadcast_to(thr, (tq, LANES))
    n_gt = count(lambda kv, c: kv > thr_b)
    n_ge = count(lambda kv, c: kv >= thr_b)
    need = n_sel - n_gt
    tie_rows = jnp.logical_and(n_ge > n_sel, need > 0)

    seq_bits = max(1, (key_ref.shape[0] * LANES).bit_length())

    def tie_search():
        need_b = need

        def bit_step(i, j):
            cand = j | (1 << (seq_bits - 1 - i))
            cand_b = jnp.broadcast_to(cand, (tq, LANES))
            total = count(lambda kv, c: jnp.logical_and(kv == thr_b, c * LANES + lane < cand_b))
            return jnp.where(total <= need_b, cand, j)

        return lax.fori_loop(0, seq_bits, bit_step, jnp.zeros((tq, 1), I32))

    any_tie = jnp.max(jnp.where(tie_rows, 1.0, 0.0)) > 0.0
    bound = lax.cond(any_tie, tie_search, lambda: jnp.full((tq, 1), 2 ** 30, I32))
    bound = jnp.where(tie_rows, bound, 2 ** 30)
    bound_b = jnp.broadcast_to(bound, (tq, LANES))

    q_all = jnp.concatenate([q_ref[:, h * HEAD_DIM:(h + 1) * HEAD_DIM] for h in range(heads)], axis=0)

    ta = min(tk, DSA_ATTN_TILE)
    sub_a = ta // LANES

    def attn_block(kb, carry):
        m, l, acc = carry
        k0 = pl.multiple_of(kb * ta, ta)
        kk = k_ref[pl.ds(k0, ta), :]
        vv = v_ref[pl.ds(k0, ta), :]
        s = lax.dot_general(q_all, kk, (((1,), (1,)), ((), ())), preferred_element_type=F32)
        bias_cols = []
        for j in range(sub_a):
            kv = key_ref[kb * sub_a + j]
            col_pos = k0 + j * LANES + lane
            take = jnp.logical_or(kv > thr_b, jnp.logical_and(kv == thr_b, col_pos < bound_b))
            take = jnp.logical_and(take, col_pos <= row_pos)
            bias_cols.append(jnp.where(take, 0.0, MASKED))
        bias = jnp.concatenate(bias_cols, axis=1)
        s = s.reshape(heads, tq, ta) + bias[None]
        m_new = jnp.maximum(m, jnp.max(s, axis=-1, keepdims=True))
        alpha = jnp.exp(m - m_new)
        p = jnp.exp((s - m_new).astype(BF16))
        l = alpha * l + jnp.sum(p.astype(F32), axis=-1, keepdims=True)
        pv = jnp.dot(p.reshape(heads * tq, ta), vv, preferred_element_type=F32)
        acc = alpha * acc + pv.reshape(heads, tq, HEAD_DIM)
        return m_new, l, acc

    init = (jnp.full((heads, tq, 1), MASKED, F32), jnp.zeros((heads, tq, 1), F32),
            jnp.zeros((heads, tq, HEAD_DIM), F32))
    m, l, acc = lax.fori_loop(0, n_kb * (tk // ta), attn_block, init)
    out = acc / l
    for h in range(heads):
        o_ref[:, h * HEAD_DIM:(h + 1) * HEAD_DIM] = out[h].astype(o_ref.dtype)


def dsa_mixer(qi, w_idx, kia, kib, q, k, proj, v_col_block, buf, seq, n_sel):
    n = q.shape[0]
    bsz = n // seq
    heads = q.shape[1] // HEAD_DIM
    tq = min(128, seq)
    tk = min(512, seq)
    nq = seq // tq
    kernel = functools.partial(_dsa_kernel, tq=tq, tk=tk, heads=heads, n_sel=n_sel)
    qrow = lambda w: pl.BlockSpec((tq, w), lambda b, i: (b * nq + i, 0))
    kv = lambda c: pl.BlockSpec((seq, LANES), lambda b, i: (b, c))
    return pl.pallas_call(
        kernel,
        grid=(bsz, nq),
        in_specs=[qrow(qi.shape[1]), qrow(LANES), kv(0), kv(0), qrow(q.shape[1]), kv(0), kv(v_col_block),
                  pl.BlockSpec(memory_space=pl.ANY)],
        out_specs=pl.BlockSpec((tq, heads * HEAD_DIM), lambda b, i: (b * nq + i, 1)),
        out_shape=jax.ShapeDtypeStruct(buf.shape, buf.dtype),
        scratch_shapes=[pltpu.VMEM((seq // LANES, tq, LANES), I32)],
        input_output_aliases={7: 0},
        compiler_params=_params("parallel", "arbitrary"), name="dsa_mixer",
    )(qi, w_idx, kia, kib, q, k, proj, buf)


SB_HEADS_PER_STEP = 4
SB_QUERY_TILE = 512
SB_KEY_TILE = 256
SB_LOGIT_SCALE = HEAD_DIM ** -0.5 * LOG2E


def _sb_kernel(q_ref, k_ref, v_ref, o_ref, *, tq, tk, group):
    qt = pl.program_id(2)
    ratio = tq // tk
    row = lax.broadcasted_iota(I32, (tk, tk), 0)
    col = lax.broadcasted_iota(I32, (tk, tk), 1)
    later = jnp.where(row > col, 1.0, 0.0).astype(BF16)
    q_pos = qt * tq + lax.broadcasted_iota(I32, (tq, tk), 0)
    k_off = lax.broadcasted_iota(I32, (tq, tk), 1)

    def block(kb, masked, carry):
        k0 = pl.multiple_of(kb * tk, tk)
        heads = range(group)
        cols = [slice(h * HEAD_DIM, (h + 1) * HEAD_DIM) for h in heads]
        z = [lax.dot_general(q_ref[:, cols[h]], k_ref[pl.ds(k0, tk), cols[h]], (((1,), (1,)), ((), ())),
                             preferred_element_type=F32).astype(BF16) for h in heads]
        log_beta = [jnp.minimum(z[h], 0.0) - jnp.log(1.0 + jnp.exp2(-jnp.abs(z[h]))) * LOG2E for h in heads]
        log_fail = [log_beta[h] - z[h] for h in heads]
        if masked:
            strict = k0 + k_off < q_pos
            log_fail = [jnp.where(strict, log_fail[h], 0.0) for h in heads]
        between = [jnp.dot(log_fail[h].astype(BF16), later, preferred_element_type=F32) + carry[h][1]
                   for h in heads]
        a = [jnp.exp2(log_beta[h] + between[h]) for h in heads]
        if masked:
            a = [jnp.where(strict, a[h], 0.0) for h in heads]
        acc = [carry[h][0] + jnp.dot(a[h].astype(BF16), v_ref[pl.ds(k0, tk), cols[h]], preferred_element_type=F32)
               for h in heads]
        run = [between[h][:, 0:1] + log_fail[h][:, 0:1] for h in heads]
        return tuple((acc[h], run[h]) for h in heads)

    carry = tuple((jnp.zeros((tq, HEAD_DIM), F32), jnp.zeros((tq, 1), F32)) for _ in range(group))
    first_full = qt * ratio
    for j in reversed(range(ratio)):
        carry = block(first_full + j, True, carry)
    carry = lax.fori_loop(0, first_full, lambda i, c: block(first_full - 1 - i, False, c), carry)
    for h in range(group):
        o_ref[:, h * HEAD_DIM:(h + 1) * HEAD_DIM] = carry[h][0].astype(o_ref.dtype)


def stick_breaking_mixer(qkv, seq, heads):
    n = qkv.shape[0]
    bsz = n // seq
    tq = min(SB_QUERY_TILE, seq)
    tk = min(SB_KEY_TILE, seq)
    nq = seq // tq
    group = SB_HEADS_PER_STEP
    hg = heads // group
    width = group * HEAD_DIM
    kernel = functools.partial(_sb_kernel, tq=tq, tk=tk, group=group)
    return pl.pallas_call(
        kernel,
        grid=(bsz, hg, nq),
        in_specs=[pl.BlockSpec((tq, width), lambda b, h, i: (b * nq + i, h)),
                  pl.BlockSpec((seq, width), lambda b, h, i: (b, hg + h)),
                  pl.BlockSpec((seq, width), lambda b, h, i: (b, 2 * hg + h))],
        out_specs=pl.BlockSpec((tq, width), lambda b, h, i: (b * nq + i, h)),
        out_shape=jax.ShapeDtypeStruct((n, heads * HEAD_DIM), BF16),
        compiler_params=_params("parallel", "parallel", "arbitrary"), name="stick_breaking",
    )(qkv, qkv, qkv)


def _router_kernel(h_ref, wr_ref, b_ref, s_ref, w_ref, c_ref, cnt_ref, *, n_experts):
    per_group = n_experts // N_GROUPS
    tm = h_ref.shape[0]
    logits = lax.dot_general(wr_ref[...], h_ref[...], (((1,), (1,)), ((), ())), preferred_element_type=F32)
    scores = jax.nn.sigmoid(logits)
    choice = (scores + b_ref[...]).reshape(N_GROUPS, per_group, tm)
    s3 = scores.reshape(N_GROUPS, per_group, tm)
    neg = -jnp.inf
    in_group = lax.broadcasted_iota(I32, choice.shape, 1)
    m1 = jnp.max(choice, axis=1, keepdims=True)
    first = jnp.min(jnp.where(choice == m1, in_group, per_group), axis=1, keepdims=True)
    m2 = jnp.max(jnp.where(in_group == first, neg, choice), axis=1, keepdims=True)
    group_score = m1 + m2
    gid = lax.broadcasted_iota(I32, group_score.shape, 0)
    group_sel = jnp.zeros(group_score.shape, jnp.bool_)
    for _ in range(TOPK_GROUPS):
        m = jnp.max(group_score, axis=0, keepdims=True)
        f = jnp.min(jnp.where(group_score == m, gid, N_GROUPS), axis=0, keepdims=True)
        hit = gid == f
        group_sel = jnp.logical_or(group_sel, hit)
        group_score = jnp.where(hit, neg, group_score)
    cand = jnp.where(group_sel, choice, neg)
    eid = lax.broadcasted_iota(I32, choice.shape, 0) * per_group + in_group
    sel = jnp.zeros(choice.shape, jnp.bool_)
    hits = []
    for _ in range(TOP_K):
        m = jnp.max(jnp.max(cand, axis=1, keepdims=True), axis=0, keepdims=True)
        f = jnp.min(jnp.min(jnp.where(cand == m, eid, n_experts), axis=1, keepdims=True), axis=0, keepdims=True)
        hit = eid == f
        hits.append((hit, f))
        sel = jnp.logical_or(sel, hit)
        cand = jnp.where(hit, neg, cand)
    top_w = jnp.where(sel, s3, 0.0)
    total = jnp.sum(jnp.sum(top_w, axis=1, keepdims=True), axis=0, keepdims=True)
    gates = top_w / total * ROUTED_SCALE

    @pl.when(pl.program_id(0) == 0)
    def _():
        cnt_ref[...] = jnp.zeros_like(cnt_ref)

    sel_f = jnp.where(sel, 1.0, 0.0).reshape(n_experts, tm)
    row = lax.broadcasted_iota(I32, (tm, tm), 0)
    col = lax.broadcasted_iota(I32, (tm, tm), 1)
    before = jnp.where(row < col, 1.0, 0.0).astype(BF16)
    prefix = jnp.dot(sel_f.astype(BF16), before, preferred_element_type=F32)
    rank_all = (prefix + cnt_ref[:, 0:1]).reshape(choice.shape)

    def pick(hit, val):
        return jnp.sum(jnp.sum(jnp.where(hit, val, 0.0), axis=1, keepdims=True), axis=0, keepdims=True)

    for k, (hit, f) in enumerate(hits):
        rank = pick(hit, rank_all).reshape(1, tm).astype(I32)
        s_ref[k:k + 1, :] = rank | lax.shift_left(f.reshape(1, tm), SLOT_RANK_BITS)
        w_ref[k:k + 1, :] = pick(hit, gates).reshape(1, tm)
    cnt_ref[...] = cnt_ref[...] + jnp.sum(sel_f, axis=1, keepdims=True)
    c_ref[...] = cnt_ref[...].astype(I32)


def moe_router(h, w_router, bias):
    n, d = h.shape
    e = w_router.shape[1]
    tm = min(512, n)
    top = lambda dt: jax.ShapeDtypeStruct((TOP_K, n), dt)
    top_spec = pl.BlockSpec((TOP_K, tm), lambda i: (0, i))
    slots, wts, counts = pl.pallas_call(
        functools.partial(_router_kernel, n_experts=e),
        grid=(n // tm,),
        in_specs=[pl.BlockSpec((tm, d), lambda i: (i, 0)),
                  pl.BlockSpec((e, d), lambda i: (0, 0)),
                  pl.BlockSpec((e, 1), lambda i: (0, 0))],
        out_specs=[top_spec, top_spec, pl.BlockSpec((e, LANES), lambda i: (0, 0))],
        out_shape=[top(I32), top(F32), jax.ShapeDtypeStruct((e, LANES), I32)],
        scratch_shapes=[pltpu.VMEM((e, LANES), F32)],
        compiler_params=_params("arbitrary"), name="moe_router",
    )(h, w_router.T.astype(BF16), bias.reshape(e, 1))
    return slots.reshape(-1), wts, counts[:, 0]


MOE_TILE = 512
DISPATCH_TOKENS = 256
COMBINE_TOKENS = 128
DMA_ISSUE_UNROLL = 16
PAD_CHUNK = 512


def _pack_pairs(x):
    w = x.shape[1] // 2
    lo = pltpu.bitcast(x[:, :w].astype(BF16).astype(F32), I32)
    hi = pltpu.bitcast(x[:, w:].astype(BF16).astype(F32), I32)
    return lax.shift_right_logical(lo, 16) | hi


def _unpack_pairs(p):
    lo = pltpu.bitcast(lax.shift_left(p, 16), F32)
    hi = pltpu.bitcast(p & jnp.int32(-65536), F32)
    return lo, hi


def _store_slabs(ref, x):
    r, n_slabs, _ = ref.shape
    flat = ref.reshape(r * n_slabs, LANES)
    for s in range(n_slabs):
        flat[pl.ds(s, r, stride=n_slabs), :] = x[:, s * LANES:(s + 1) * LANES]


def _load_slab_column(ref, s):
    r, n_slabs, _ = ref.shape
    return ref.reshape(r * n_slabs, LANES)[pl.ds(s, r, stride=n_slabs), :]


def _load_slabs(ref):
    return jnp.concatenate([_load_slab_column(ref, s) for s in range(ref.shape[1])], axis=1)


SLOT_RANK_BITS = 20


def _dispatch_kernel(row_ref, off_ref, cnt_ref, hp_ref, xs_ref, zero_ref, sem, pad_sem, *, td, n_tok, n_rows):
    base = pl.program_id(0) * td

    @pl.when(pl.program_id(0) == 0)
    def _():
        zero_ref[...] = jnp.zeros(zero_ref.shape, zero_ref.dtype)
        n_experts = cnt_ref.shape[0]

        chunk = zero_ref.shape[0]
        bits = chunk.bit_length() - 1

        def zero_copy(row, size):
            return pltpu.make_async_copy(zero_ref.at[pl.ds(0, size)], xs_ref.at[pl.ds(row, size)], pad_sem)

        def full_chunk(c, row):
            zero_copy(row, chunk).start()
            zero_copy(row, chunk).wait()
            return row + chunk

        def pad_expert(e, carry):
            start = off_ref[e] + cnt_ref[e]
            end = jnp.where(e + 1 < n_experts, off_ref[jnp.minimum(e + 1, n_experts - 1)], n_rows)
            row = lax.fori_loop(0, (end - start) // chunk, full_chunk, start)
            rem = (end - start) % chunk
            pieces = [(rem & (1 << b), row + lax.shift_left(lax.shift_right_logical(rem, b + 1), b + 1), 1 << b)
                      for b in reversed(range(bits))]
            for go, at, size in pieces:
                @pl.when(go != 0)
                def _(at=at, size=size):
                    zero_copy(at, size).start()
            for go, at, size in pieces:
                @pl.when(go != 0)
                def _(at=at, size=size):
                    zero_copy(at, size).wait()
            return carry

        lax.fori_loop(0, n_experts, pad_expert, 0)

    for k in range(TOP_K):
        def issue(t, carry, k=k):
            pltpu.make_async_copy(hp_ref.at[t], xs_ref.at[row_ref[k * n_tok + base + t]], sem).start()
            return carry

        lax.fori_loop(0, td, issue, 0, unroll=DMA_ISSUE_UNROLL)
    for k in range(TOP_K):
        pltpu.make_async_copy(hp_ref, xs_ref.at[pl.ds(0, td)], sem).wait()


def moe_dispatch(hp, slots, offsets, counts, n_rows):
    n, s, _ = hp.shape
    td = min(DISPATCH_TOKENS, n)
    kernel = functools.partial(_dispatch_kernel, td=td, n_tok=n, n_rows=n_rows)
    return pl.pallas_call(
        kernel,
        grid_spec=pltpu.PrefetchScalarGridSpec(
            num_scalar_prefetch=3,
            grid=(n // td,),
            in_specs=[pl.BlockSpec((td, s, LANES), lambda i, *_: (i, 0, 0))],
            out_specs=pl.BlockSpec(memory_space=pl.ANY),
            scratch_shapes=[pltpu.VMEM((PAD_CHUNK, s, LANES), I32), pltpu.SemaphoreType.DMA(()),
                            pltpu.SemaphoreType.DMA(())]),
        out_shape=jax.ShapeDtypeStruct((n_rows, s, LANES), I32),
        compiler_params=_params("arbitrary"), name="moe_dispatch",
    )(slots, offsets, counts, hp)


def _swiglu(lo, hi, wg_ref, wu_ref, wd_ref):
    w = lo.shape[1]
    g = (jnp.dot(lo, wg_ref[:w, :].astype(BF16), preferred_element_type=F32)
         + jnp.dot(hi, wg_ref[w:, :].astype(BF16), preferred_element_type=F32))
    u = (jnp.dot(lo, wu_ref[:w, :].astype(BF16), preferred_element_type=F32)
         + jnp.dot(hi, wu_ref[w:, :].astype(BF16), preferred_element_type=F32))
    act = g * jax.nn.sigmoid(g) * u
    return jnp.dot(act.astype(BF16), wd_ref[...].astype(BF16), preferred_element_type=F32)


def _grouped_kernel(te_ref, nu_ref, x_ref, wg_ref, wu_ref, wd_ref, y_ref, wg_bf, wu_bf, wd_bf):
    i = pl.program_id(0)

    @pl.when(i < nu_ref[0])
    def _():
        @pl.when(jnp.logical_or(i == 0, te_ref[i] != te_ref[jnp.maximum(i - 1, 0)]))
        def _():
            wg_bf[...] = wg_ref[...].astype(BF16)
            wu_bf[...] = wu_ref[...].astype(BF16)
            wd_bf[...] = wd_ref[...].astype(BF16)

        lo, hi = _unpack_pairs(_load_slabs(x_ref))
        _store_slabs(y_ref, _pack_pairs(_swiglu(lo.astype(BF16), hi.astype(BF16), wg_bf, wu_bf, wd_bf)))


def moe_grouped(xs, tile_expert, n_used, wg, wu, wd, layer):
    p, s, _ = xs.shape
    _, e, d, f = wg.shape
    tm = MOE_TILE
    row = lambda i, te, nu: (jnp.minimum(i, nu[0] - 1), 0, 0)
    expert = lambda i, te, nu: (layer, te[i], 0, 0)
    return pl.pallas_call(
        _grouped_kernel,
        grid_spec=pltpu.PrefetchScalarGridSpec(
            num_scalar_prefetch=2,
            grid=(p // tm,),
            in_specs=[pl.BlockSpec((tm, s, LANES), row),
                      pl.BlockSpec((None, None, d, f), expert),
                      pl.BlockSpec((None, None, d, f), expert),
                      pl.BlockSpec((None, None, f, d), expert)],
            out_specs=pl.BlockSpec((tm, s, LANES), row),
            scratch_shapes=[pltpu.VMEM((d, f), BF16), pltpu.VMEM((d, f), BF16), pltpu.VMEM((f, d), BF16)]),
        out_shape=jax.ShapeDtypeStruct((p, s, LANES), I32),
        compiler_params=_params("arbitrary"), name="moe_grouped",
    )(tile_expert, n_used, xs, wg, wu, wd)


def _shared_kernel(h_ref, wg_ref, wu_ref, wd_ref, o_ref):
    x = h_ref[...]
    w = x.shape[1] // 2
    o_ref[...] = _swiglu(x[:, :w], x[:, w:], wg_ref, wu_ref, wd_ref).astype(o_ref.dtype)


def shared_expert(h, wg, wu, wd):
    n, d = h.shape
    f = wg.shape[1]
    tm = min(512, n)
    return pl.pallas_call(
        _shared_kernel,
        grid=(n // tm,),
        in_specs=[pl.BlockSpec((tm, d), lambda i: (i, 0)),
                  pl.BlockSpec((d, f), lambda i: (0, 0)),
                  pl.BlockSpec((d, f), lambda i: (0, 0)),
                  pl.BlockSpec((f, d), lambda i: (0, 0))],
        out_specs=pl.BlockSpec((tm, d), lambda i: (i, 0)),
        out_shape=jax.ShapeDtypeStruct((n, d), BF16),
        compiler_params=_params("parallel"), name="shared_expert",
    )(h, wg, wu, wd)


def _combine_kernel(row_ref, ys_ref, w_ref, x_ref, sh_ref, g_ref, o_ref, rows_ref, sem, *, tc, n_tok):
    base = pl.program_id(0) * tc

    for k in range(TOP_K):
        def issue(t, carry, k=k):
            pltpu.make_async_copy(ys_ref.at[row_ref[k * n_tok + base + t]], rows_ref.at[k, t], sem).start()
            return carry

        lax.fori_loop(0, tc, issue, 0, unroll=DMA_ISSUE_UNROLL)
    for k in range(TOP_K):
        pltpu.make_async_copy(ys_ref.at[pl.ds(0, tc)], rows_ref.at[k], sem).wait()

    half = x_ref.shape[1] // 2
    w_cols = [w_ref[:, k:k + 1] for k in range(TOP_K)]
    for s in range(rows_ref.shape[2]):
        acc_lo = jnp.zeros((tc, LANES), F32)
        acc_hi = jnp.zeros((tc, LANES), F32)
        for k in range(TOP_K):
            lo, hi = _unpack_pairs(_load_slab_column(rows_ref.at[k], s))
            acc_lo = acc_lo + w_cols[k] * lo
            acc_hi = acc_hi + w_cols[k] * hi
        for acc, c0 in ((acc_lo, s * LANES), (acc_hi, half + s * LANES)):
            cols = slice(c0, c0 + LANES)
            o_ref[:, cols] = x_ref[:, cols] + g_ref[:, cols] * (acc + sh_ref[:, cols].astype(F32))


def moe_combine(ys, slots, wts, x2, shared, gate, seq):
    n, d = x2.shape
    s = ys.shape[1]
    tc = min(COMBINE_TOKENS, seq)
    per_batch = seq // tc
    kernel = functools.partial(_combine_kernel, tc=tc, n_tok=n)
    return pl.pallas_call(
        kernel,
        grid_spec=pltpu.PrefetchScalarGridSpec(
            num_scalar_prefetch=1,
            grid=(n // tc,),
            in_specs=[pl.BlockSpec(memory_space=pl.ANY),
                      pl.BlockSpec((tc, TOP_K), lambda i, *_: (i, 0)),
                      pl.BlockSpec((tc, d), lambda i, *_: (i, 0)),
                      pl.BlockSpec((tc, d), lambda i, *_: (i, 0)),
                      pl.BlockSpec((None, 1, d), lambda i, *_: (i // per_batch, 0, 0))],
            out_specs=pl.BlockSpec((tc, d), lambda i, *_: (i, 0)),
            scratch_shapes=[pltpu.VMEM((TOP_K, tc, s, LANES), I32), pltpu.SemaphoreType.DMA(())]),
        out_shape=jax.ShapeDtypeStruct((n, d), F32),
        compiler_params=_params("arbitrary"), name="moe_combine",
    )(slots, ys, wts, x2, shared, gate)


def moe_layout(slots, counts, n_tiles):
    e = counts.shape[0]
    tiles = (counts + MOE_TILE - 1) // MOE_TILE
    earlier = jnp.arange(e)[None, :] <= jnp.arange(e)[:, None]
    tile_end = jnp.sum(jnp.where(earlier, tiles[None, :], 0), axis=1)
    offsets = (tile_end - tiles) * MOE_TILE
    n_used = tile_end[-1]
    tile_ids = jnp.minimum(jnp.arange(n_tiles, dtype=I32), n_used - 1)
    tile_expert = jnp.sum(tile_end[None, :] <= tile_ids[:, None], axis=1)
    expert = lax.shift_right_logical(slots, SLOT_RANK_BITS)
    rows = (jnp.sum(jnp.where(expert[:, None] == jnp.arange(e)[None, :], offsets[None, :], 0), axis=1)
            + (slots & ((1 << SLOT_RANK_BITS) - 1)))
    return rows.astype(I32), offsets.astype(I32), tile_expert.astype(I32), n_used.reshape(1).astype(I32)


def _pad_cols(w, width):
    return jnp.pad(w, ((0, 0), (0, width - w.shape[1])))


def kernel(x, c, positions, ab_w_in, ab_w_out, gmlp_ln_g, gmlp_ln_b, gmlp_w_s, gmlp_b_s, sb_w_qkv, sb_w_out,
           norm_mix_g, ada_mix_w, ada_mix_b, norm_ffn_g, ada_ffn_w, ada_ffn_b, router_w, router_bias,
           expert_w_gate, expert_w_up, expert_w_down, shared_w_gate, shared_w_up, shared_w_down, final_norm_g):
    bsz, seq, d = x.shape
    depth = norm_mix_g.shape[0]
    a_width = gmlp_ln_g.shape[1]
    b_width = ab_w_out.shape[1] - a_width
    b_heads = b_width // HEAD_DIM
    c_heads = sb_w_out.shape[1] // HEAD_DIM
    n_sel = min(TOPK_MAX, seq // 4)
    main_width = 2 * a_width + b_width + 2 * HEAD_DIM
    idx_width = IDX_HEADS * IDX_DIM + LANES

    mod_mix = ada_modulation_all(c, ada_mix_w, ada_mix_b)
    mod_ffn = ada_modulation_all(c, ada_ffn_w, ada_ffn_b)
    tables = rope_tables(positions)
    x2 = x.reshape(bsz * seq, d)

    for layer in range(depth):
        j = layer // 2
        shift, scale, gate = mod_mix[layer, :, 0], mod_mix[layer, :, 1], mod_mix[layer, :, 2]
        h = norm_modulate(x2, norm_mix_g[layer], scale, shift, seq)
        if layer % 2 == 0:
            w_in = ab_w_in[j]
            proj = matmul(h, w_in, BF16, tn=640 if main_width % 640 == 0 else 128, n_out=main_width)
            proj_idx = matmul(h, _pad_cols(w_in[:, main_width:], idx_width).astype(BF16), F32, tn=idx_width)
            q, k, qi, kia, kib, w_idx = rope_prepare(proj, proj_idx, tables, 2 * a_width, b_heads)
            buf = gmlp_mixer(proj, a_width, gmlp_ln_g[j], gmlp_ln_b[j], gmlp_w_s[j], gmlp_b_s[j],
                             a_width + b_width)
            v_col_block = (2 * a_width + b_width + HEAD_DIM) // HEAD_DIM
            mixed = dsa_mixer(qi, w_idx, kia, kib, q, k, proj, v_col_block, buf, seq, n_sel)
            x2 = matmul_residual(mixed, ab_w_out[j], x2, gate, seq)
        else:
            c_width = c_heads * HEAD_DIM
            col_scale = jnp.where(jnp.arange(3 * c_width) < c_width, SB_LOGIT_SCALE, 1.0).astype(F32)
            qkv = matmul(h, sb_w_qkv[j], BF16, col_scale=col_scale)
            o = stick_breaking_mixer(qkv, seq, c_heads)
            x2 = matmul_residual(o, sb_w_out[j], x2, gate, seq)

        shift, scale, gate = mod_ffn[layer, :, 0], mod_ffn[layer, :, 1], mod_ffn[layer, :, 2]
        h, h_packed = norm_modulate(x2, norm_ffn_g[layer], scale, shift, seq, pack=True)
        slots, wts, counts = moe_router(h, router_w[layer], router_bias[layer])
        n_tiles = bsz * seq * TOP_K // MOE_TILE + router_w.shape[2]
        rows, offsets, tile_expert, n_used = moe_layout(slots, counts, n_tiles)
        xs = moe_dispatch(h_packed, rows, offsets, counts, n_tiles * MOE_TILE)
        ys = moe_grouped(xs, tile_expert, n_used, expert_w_gate, expert_w_up, expert_w_down, layer)
        shared = shared_expert(h, shared_w_gate[layer], shared_w_up[layer], shared_w_down[layer])
        x2 = moe_combine(ys, rows, wts.T, x2, shared, gate, seq)

    return final_norm(x2, final_norm_g).reshape(bsz, seq, d)
```

```python
import functools

import jax
import jax.numpy as jnp
from jax import lax
from jax.experimental import pallas as pl
from jax.experimental.pallas import tpu as pltpu

F32 = jnp.float32
BF16 = jnp.bfloat16
I32 = jnp.int32

EPS = 1e-6
ROPE_THETA = 10000.0
LANES = 128
HEAD_DIM = 128
A_GROUPS = 8
A_CHUNK = 128
IDX_HEADS = 16
IDX_DIM = 64
TOPK_MAX = 256
N_GROUPS = 8
TOPK_GROUPS = 4
TOP_K = 8
ROUTED_SCALE = 2.5
VMEM_LIMIT = 56 * 1024 * 1024
INT_MIN = -(2 ** 31)
MASKED = -1e30
LOG2E = 1.4426950408889634


def _params(*sem):
    return pltpu.CompilerParams(dimension_semantics=sem, vmem_limit_bytes=VMEM_LIMIT)


def _mod_kernel(c_ref, w_ref, b_ref, o_ref):
    c = c_ref[...]
    s = c * jax.nn.sigmoid(c)
    o_ref[...] = jnp.dot(s.astype(BF16), w_ref[...].astype(BF16), preferred_element_type=F32) + b_ref[...]


def ada_modulation_all(c, w, b):
    bsz, d = c.shape
    n_layers, _, n_out = w.shape
    rows = 8
    c_pad = jnp.zeros((rows, d), F32).at[:bsz].set(c)
    tn = min(512, n_out)
    out = pl.pallas_call(
        _mod_kernel,
        grid=(n_layers, n_out // tn),
        in_specs=[pl.BlockSpec((rows, d), lambda l, j: (0, 0)),
                  pl.BlockSpec((None, d, tn), lambda l, j: (l, 0, j)),
                  pl.BlockSpec((None, 1, tn), lambda l, j: (l, 0, j))],
        out_specs=pl.BlockSpec((None, rows, tn), lambda l, j: (l, 0, j)),
        out_shape=jax.ShapeDtypeStruct((n_layers, rows, n_out), F32),
        compiler_params=_params("parallel", "parallel"), name="ada_modulation",
    )(c_pad, w, b.reshape(n_layers, 1, n_out))
    return out[:, :bsz].reshape(n_layers, bsz, 3, 1, d)


def _norm_mod_kernel(x_ref, g_ref, sc_ref, sh_ref, o_ref):
    x = x_ref[...]
    y = x * lax.rsqrt(jnp.mean(x * x, axis=-1, keepdims=True) + EPS)
    o_ref[...] = ((y * g_ref[...]) * (1.0 + sc_ref[...]) + sh_ref[...]).astype(o_ref.dtype)


def _norm_mod_pack_kernel(x_ref, g_ref, sc_ref, sh_ref, o_ref, p_ref):
    x = x_ref[...]
    y = x * lax.rsqrt(jnp.mean(x * x, axis=-1, keepdims=True) + EPS)
    h = (y * g_ref[...]) * (1.0 + sc_ref[...]) + sh_ref[...]
    o_ref[...] = h.astype(o_ref.dtype)
    _store_slabs(p_ref, _pack_pairs(h))


def norm_modulate(x2, g, scale, shift, seq, pack=False):
    n, d = x2.shape
    tm = min(256, seq)
    per_batch = seq // tm
    row = lambda w: pl.BlockSpec((tm, w), lambda i: (i, 0))
    out_specs, out_shape = row(d), jax.ShapeDtypeStruct((n, d), BF16)
    if pack:
        slabs = d // 2 // LANES
        out_specs = [out_specs, pl.BlockSpec((tm, slabs, LANES), lambda i: (i, 0, 0))]
        out_shape = [out_shape, jax.ShapeDtypeStruct((n, slabs, LANES), I32)]
    return pl.pallas_call(
        _norm_mod_pack_kernel if pack else _norm_mod_kernel,
        grid=(n // tm,),
        in_specs=[row(d),
                  pl.BlockSpec((1, d), lambda i: (0, 0)),
                  pl.BlockSpec((None, 1, d), lambda i: (i // per_batch, 0, 0)),
                  pl.BlockSpec((None, 1, d), lambda i: (i // per_batch, 0, 0))],
        out_specs=out_specs,
        out_shape=out_shape,
        compiler_params=_params("parallel"), name="norm_modulate",
    )(x2, g.reshape(1, d), scale, shift)


def _final_norm_kernel(x_ref, g_ref, o_ref):
    x = x_ref[...]
    y = x * lax.rsqrt(jnp.mean(x * x, axis=-1, keepdims=True) + EPS)
    o_ref[...] = y * g_ref[...]


def final_norm(x2, g):
    n, d = x2.shape
    tm = min(256, n)
    return pl.pallas_call(
        _final_norm_kernel,
        grid=(n // tm,),
        in_specs=[pl.BlockSpec((tm, d), lambda i: (i, 0)), pl.BlockSpec((1, d), lambda i: (0, 0))],
        out_specs=pl.BlockSpec((tm, d), lambda i: (i, 0)),
        out_shape=jax.ShapeDtypeStruct((n, d), F32),
        compiler_params=_params("parallel"), name="final_norm",
    )(x2, g.reshape(1, d))


def _mm_kernel(a_ref, w_ref, o_ref):
    o_ref[...] = jnp.dot(a_ref[...], w_ref[...].astype(BF16), preferred_element_type=F32).astype(o_ref.dtype)


def _mm_colscale_kernel(a_ref, w_ref, s_ref, o_ref):
    w = (w_ref[...] * s_ref[...]).astype(BF16)
    o_ref[...] = jnp.dot(a_ref[...], w, preferred_element_type=F32).astype(o_ref.dtype)


def matmul(a, w, out_dtype, tm=1024, tn=512, n_out=None, col_scale=None):
    m, k = a.shape
    n = w.shape[1] if n_out is None else n_out
    tm, tn = min(tm, m), min(tn, n)
    in_specs = [pl.BlockSpec((tm, k), lambda i, j: (i, 0)),
                pl.BlockSpec((k, tn), lambda i, j: (0, j))]
    args = (a, w)
    if col_scale is not None:
        in_specs.append(pl.BlockSpec((1, tn), lambda i, j: (0, j)))
        args += (col_scale.reshape(1, -1),)
    return pl.pallas_call(
        _mm_kernel if col_scale is None else _mm_colscale_kernel,
        grid=(m // tm, n // tn),
        in_specs=in_specs,
        out_specs=pl.BlockSpec((tm, tn), lambda i, j: (i, j)),
        out_shape=jax.ShapeDtypeStruct((m, n), out_dtype),
        compiler_params=_params("parallel", "parallel"), name="projection",
    )(*args)


def _mm_res_kernel(a_ref, w_ref, r_ref, g_ref, o_ref):
    acc = jnp.dot(a_ref[...], w_ref[...].astype(BF16), preferred_element_type=F32)
    o_ref[...] = r_ref[...] + g_ref[...] * acc


def matmul_residual(a, w, res, gate, seq, tm=1024, tn=512):
    m, k = a.shape
    n = w.shape[1]
    tm, tn = min(tm, seq), min(tn, n)
    per_batch = seq // tm
    return pl.pallas_call(
        _mm_res_kernel,
        grid=(m // tm, n // tn),
        in_specs=[pl.BlockSpec((tm, k), lambda i, j: (i, 0)),
                  pl.BlockSpec((k, tn), lambda i, j: (0, j)),
                  pl.BlockSpec((tm, tn), lambda i, j: (i, j)),
                  pl.BlockSpec((None, 1, tn), lambda i, j: (i // per_batch, 0, j))],
        out_specs=pl.BlockSpec((tm, tn), lambda i, j: (i, j)),
        out_shape=jax.ShapeDtypeStruct((m, n), F32),
        compiler_params=_params("parallel", "parallel"), name="projection_residual",
    )(a, w, res, gate)


def _gelu(x):
    return 0.5 * x * (1.0 + lax.erf(x * (2.0 ** -0.5)))


def _gmlp_kernel(u_ref, v_ref, lng_ref, lnb_ref, ws_ref, bs_ref, o_ref, *, groups):
    u = _gelu(u_ref[...].astype(F32))
    v = _gelu(v_ref[...].astype(F32))
    mu = jnp.mean(v, axis=-1, keepdims=True)
    var = jnp.mean(jnp.square(v - mu), axis=-1, keepdims=True)
    vn = ((v - mu) * lax.rsqrt(var + EPS) * lng_ref[...] + lnb_ref[...]).astype(BF16)
    t = ws_ref.shape[1]
    gw = u.shape[1] // groups
    row = lax.broadcasted_iota(I32, (t, t), 0)
    col = lax.broadcasted_iota(I32, (t, t), 1)
    causal = col <= row
    for g in range(groups):
        w = jnp.where(causal, ws_ref[g], 0.0).astype(BF16)
        mixed = jnp.dot(w, vn[:, g * gw:(g + 1) * gw], preferred_element_type=F32) + bs_ref[:, g:g + 1]
        o_ref[:, g * gw:(g + 1) * gw] = (u[:, g * gw:(g + 1) * gw] * mixed).astype(o_ref.dtype)
    o_ref[:, u.shape[1]:] = jnp.zeros((t, o_ref.shape[1] - u.shape[1]), o_ref.dtype)


def gmlp_mixer(proj, a_width, ln_g, ln_b, w_s, b_s, out_width):
    n = proj.shape[0]
    groups, t, _ = w_s.shape
    return pl.pallas_call(
        functools.partial(_gmlp_kernel, groups=groups),
        grid=(n // t,),
        in_specs=[pl.BlockSpec((t, a_width), lambda i: (i, 0)),
                  pl.BlockSpec((t, a_width), lambda i: (i, 1)),
                  pl.BlockSpec((1, a_width), lambda i: (0, 0)),
                  pl.BlockSpec((1, a_width), lambda i: (0, 0)),
                  pl.BlockSpec((groups, t, t), lambda i: (0, 0, 0)),
                  pl.BlockSpec((t, groups), lambda i: (0, 0))],
        out_specs=pl.BlockSpec((t, out_width), lambda i: (i, 0)),
        out_shape=jax.ShapeDtypeStruct((n, out_width), BF16),
        compiler_params=_params("parallel"), name="gmlp_mixer",
    )(proj, proj, ln_g.reshape(1, a_width), ln_b.reshape(1, a_width), w_s, b_s.T)


def _rope_kernel(q_ref, k_ref, x_ref, cb_ref, sb_ref, ci_ref, si_ref, ck_ref, sk_ref,
                 qo_ref, ko_ref, qio_ref, kia_ref, kib_ref, wo_ref, *, heads, idx_pairs, q_scale, w_scale):
    cb, sb = cb_ref[...], sb_ref[...]
    half = HEAD_DIM // 2
    for h in range(heads):
        x = q_ref[:, h * HEAD_DIM:(h + 1) * HEAD_DIM].astype(F32)
        r = x * cb + pltpu.roll(x, half, 1) * sb
        qo_ref[:, h * HEAD_DIM:(h + 1) * HEAD_DIM] = (r * q_scale).astype(qo_ref.dtype)
    x = k_ref[...].astype(F32)
    ko_ref[...] = (x * cb + pltpu.roll(x, half, 1) * sb).astype(ko_ref.dtype)

    lane = lax.broadcasted_iota(I32, cb.shape, 1)
    first_half = (lane % IDX_DIM) < (IDX_DIM // 2)

    def rope_idx(x, c, s):
        rot = jnp.where(first_half, pltpu.roll(x, LANES - IDX_DIM // 2, 1), pltpu.roll(x, IDX_DIM // 2, 1))
        return x * c + rot * s

    ci, si = ci_ref[...], si_ref[...]
    for p in range(idx_pairs):
        x = x_ref[:, p * LANES:(p + 1) * LANES]
        qio_ref[:, p * LANES:(p + 1) * LANES] = rope_idx(x, ci, si).astype(qio_ref.dtype)
    tail = x_ref[:, idx_pairs * LANES:(idx_pairs + 1) * LANES]
    roped = rope_idx(tail, ck_ref[...], sk_ref[...])
    is_k = lane < IDX_DIM
    ka = jnp.where(is_k, roped, 0.0)
    kia_ref[...] = ka.astype(kia_ref.dtype)
    kib_ref[...] = pltpu.roll(ka, IDX_DIM, 1).astype(kib_ref.dtype)
    w = pltpu.roll(tail, LANES - IDX_DIM, 1)
    wo_ref[...] = jnp.where(lane < IDX_HEADS, w * w_scale, 0.0)


def rope_prepare(proj, proj_idx, tables, q_off, heads):
    n = proj.shape[0]
    tm = min(256, n)
    idx_pairs = IDX_HEADS * IDX_DIM // LANES
    qb = q_off // (heads * HEAD_DIM)
    kb = (q_off + heads * HEAD_DIM) // HEAD_DIM
    row = lambda w: pl.BlockSpec((tm, w), lambda i: (i, 0))
    kernel = functools.partial(_rope_kernel, heads=heads, idx_pairs=idx_pairs, q_scale=HEAD_DIM ** -0.5,
                               w_scale=IDX_HEADS ** -0.5 * IDX_DIM ** -0.5)
    return pl.pallas_call(
        kernel,
        grid=(n // tm,),
        in_specs=[pl.BlockSpec((tm, heads * HEAD_DIM), lambda i: (i, qb)),
                  pl.BlockSpec((tm, HEAD_DIM), lambda i: (i, kb)),
                  row(proj_idx.shape[1])] + [row(LANES)] * 6,
        out_specs=[row(heads * HEAD_DIM), row(HEAD_DIM), row(idx_pairs * LANES), row(LANES), row(LANES), row(LANES)],
        out_shape=[jax.ShapeDtypeStruct((n, heads * HEAD_DIM), BF16),
                   jax.ShapeDtypeStruct((n, HEAD_DIM), BF16),
                   jax.ShapeDtypeStruct((n, idx_pairs * LANES), BF16),
                   jax.ShapeDtypeStruct((n, LANES), BF16),
                   jax.ShapeDtypeStruct((n, LANES), BF16),
                   jax.ShapeDtypeStruct((n, LANES), F32)],
        compiler_params=_params("parallel"), name="rope_prepare",
    )(proj, proj, proj_idx, *tables)


def rope_tables(positions):
    pos = positions.reshape(-1).astype(F32)[:, None]

    def cs(dim):
        inv = ROPE_THETA ** (-jnp.arange(0, dim, 2, dtype=F32) / dim)
        ang = pos * inv
        return jnp.cos(ang), jnp.sin(ang)

    cb, sb = cs(HEAD_DIM)
    ci, si = cs(IDX_DIM)
    ones, zeros = jnp.ones_like(cb), jnp.zeros_like(cb)
    return (jnp.concatenate([cb, cb], 1), jnp.concatenate([-sb, sb], 1),
            jnp.concatenate([ci, ci, ci, ci], 1), jnp.concatenate([-si, si, -si, si], 1),
            jnp.concatenate([ci, ci, ones], 1), jnp.concatenate([-si, si, zeros], 1))


DSA_ATTN_TILE = 512


def _sort_key(x):
    x = jnp.where(x == 0.0, 0.0, x)
    bits = pltpu.bitcast(x, I32)
    return jnp.where(bits < 0, bits ^ 0x7FFFFFFF, bits)


def _dsa_kernel(qi_ref, w_ref, kia_ref, kib_ref, q_ref, k_ref, v_ref, buf_ref, o_ref, key_ref,
                *, tq, tk, heads, n_sel):
    del buf_ref
    qt = pl.program_id(1)
    q_lo = qt * tq
    n_kb = (q_lo + tq + tk - 1) // tk
    sub = tk // LANES
    row_pos = q_lo + lax.broadcasted_iota(I32, (tq, LANES), 0)
    lane = lax.broadcasted_iota(I32, (tq, LANES), 1)

    w = w_ref[...]
    w_cols = [jnp.broadcast_to(w[:, h:h + 1], (tq, tk)) for h in range(IDX_HEADS)]

    def score_block(kb, carry):
        k0 = pl.multiple_of(kb * tk, tk)
        ka = kia_ref[pl.ds(k0, tk), :]
        kb_ = kib_ref[pl.ds(k0, tk), :]
        acc = jnp.zeros((tq, tk), F32)
        for p in range(IDX_HEADS // 2):
            x = qi_ref[:, p * LANES:(p + 1) * LANES]
            ra = lax.dot_general(x, ka, (((1,), (1,)), ((), ())), preferred_element_type=F32)
            rb = lax.dot_general(x, kb_, (((1,), (1,)), ((), ())), preferred_element_type=F32)
            acc = acc + w_cols[2 * p] * jnp.maximum(ra, 0.0)
            acc = acc + w_cols[2 * p + 1] * jnp.maximum(rb, 0.0)
        key = _sort_key(acc)
        for j in range(sub):
            col_pos = k0 + j * LANES + lane
            key_ref[kb * sub + j] = jnp.where(col_pos <= row_pos, key[:, j * LANES:(j + 1) * LANES], INT_MIN)
        return carry

    lax.fori_loop(0, n_kb, score_block, 0)
    n_slabs = n_kb * sub

    def count(pred):
        def body(kb, acc):
            hits = [jnp.where(pred(key_ref[kb * sub + j], kb * sub + j), 1, 0) for j in range(sub)]
            while len(hits) > 1:
                hits = [a + b for a, b in zip(hits[::2], hits[1::2])]
            return acc + hits[0]
        part = lax.fori_loop(0, n_kb, body, jnp.zeros((tq, LANES), I32))
        return jnp.sum(part.astype(F32), axis=1, keepdims=True).astype(I32)

    def search_bit(i, u):
        cand = u | (1 << (31 - i))
        thr = jnp.broadcast_to(cand ^ INT_MIN, (tq, LANES))
        total = count(lambda kv, c: kv >= thr)
        return jnp.where(total >= n_sel, cand, u)

    u = lax.fori_loop(0, 32, search_bit, jnp.zeros((tq, 1), I32))
    thr = u ^ INT_MIN
    thr_b = jnp.broadcast_to(thr, (tq, LANES))
    n_gt = count(lambda kv, c: kv > thr_b)
    n_ge = count(lambda kv, c: kv >= thr_b)
    need = n_sel - n_gt
    tie_rows = jnp.logical_and(n_ge > n_sel, need > 0)

    seq_bits = max(1, (key_ref.shape[0] * LANES).bit_length())

    def tie_search():
        need_b = need

        def bit_step(i, j):
            cand = j | (1 << (seq_bits - 1 - i))
            cand_b = jnp.broadcast_to(cand, (tq, LANES))
            total = count(lambda kv, c: jnp.logical_and(kv == thr_b, c * LANES + lane < cand_b))
            return jnp.where(total <= need_b, cand, j)

        return lax.fori_loop(0, seq_bits, bit_step, jnp.zeros((tq, 1), I32))

    any_tie = jnp.max(jnp.where(tie_rows, 1.0, 0.0)) > 0.0
    bound = lax.cond(any_tie, tie_search, lambda: jnp.full((tq, 1), 2 ** 30, I32))
    bound = jnp.where(tie_rows, bound, 2 ** 30)
    bound_b = jnp.broadcast_to(bound, (tq, LANES))

    q_all = jnp.concatenate([q_ref[:, h * HEAD_DIM:(h + 1) * HEAD_DIM] for h in range(heads)], axis=0)

    ta = min(tk, DSA_ATTN_TILE)
    sub_a = ta // LANES

    def attn_block(kb, carry):
        m, l, acc = carry
        k0 = pl.multiple_of(kb * ta, ta)
        kk = k_ref[pl.ds(k0, ta), :]
        vv = v_ref[pl.ds(k0, ta), :]
        s = lax.dot_general(q_all, kk, (((1,), (1,)), ((), ())), preferred_element_type=F32)
        bias_cols = []
        for j in range(sub_a):
            kv = key_ref[kb * sub_a + j]
            col_pos = k0 + j * LANES + lane
            take = jnp.logical_or(kv > thr_b, jnp.logical_and(kv == thr_b, col_pos < bound_b))
            take = jnp.logical_and(take, col_pos <= row_pos)
            bias_cols.append(jnp.where(take, 0.0, MASKED))
        bias = jnp.concatenate(bias_cols, axis=1)
        s = s.reshape(heads, tq, ta) + bias[None]
        m_new = jnp.maximum(m, jnp.max(s, axis=-1, keepdims=True))
        alpha = jnp.exp(m - m_new)
        p = jnp.exp((s - m_new).astype(BF16))
        l = alpha * l + jnp.sum(p.astype(F32), axis=-1, keepdims=True)
        pv = jnp.dot(p.reshape(heads * tq, ta), vv, preferred_element_type=F32)
        acc = alpha * acc + pv.reshape(heads, tq, HEAD_DIM)
        return m_new, l, acc

    init = (jnp.full((heads, tq, 1), MASKED, F32), jnp.zeros((heads, tq, 1), F32),
            jnp.zeros((heads, tq, HEAD_DIM), F32))
    m, l, acc = lax.fori_loop(0, n_kb * (tk // ta), attn_block, init)
    out = acc / l
    for h in range(heads):
        o_ref[:, h * HEAD_DIM:(h + 1) * HEAD_DIM] = out[h].astype(o_ref.dtype)


def dsa_mixer(qi, w_idx, kia, kib, q, k, proj, v_col_block, buf, seq, n_sel):
    n = q.shape[0]
    bsz = n // seq
    heads = q.shape[1] // HEAD_DIM
    tq = min(128, seq)
    tk = min(512, seq)
    nq = seq // tq
    kernel = functools.partial(_dsa_kernel, tq=tq, tk=tk, heads=heads, n_sel=n_sel)
    qrow = lambda w: pl.BlockSpec((tq, w), lambda b, i: (b * nq + i, 0))
    kv = lambda c: pl.BlockSpec((seq, LANES), lambda b, i: (b, c))
    return pl.pallas_call(
        kernel,
        grid=(bsz, nq),
        in_specs=[qrow(qi.shape[1]), qrow(LANES), kv(0), kv(0), qrow(q.shape[1]), kv(0), kv(v_col_block),
                  pl.BlockSpec(memory_space=pl.ANY)],
        out_specs=pl.BlockSpec((tq, heads * HEAD_DIM), lambda b, i: (b * nq + i, 1)),
        out_shape=jax.ShapeDtypeStruct(buf.shape, buf.dtype),
        scratch_shapes=[pltpu.VMEM((seq // LANES, tq, LANES), I32)],
        input_output_aliases={7: 0},
        compiler_params=_params("parallel", "arbitrary"), name="dsa_mixer",
    )(qi, w_idx, kia, kib, q, k, proj, buf)


SB_HEADS_PER_STEP = 4
SB_QUERY_TILE = 512
SB_KEY_TILE = 256
SB_LOGIT_SCALE = HEAD_DIM ** -0.5 * LOG2E


def _sb_kernel(q_ref, k_ref, v_ref, o_ref, *, tq, tk, group):
    qt = pl.program_id(2)
    ratio = tq // tk
    row = lax.broadcasted_iota(I32, (tk, tk), 0)
    col = lax.broadcasted_iota(I32, (tk, tk), 1)
    later = jnp.where(row > col, 1.0, 0.0).astype(BF16)
    q_pos = qt * tq + lax.broadcasted_iota(I32, (tq, tk), 0)
    k_off = lax.broadcasted_iota(I32, (tq, tk), 1)

    def block(kb, masked, carry):
        k0 = pl.multiple_of(kb * tk, tk)
        heads = range(group)
        cols = [slice(h * HEAD_DIM, (h + 1) * HEAD_DIM) for h in heads]
        z = [lax.dot_general(q_ref[:, cols[h]], k_ref[pl.ds(k0, tk), cols[h]], (((1,), (1,)), ((), ())),
                             preferred_element_type=F32).astype(BF16) for h in heads]
        log_beta = [jnp.minimum(z[h], 0.0) - jnp.log(1.0 + jnp.exp2(-jnp.abs(z[h]))) * LOG2E for h in heads]
        log_fail = [log_beta[h] - z[h] for h in heads]
        if masked:
            strict = k0 + k_off < q_pos
            log_fail = [jnp.where(strict, log_fail[h], 0.0) for h in heads]
        between = [jnp.dot(log_fail[h].astype(BF16), later, preferred_element_type=F32) + carry[h][1]
                   for h in heads]
        a = [jnp.exp2(log_beta[h] + between[h]) for h in heads]
        if masked:
            a = [jnp.where(strict, a[h], 0.0) for h in heads]
        acc = [carry[h][0] + jnp.dot(a[h].astype(BF16), v_ref[pl.ds(k0, tk), cols[h]], preferred_element_type=F32)
               for h in heads]
        run = [between[h][:, 0:1] + log_fail[h][:, 0:1] for h in heads]
        return tuple((acc[h], run[h]) for h in heads)

    carry = tuple((jnp.zeros((tq, HEAD_DIM), F32), jnp.zeros((tq, 1), F32)) for _ in range(group))
    first_full = qt * ratio
    for j in reversed(range(ratio)):
        carry = block(first_full + j, True, carry)
    carry = lax.fori_loop(0, first_full, lambda i, c: block(first_full - 1 - i, False, c), carry)
    for h in range(group):
        o_ref[:, h * HEAD_DIM:(h + 1) * HEAD_DIM] = carry[h][0].astype(o_ref.dtype)


def stick_breaking_mixer(qkv, seq, heads):
    n = qkv.shape[0]
    bsz = n // seq
    tq = min(SB_QUERY_TILE, seq)
    tk = min(SB_KEY_TILE, seq)
    nq = seq // tq
    group = SB_HEADS_PER_STEP
    hg = heads // group
    width = group * HEAD_DIM
    kernel = functools.partial(_sb_kernel, tq=tq, tk=tk, group=group)
    return pl.pallas_call(
        kernel,
        grid=(bsz, hg, nq),
        in_specs=[pl.BlockSpec((tq, width), lambda b, h, i: (b * nq + i, h)),
                  pl.BlockSpec((seq, width), lambda b, h, i: (b, hg + h)),
                  pl.BlockSpec((seq, width), lambda b, h, i: (b, 2 * hg + h))],
        out_specs=pl.BlockSpec((tq, width), lambda b, h, i: (b * nq + i, h)),
        out_shape=jax.ShapeDtypeStruct((n, heads * HEAD_DIM), BF16),
        compiler_params=_params("parallel", "parallel", "arbitrary"), name="stick_breaking",
    )(qkv, qkv, qkv)


def _router_kernel(h_ref, wr_ref, b_ref, s_ref, w_ref, c_ref, cnt_ref, *, n_experts):
    per_group = n_experts // N_GROUPS
    tm = h_ref.shape[0]
    logits = lax.dot_general(wr_ref[...], h_ref[...], (((1,), (1,)), ((), ())), preferred_element_type=F32)
    scores = jax.nn.sigmoid(logits)
    choice = (scores + b_ref[...]).reshape(N_GROUPS, per_group, tm)
    s3 = scores.reshape(N_GROUPS, per_group, tm)
    neg = -jnp.inf
    in_group = lax.broadcasted_iota(I32, choice.shape, 1)
    m1 = jnp.max(choice, axis=1, keepdims=True)
    first = jnp.min(jnp.where(choice == m1, in_group, per_group), axis=1, keepdims=True)
    m2 = jnp.max(jnp.where(in_group == first, neg, choice), axis=1, keepdims=True)
    group_score = m1 + m2
    gid = lax.broadcasted_iota(I32, group_score.shape, 0)
    group_sel = jnp.zeros(group_score.shape, jnp.bool_)
    for _ in range(TOPK_GROUPS):
        m = jnp.max(group_score, axis=0, keepdims=True)
        f = jnp.min(jnp.where(group_score == m, gid, N_GROUPS), axis=0, keepdims=True)
        hit = gid == f
        group_sel = jnp.logical_or(group_sel, hit)
        group_score = jnp.where(hit, neg, group_score)
    cand = jnp.where(group_sel, choice, neg)
    eid = lax.broadcasted_iota(I32, choice.shape, 0) * per_group + in_group
    sel = jnp.zeros(choice.shape, jnp.bool_)
    hits = []
    for _ in range(TOP_K):
        m = jnp.max(jnp.max(cand, axis=1, keepdims=True), axis=0, keepdims=True)
        f = jnp.min(jnp.min(jnp.where(cand == m, eid, n_experts), axis=1, keepdims=True), axis=0, keepdims=True)
        hit = eid == f
        hits.append((hit, f))
        sel = jnp.logical_or(sel, hit)
        cand = jnp.where(hit, neg, cand)
    top_w = jnp.where(sel, s3, 0.0)
    total = jnp.sum(jnp.sum(top_w, axis=1, keepdims=True), axis=0, keepdims=True)
    gates = top_w / total * ROUTED_SCALE

    @pl.when(pl.program_id(0) == 0)
    def _():
        cnt_ref[...] = jnp.zeros_like(cnt_ref)

    sel_f = jnp.where(sel, 1.0, 0.0).reshape(n_experts, tm)
    row = lax.broadcasted_iota(I32, (tm, tm), 0)
    col = lax.broadcasted_iota(I32, (tm, tm), 1)
    before = jnp.where(row < col, 1.0, 0.0).astype(BF16)
    prefix = jnp.dot(sel_f.astype(BF16), before, preferred_element_type=F32)
    rank_all = (prefix + cnt_ref[:, 0:1]).reshape(choice.shape)

    def pick(hit, val):
        return jnp.sum(jnp.sum(jnp.where(hit, val, 0.0), axis=1, keepdims=True), axis=0, keepdims=True)

    for k, (hit, f) in enumerate(hits):
        rank = pick(hit, rank_all).reshape(1, tm).astype(I32)
        s_ref[k:k + 1, :] = rank | lax.shift_left(f.reshape(1, tm), SLOT_RANK_BITS)
        w_ref[k:k + 1, :] = pick(hit, gates).reshape(1, tm)
    cnt_ref[...] = cnt_ref[...] + jnp.sum(sel_f, axis=1, keepdims=True)
    c_ref[...] = cnt_ref[...].astype(I32)


def moe_router(h, w_router, bias):
    n, d = h.shape
    e = w_router.shape[1]
    tm = min(512, n)
    top = lambda dt: jax.ShapeDtypeStruct((TOP_K, n), dt)
    top_spec = pl.BlockSpec((TOP_K, tm), lambda i: (0, i))
    slots, wts, counts = pl.pallas_call(
        functools.partial(_router_kernel, n_experts=e),
        grid=(n // tm,),
        in_specs=[pl.BlockSpec((tm, d), lambda i: (i, 0)),
                  pl.BlockSpec((e, d), lambda i: (0, 0)),
                  pl.BlockSpec((e, 1), lambda i: (0, 0))],
        out_specs=[top_spec, top_spec, pl.BlockSpec((e, LANES), lambda i: (0, 0))],
        out_shape=[top(I32), top(F32), jax.ShapeDtypeStruct((e, LANES), I32)],
        scratch_shapes=[pltpu.VMEM((e, LANES), F32)],
        compiler_params=_params("arbitrary"), name="moe_router",
    )(h, w_router.T.astype(BF16), bias.reshape(e, 1))
    return slots.reshape(-1), wts, counts[:, 0]


MOE_TILE = 512
DISPATCH_TOKENS = 256
COMBINE_TOKENS = 128
DMA_ISSUE_UNROLL = 16
PAD_CHUNK = 512


def _pack_pairs(x):
    w = x.shape[1] // 2
    lo = pltpu.bitcast(x[:, :w].astype(BF16).astype(F32), I32)
    hi = pltpu.bitcast(x[:, w:].astype(BF16).astype(F32), I32)
    return lax.shift_right_logical(lo, 16) | hi


def _unpack_pairs(p):
    lo = pltpu.bitcast(lax.shift_left(p, 16), F32)
    hi = pltpu.bitcast(p & jnp.int32(-65536), F32)
    return lo, hi


def _store_slabs(ref, x):
    r, n_slabs, _ = ref.shape
    flat = ref.reshape(r * n_slabs, LANES)
    for s in range(n_slabs):
        flat[pl.ds(s, r, stride=n_slabs), :] = x[:, s * LANES:(s + 1) * LANES]


def _load_slab_column(ref, s):
    r, n_slabs, _ = ref.shape
    return ref.reshape(r * n_slabs, LANES)[pl.ds(s, r, stride=n_slabs), :]


def _load_slabs(ref):
    return jnp.concatenate([_load_slab_column(ref, s) for s in range(ref.shape[1])], axis=1)


SLOT_RANK_BITS = 20


def _dispatch_kernel(row_ref, off_ref, cnt_ref, hp_ref, xs_ref, zero_ref, sem, pad_sem, *, td, n_tok, n_rows):
    base = pl.program_id(0) * td

    @pl.when(pl.program_id(0) == 0)
    def _():
        zero_ref[...] = jnp.zeros(zero_ref.shape, zero_ref.dtype)
        n_experts = cnt_ref.shape[0]

        chunk = zero_ref.shape[0]
        bits = chunk.bit_length() - 1

        def zero_copy(row, size):
            return pltpu.make_async_copy(zero_ref.at[pl.ds(0, size)], xs_ref.at[pl.ds(row, size)], pad_sem)

        def full_chunk(c, row):
            zero_copy(row, chunk).start()
            zero_copy(row, chunk).wait()
            return row + chunk

        def pad_expert(e, carry):
            start = off_ref[e] + cnt_ref[e]
            end = jnp.where(e + 1 < n_experts, off_ref[jnp.minimum(e + 1, n_experts - 1)], n_rows)
            row = lax.fori_loop(0, (end - start) // chunk, full_chunk, start)
            rem = (end - start) % chunk
            pieces = [(rem & (1 << b), row + lax.shift_left(lax.shift_right_logical(rem, b + 1), b + 1), 1 << b)
                      for b in reversed(range(bits))]
            for go, at, size in pieces:
                @pl.when(go != 0)
                def _(at=at, size=size):
                    zero_copy(at, size).start()
            for go, at, size in pieces:
                @pl.when(go != 0)
                def _(at=at, size=size):
                    zero_copy(at, size).wait()
            return carry

        lax.fori_loop(0, n_experts, pad_expert, 0)

    for k in range(TOP_K):
        def issue(t, carry, k=k):
            pltpu.make_async_copy(hp_ref.at[t], xs_ref.at[row_ref[k * n_tok + base + t]], sem).start()
            return carry

        lax.fori_loop(0, td, issue, 0, unroll=DMA_ISSUE_UNROLL)
    for k in range(TOP_K):
        pltpu.make_async_copy(hp_ref, xs_ref.at[pl.ds(0, td)], sem).wait()


def moe_dispatch(hp, slots, offsets, counts, n_rows):
    n, s, _ = hp.shape
    td = min(DISPATCH_TOKENS, n)
    kernel = functools.partial(_dispatch_kernel, td=td, n_tok=n, n_rows=n_rows)
    return pl.pallas_call(
        kernel,
        grid_spec=pltpu.PrefetchScalarGridSpec(
            num_scalar_prefetch=3,
            grid=(n // td,),
            in_specs=[pl.BlockSpec((td, s, LANES), lambda i, *_: (i, 0, 0))],
            out_specs=pl.BlockSpec(memory_space=pl.ANY),
            scratch_shapes=[pltpu.VMEM((PAD_CHUNK, s, LANES), I32), pltpu.SemaphoreType.DMA(()),
                            pltpu.SemaphoreType.DMA(())]),
        out_shape=jax.ShapeDtypeStruct((n_rows, s, LANES), I32),
        compiler_params=_params("arbitrary"), name="moe_dispatch",
    )(slots, offsets, counts, hp)


def _swiglu(lo, hi, wg_ref, wu_ref, wd_ref):
    w = lo.shape[1]
    g = (jnp.dot(lo, wg_ref[:w, :].astype(BF16), preferred_element_type=F32)
         + jnp.dot(hi, wg_ref[w:, :].astype(BF16), preferred_element_type=F32))
    u = (jnp.dot(lo, wu_ref[:w, :].astype(BF16), preferred_element_type=F32)
         + jnp.dot(hi, wu_ref[w:, :].astype(BF16), preferred_element_type=F32))
    act = g * jax.nn.sigmoid(g) * u
    return jnp.dot(act.astype(BF16), wd_ref[...].astype(BF16), preferred_element_type=F32)


def _grouped_kernel(te_ref, nu_ref, x_ref, wg_ref, wu_ref, wd_ref, y_ref, wg_bf, wu_bf, wd_bf):
    i = pl.program_id(0)

    @pl.when(i < nu_ref[0])
    def _():
        @pl.when(jnp.logical_or(i == 0, te_ref[i] != te_ref[jnp.maximum(i - 1, 0)]))
        def _():
            wg_bf[...] = wg_ref[...].astype(BF16)
            wu_bf[...] = wu_ref[...].astype(BF16)
            wd_bf[...] = wd_ref[...].astype(BF16)

        lo, hi = _unpack_pairs(_load_slabs(x_ref))
        _store_slabs(y_ref, _pack_pairs(_swiglu(lo.astype(BF16), hi.astype(BF16), wg_bf, wu_bf, wd_bf)))


def moe_grouped(xs, tile_expert, n_used, wg, wu, wd, layer):
    p, s, _ = xs.shape
    _, e, d, f = wg.shape
    tm = MOE_TILE
    row = lambda i, te, nu: (jnp.minimum(i, nu[0] - 1), 0, 0)
    expert = lambda i, te, nu: (layer, te[i], 0, 0)
    return pl.pallas_call(
        _grouped_kernel,
        grid_spec=pltpu.PrefetchScalarGridSpec(
            num_scalar_prefetch=2,
            grid=(p // tm,),
            in_specs=[pl.BlockSpec((tm, s, LANES), row),
                      pl.BlockSpec((None, None, d, f), expert),
                      pl.BlockSpec((None, None, d, f), expert),
                      pl.BlockSpec((None, None, f, d), expert)],
            out_specs=pl.BlockSpec((tm, s, LANES), row),
            scratch_shapes=[pltpu.VMEM((d, f), BF16), pltpu.VMEM((d, f), BF16), pltpu.VMEM((f, d), BF16)]),
        out_shape=jax.ShapeDtypeStruct((p, s, LANES), I32),
        compiler_params=_params("arbitrary"), name="moe_grouped",
    )(tile_expert, n_used, xs, wg, wu, wd)


def _shared_kernel(h_ref, wg_ref, wu_ref, wd_ref, o_ref):
    x = h_ref[...]
    w = x.shape[1] // 2
    o_ref[...] = _swiglu(x[:, :w], x[:, w:], wg_ref, wu_ref, wd_ref).astype(o_ref.dtype)


def shared_expert(h, wg, wu, wd):
    n, d = h.shape
    f = wg.shape[1]
    tm = min(512, n)
    return pl.pallas_call(
        _shared_kernel,
        grid=(n // tm,),
        in_specs=[pl.BlockSpec((tm, d), lambda i: (i, 0)),
                  pl.BlockSpec((d, f), lambda i: (0, 0)),
                  pl.BlockSpec((d, f), lambda i: (0, 0)),
                  pl.BlockSpec((f, d), lambda i: (0, 0))],
        out_specs=pl.BlockSpec((tm, d), lambda i: (i, 0)),
        out_shape=jax.ShapeDtypeStruct((n, d), BF16),
        compiler_params=_params("parallel"), name="shared_expert",
    )(h, wg, wu, wd)


def _combine_kernel(row_ref, ys_ref, w_ref, x_ref, sh_ref, g_ref, o_ref, rows_ref, sems, *, tc, n_tok, n_steps):
    step = pl.program_id(0)
    slot = step % 2

    def start_gather(which_step, which_slot):
        base = which_step * tc
        for k in range(TOP_K):
            def issue(t, carry, k=k):
                pltpu.make_async_copy(ys_ref.at[row_ref[k * n_tok + base + t]], rows_ref.at[which_slot, k, t],
                                      sems.at[which_slot]).start()
                return carry

            lax.fori_loop(0, tc, issue, 0, unroll=DMA_ISSUE_UNROLL)

    @pl.when(step == 0)
    def _():
        start_gather(0, 0)

    @pl.when(step + 1 < n_steps)
    def _():
        start_gather(step + 1, 1 - slot)

    for k in range(TOP_K):
        pltpu.make_async_copy(ys_ref.at[pl.ds(0, tc)], rows_ref.at[slot, k], sems.at[slot]).wait()

    half = x_ref.shape[1] // 2
    w_cols = [w_ref[:, k:k + 1] for k in range(TOP_K)]
    for s in range(rows_ref.shape[3]):
        acc_lo = jnp.zeros((tc, LANES), F32)
        acc_hi = jnp.zeros((tc, LANES), F32)
        for k in range(TOP_K):
            lo, hi = _unpack_pairs(_load_slab_column(rows_ref.at[slot, k], s))
            acc_lo = acc_lo + w_cols[k] * lo
            acc_hi = acc_hi + w_cols[k] * hi
        for acc, c0 in ((acc_lo, s * LANES), (acc_hi, half + s * LANES)):
            cols = slice(c0, c0 + LANES)
            o_ref[:, cols] = x_ref[:, cols] + g_ref[:, cols] * (acc + sh_ref[:, cols].astype(F32))


def moe_combine(ys, slots, wts, x2, shared, gate, seq):
    n, d = x2.shape
    s = ys.shape[1]
    tc = min(COMBINE_TOKENS, seq)
    per_batch = seq // tc
    kernel = functools.partial(_combine_kernel, tc=tc, n_tok=n, n_steps=n // tc)
    return pl.pallas_call(
        kernel,
        grid_spec=pltpu.PrefetchScalarGridSpec(
            num_scalar_prefetch=1,
            grid=(n // tc,),
            in_specs=[pl.BlockSpec(memory_space=pl.ANY),
                      pl.BlockSpec((tc, TOP_K), lambda i, *_: (i, 0)),
                      pl.BlockSpec((tc, d), lambda i, *_: (i, 0)),
                      pl.BlockSpec((tc, d), lambda i, *_: (i, 0)),
                      pl.BlockSpec((None, 1, d), lambda i, *_: (i // per_batch, 0, 0))],
            out_specs=pl.BlockSpec((tc, d), lambda i, *_: (i, 0)),
            scratch_shapes=[pltpu.VMEM((2, TOP_K, tc, s, LANES), I32), pltpu.SemaphoreType.DMA((2,))]),
        out_shape=jax.ShapeDtypeStruct((n, d), F32),
        compiler_params=_params("arbitrary"), name="moe_combine",
    )(slots, ys, wts, x2, shared, gate)


def moe_layout(slots, counts, n_tiles):
    e = counts.shape[0]
    tiles = (counts + MOE_TILE - 1) // MOE_TILE
    earlier = jnp.arange(e)[None, :] <= jnp.arange(e)[:, None]
    tile_end = jnp.sum(jnp.where(earlier, tiles[None, :], 0), axis=1)
    offsets = (tile_end - tiles) * MOE_TILE
    n_used = tile_end[-1]
    tile_ids = jnp.minimum(jnp.arange(n_tiles, dtype=I32), n_used - 1)
    tile_expert = jnp.sum(tile_end[None, :] <= tile_ids[:, None], axis=1)
    expert = lax.shift_right_logical(slots, SLOT_RANK_BITS)
    rows = (jnp.sum(jnp.where(expert[:, None] == jnp.arange(e)[None, :], offsets[None, :], 0), axis=1)
            + (slots & ((1 << SLOT_RANK_BITS) - 1)))
    return rows.astype(I32), offsets.astype(I32), tile_expert.astype(I32), n_used.reshape(1).astype(I32)


def _pad_cols(w, width):
    return jnp.pad(w, ((0, 0), (0, width - w.shape[1])))


def kernel(x, c, positions, ab_w_in, ab_w_out, gmlp_ln_g, gmlp_ln_b, gmlp_w_s, gmlp_b_s, sb_w_qkv, sb_w_out,
           norm_mix_g, ada_mix_w, ada_mix_b, norm_ffn_g, ada_ffn_w, ada_ffn_b, router_w, router_bias,
           expert_w_gate, expert_w_up, expert_w_down, shared_w_gate, shared_w_up, shared_w_down, final_norm_g):
    bsz, seq, d = x.shape
    depth = norm_mix_g.shape[0]
    a_width = gmlp_ln_g.shape[1]
    b_width = ab_w_out.shape[1] - a_width
    b_heads = b_width // HEAD_DIM
    c_heads = sb_w_out.shape[1] // HEAD_DIM
    n_sel = min(TOPK_MAX, seq // 4)
    main_width = 2 * a_width + b_width + 2 * HEAD_DIM
    idx_width = IDX_HEADS * IDX_DIM + LANES

    mod_mix = ada_modulation_all(c, ada_mix_w, ada_mix_b)
    mod_ffn = ada_modulation_all(c, ada_ffn_w, ada_ffn_b)
    tables = rope_tables(positions)
    x2 = x.reshape(bsz * seq, d)

    for layer in range(depth):
        j = layer // 2
        shift, scale, gate = mod_mix[layer, :, 0], mod_mix[layer, :, 1], mod_mix[layer, :, 2]
        h = norm_modulate(x2, norm_mix_g[layer], scale, shift, seq)
        if layer % 2 == 0:
            w_in = ab_w_in[j]
            proj = matmul(h, w_in, BF16, tn=640 if main_width % 640 == 0 else 128, n_out=main_width)
            proj_idx = matmul(h, _pad_cols(w_in[:, main_width:], idx_width).astype(BF16), F32, tn=idx_width)
            q, k, qi, kia, kib, w_idx = rope_prepare(proj, proj_idx, tables, 2 * a_width, b_heads)
            buf = gmlp_mixer(proj, a_width, gmlp_ln_g[j], gmlp_ln_b[j], gmlp_w_s[j], gmlp_b_s[j],
                             a_width + b_width)
            v_col_block = (2 * a_width + b_width + HEAD_DIM) // HEAD_DIM
            mixed = dsa_mixer(qi, w_idx, kia, kib, q, k, proj, v_col_block, buf, seq, n_sel)
            x2 = matmul_residual(mixed, ab_w_out[j], x2, gate, seq)
        else:
            c_width = c_heads * HEAD_DIM
            col_scale = jnp.where(jnp.arange(3 * c_width) < c_width, SB_LOGIT_SCALE, 1.0).astype(F32)
            qkv = matmul(h, sb_w_qkv[j], BF16, col_scale=col_scale)
            o = stick_breaking_mixer(qkv, seq, c_heads)
            x2 = matmul_residual(o, sb_w_out[j], x2, gate, seq)

        shift, scale, gate = mod_ffn[layer, :, 0], mod_ffn[layer, :, 1], mod_ffn[layer, :, 2]
        h, h_packed = norm_modulate(x2, norm_ffn_g[layer], scale, shift, seq, pack=True)
        slots, wts, counts = moe_router(h, router_w[layer], router_bias[layer])
        n_tiles = bsz * seq * TOP_K // MOE_TILE + router_w.shape[2]
        rows, offsets, tile_expert, n_used = moe_layout(slots, counts, n_tiles)
        xs = moe_dispatch(h_packed, rows, offsets, counts, n_tiles * MOE_TILE)
        ys = moe_grouped(xs, tile_expert, n_used, expert_w_gate, expert_w_up, expert_w_down, layer)
        shared = shared_expert(h, shared_w_gate[layer], shared_w_up[layer], shared_w_down[layer])
        x2 = moe_combine(ys, rows, wts.T, x2, shared, gate, seq)

    return final_norm(x2, final_norm_g).reshape(bsz, seq, d)
```

```python
import functools

import jax
import jax.numpy as jnp
from jax import lax
from jax.experimental import pallas as pl
from jax.experimental.pallas import tpu as pltpu

F32 = jnp.float32
BF16 = jnp.bfloat16
I32 = jnp.int32

EPS = 1e-6
ROPE_THETA = 10000.0
LANES = 128
HEAD_DIM = 128
A_GROUPS = 8
A_CHUNK = 128
IDX_HEADS = 16
IDX_DIM = 64
TOPK_MAX = 256
N_GROUPS = 8
TOPK_GROUPS = 4
TOP_K = 8
ROUTED_SCALE = 2.5
VMEM_LIMIT = 56 * 1024 * 1024
INT_MIN = -(2 ** 31)
MASKED = -1e30
LOG2E = 1.4426950408889634


def _params(*sem):
    return pltpu.CompilerParams(dimension_semantics=sem, vmem_limit_bytes=VMEM_LIMIT)


def _mod_kernel(c_ref, w_ref, b_ref, o_ref):
    c = c_ref[...]
    s = c * jax.nn.sigmoid(c)
    o_ref[...] = jnp.dot(s.astype(BF16), w_ref[...].astype(BF16), preferred_element_type=F32) + b_ref[...]


def ada_modulation_all(c, w, b):
    bsz, d = c.shape
    n_layers, _, n_out = w.shape
    rows = 8
    c_pad = jnp.zeros((rows, d), F32).at[:bsz].set(c)
    tn = min(512, n_out)
    out = pl.pallas_call(
        _mod_kernel,
        grid=(n_layers, n_out // tn),
        in_specs=[pl.BlockSpec((rows, d), lambda l, j: (0, 0)),
                  pl.BlockSpec((None, d, tn), lambda l, j: (l, 0, j)),
                  pl.BlockSpec((None, 1, tn), lambda l, j: (l, 0, j))],
        out_specs=pl.BlockSpec((None, rows, tn), lambda l, j: (l, 0, j)),
        out_shape=jax.ShapeDtypeStruct((n_layers, rows, n_out), F32),
        compiler_params=_params("parallel", "parallel"), name="ada_modulation",
    )(c_pad, w, b.reshape(n_layers, 1, n_out))
    return out[:, :bsz].reshape(n_layers, bsz, 3, 1, d)


def _inv_rms(x_ref):
    rows, d = x_ref.shape
    acc = jnp.zeros((rows, LANES), F32)
    for c in range(0, d, LANES):
        xc = x_ref[:, c:c + LANES]
        acc = acc + xc * xc
    return lax.rsqrt(jnp.sum(acc, axis=-1, keepdims=True) * (1.0 / d) + EPS)


def _norm_mod_kernel(x_ref, g_ref, sc_ref, sh_ref, o_ref):
    inv = _inv_rms(x_ref)
    for c in range(0, x_ref.shape[1], LANES):
        cols = slice(c, c + LANES)
        h = ((x_ref[:, cols] * inv) * g_ref[:, cols]) * (1.0 + sc_ref[:, cols]) + sh_ref[:, cols]
        o_ref[:, cols] = h.astype(o_ref.dtype)


def _norm_mod_pack_kernel(x_ref, g_ref, sc_ref, sh_ref, o_ref, p_ref):
    inv = _inv_rms(x_ref)
    rows, n_slabs, _ = p_ref.shape
    half = n_slabs * LANES
    flat = p_ref.reshape(rows * n_slabs, LANES)
    for s in range(n_slabs):
        pair = []
        for c in (s * LANES, half + s * LANES):
            cols = slice(c, c + LANES)
            h = ((x_ref[:, cols] * inv) * g_ref[:, cols]) * (1.0 + sc_ref[:, cols]) + sh_ref[:, cols]
            o_ref[:, cols] = h.astype(o_ref.dtype)
            pair.append(h)
        flat[pl.ds(s, rows, stride=n_slabs), :] = _pack_pairs(jnp.concatenate(pair, axis=1))


def norm_modulate(x2, g, scale, shift, seq, pack=False):
    n, d = x2.shape
    tm = min(256, seq)
    per_batch = seq // tm
    row = lambda w: pl.BlockSpec((tm, w), lambda i: (i, 0))
    out_specs, out_shape = row(d), jax.ShapeDtypeStruct((n, d), BF16)
    if pack:
        slabs = d // 2 // LANES
        out_specs = [out_specs, pl.BlockSpec((tm, slabs, LANES), lambda i: (i, 0, 0))]
        out_shape = [out_shape, jax.ShapeDtypeStruct((n, slabs, LANES), I32)]
    return pl.pallas_call(
        _norm_mod_pack_kernel if pack else _norm_mod_kernel,
        grid=(n // tm,),
        in_specs=[row(d),
                  pl.BlockSpec((1, d), lambda i: (0, 0)),
                  pl.BlockSpec((None, 1, d), lambda i: (i // per_batch, 0, 0)),
                  pl.BlockSpec((None, 1, d), lambda i: (i // per_batch, 0, 0))],
        out_specs=out_specs,
        out_shape=out_shape,
        compiler_params=_params("parallel"), name="norm_modulate",
    )(x2, g.reshape(1, d), scale, shift)


def _final_norm_kernel(x_ref, g_ref, o_ref):
    inv = _inv_rms(x_ref)
    for c in range(0, x_ref.shape[1], LANES):
        cols = slice(c, c + LANES)
        o_ref[:, cols] = (x_ref[:, cols] * inv) * g_ref[:, cols]


def final_norm(x2, g):
    n, d = x2.shape
    tm = min(256, n)
    return pl.pallas_call(
        _final_norm_kernel,
        grid=(n // tm,),
        in_specs=[pl.BlockSpec((tm, d), lambda i: (i, 0)), pl.BlockSpec((1, d), lambda i: (0, 0))],
        out_specs=pl.BlockSpec((tm, d), lambda i: (i, 0)),
        out_shape=jax.ShapeDtypeStruct((n, d), F32),
        compiler_params=_params("parallel"), name="final_norm",
    )(x2, g.reshape(1, d))


def _mm_kernel(a_ref, w_ref, o_ref):
    o_ref[...] = jnp.dot(a_ref[...], w_ref[...].astype(BF16), preferred_element_type=F32).astype(o_ref.dtype)


def _mm_colscale_kernel(a_ref, w_ref, s_ref, o_ref):
    w = (w_ref[...] * s_ref[...]).astype(BF16)
    o_ref[...] = jnp.dot(a_ref[...], w, preferred_element_type=F32).astype(o_ref.dtype)


def matmul(a, w, out_dtype, tm=1024, tn=512, n_out=None, col_scale=None):
    m, k = a.shape
    n = w.shape[1] if n_out is None else n_out
    tm, tn = min(tm, m), min(tn, n)
    in_specs = [pl.BlockSpec((tm, k), lambda i, j: (i, 0)),
                pl.BlockSpec((k, tn), lambda i, j: (0, j))]
    args = (a, w)
    if col_scale is not None:
        in_specs.append(pl.BlockSpec((1, tn), lambda i, j: (0, j)))
        args += (col_scale.reshape(1, -1),)
    return pl.pallas_call(
        _mm_kernel if col_scale is None else _mm_colscale_kernel,
        grid=(m // tm, n // tn),
        in_specs=in_specs,
        out_specs=pl.BlockSpec((tm, tn), lambda i, j: (i, j)),
        out_shape=jax.ShapeDtypeStruct((m, n), out_dtype),
        compiler_params=_params("parallel", "parallel"), name="projection",
    )(*args)


def _mm_res_kernel(a_ref, w_ref, r_ref, g_ref, o_ref):
    acc = jnp.dot(a_ref[...], w_ref[...].astype(BF16), preferred_element_type=F32)
    o_ref[...] = r_ref[...] + g_ref[...] * acc


def matmul_residual(a, w, res, gate, seq, tm=1024, tn=512):
    m, k = a.shape
    n = w.shape[1]
    tm, tn = min(tm, seq), min(tn, n)
    per_batch = seq // tm
    return pl.pallas_call(
        _mm_res_kernel,
        grid=(m // tm, n // tn),
        in_specs=[pl.BlockSpec((tm, k), lambda i, j: (i, 0)),
                  pl.BlockSpec((k, tn), lambda i, j: (0, j)),
                  pl.BlockSpec((tm, tn), lambda i, j: (i, j)),
                  pl.BlockSpec((None, 1, tn), lambda i, j: (i // per_batch, 0, j))],
        out_specs=pl.BlockSpec((tm, tn), lambda i, j: (i, j)),
        out_shape=jax.ShapeDtypeStruct((m, n), F32),
        compiler_params=_params("parallel", "parallel"), name="projection_residual",
    )(a, w, res, gate)


def _gelu(x):
    return 0.5 * x * (1.0 + lax.erf(x * (2.0 ** -0.5)))


def _gmlp_kernel(u_ref, v_ref, lng_ref, lnb_ref, ws_ref, bs_ref, o_ref, *, groups):
    u = _gelu(u_ref[...].astype(F32))
    v = _gelu(v_ref[...].astype(F32))
    mu = jnp.mean(v, axis=-1, keepdims=True)
    var = jnp.mean(jnp.square(v - mu), axis=-1, keepdims=True)
    vn = ((v - mu) * lax.rsqrt(var + EPS) * lng_ref[...] + lnb_ref[...]).astype(BF16)
    t = ws_ref.shape[1]
    gw = u.shape[1] // groups
    row = lax.broadcasted_iota(I32, (t, t), 0)
    col = lax.broadcasted_iota(I32, (t, t), 1)
    causal = col <= row
    for g in range(groups):
        w = jnp.where(causal, ws_ref[g], 0.0).astype(BF16)
        mixed = jnp.dot(w, vn[:, g * gw:(g + 1) * gw], preferred_element_type=F32) + bs_ref[:, g:g + 1]
        o_ref[:, g * gw:(g + 1) * gw] = (u[:, g * gw:(g + 1) * gw] * mixed).astype(o_ref.dtype)
    o_ref[:, u.shape[1]:] = jnp.zeros((t, o_ref.shape[1] - u.shape[1]), o_ref.dtype)


def gmlp_mixer(proj, a_width, ln_g, ln_b, w_s, b_s, out_width):
    n = proj.shape[0]
    groups, t, _ = w_s.shape
    return pl.pallas_call(
        functools.partial(_gmlp_kernel, groups=groups),
        grid=(n // t,),
        in_specs=[pl.BlockSpec((t, a_width), lambda i: (i, 0)),
                  pl.BlockSpec((t, a_width), lambda i: (i, 1)),
                  pl.BlockSpec((1, a_width), lambda i: (0, 0)),
                  pl.BlockSpec((1, a_width), lambda i: (0, 0)),
                  pl.BlockSpec((groups, t, t), lambda i: (0, 0, 0)),
                  pl.BlockSpec((t, groups), lambda i: (0, 0))],
        out_specs=pl.BlockSpec((t, out_width), lambda i: (i, 0)),
        out_shape=jax.ShapeDtypeStruct((n, out_width), BF16),
        compiler_params=_params("parallel"), name="gmlp_mixer",
    )(proj, proj, ln_g.reshape(1, a_width), ln_b.reshape(1, a_width), w_s, b_s.T)


def _rope_kernel(q_ref, k_ref, x_ref, cb_ref, sb_ref, ci_ref, si_ref, ck_ref, sk_ref,
                 qo_ref, ko_ref, qio_ref, kia_ref, kib_ref, wo_ref, *, heads, idx_pairs, q_scale, w_scale):
    cb, sb = cb_ref[...], sb_ref[...]
    half = HEAD_DIM // 2
    for h in range(heads):
        x = q_ref[:, h * HEAD_DIM:(h + 1) * HEAD_DIM].astype(F32)
        r = x * cb + pltpu.roll(x, half, 1) * sb
        qo_ref[:, h * HEAD_DIM:(h + 1) * HEAD_DIM] = (r * q_scale).astype(qo_ref.dtype)
    x = k_ref[...].astype(F32)
    ko_ref[...] = (x * cb + pltpu.roll(x, half, 1) * sb).astype(ko_ref.dtype)

    lane = lax.broadcasted_iota(I32, cb.shape, 1)
    first_half = (lane % IDX_DIM) < (IDX_DIM // 2)

    def rope_idx(x, c, s):
        rot = jnp.where(first_half, pltpu.roll(x, LANES - IDX_DIM // 2, 1), pltpu.roll(x, IDX_DIM // 2, 1))
        return x * c + rot * s

    ci, si = ci_ref[...], si_ref[...]
    for p in range(idx_pairs):
        x = x_ref[:, p * LANES:(p + 1) * LANES]
        qio_ref[:, p * LANES:(p + 1) * LANES] = rope_idx(x, ci, si).astype(qio_ref.dtype)
    tail = x_ref[:, idx_pairs * LANES:(idx_pairs + 1) * LANES]
    roped = rope_idx(tail, ck_ref[...], sk_ref[...])
    is_k = lane < IDX_DIM
    ka = jnp.where(is_k, roped, 0.0)
    kia_ref[...] = ka.astype(kia_ref.dtype)
    kib_ref[...] = pltpu.roll(ka, IDX_DIM, 1).astype(kib_ref.dtype)
    w = pltpu.roll(tail, LANES - IDX_DIM, 1)
    wo_ref[...] = jnp.where(lane < IDX_HEADS, w * w_scale, 0.0)


def rope_prepare(proj, proj_idx, tables, q_off, heads):
    n = proj.shape[0]
    tm = min(256, n)
    idx_pairs = IDX_HEADS * IDX_DIM // LANES
    qb = q_off // (heads * HEAD_DIM)
    kb = (q_off + heads * HEAD_DIM) // HEAD_DIM
    row = lambda w: pl.BlockSpec((tm, w), lambda i: (i, 0))
    kernel = functools.partial(_rope_kernel, heads=heads, idx_pairs=idx_pairs, q_scale=HEAD_DIM ** -0.5,
                               w_scale=IDX_HEADS ** -0.5 * IDX_DIM ** -0.5)
    return pl.pallas_call(
        kernel,
        grid=(n // tm,),
        in_specs=[pl.BlockSpec((tm, heads * HEAD_DIM), lambda i: (i, qb)),
                  pl.BlockSpec((tm, HEAD_DIM), lambda i: (i, kb)),
                  row(proj_idx.shape[1])] + [row(LANES)] * 6,
        out_specs=[row(heads * HEAD_DIM), row(HEAD_DIM), row(idx_pairs * LANES), row(LANES), row(LANES), row(LANES)],
        out_shape=[jax.ShapeDtypeStruct((n, heads * HEAD_DIM), BF16),
                   jax.ShapeDtypeStruct((n, HEAD_DIM), BF16),
                   jax.ShapeDtypeStruct((n, idx_pairs * LANES), BF16),
                   jax.ShapeDtypeStruct((n, LANES), BF16),
                   jax.ShapeDtypeStruct((n, LANES), BF16),
                   jax.ShapeDtypeStruct((n, LANES), F32)],
        compiler_params=_params("parallel"), name="rope_prepare",
    )(proj, proj, proj_idx, *tables)


def rope_tables(positions):
    pos = positions.reshape(-1).astype(F32)[:, None]

    def cs(dim):
        inv = ROPE_THETA ** (-jnp.arange(0, dim, 2, dtype=F32) / dim)
        ang = pos * inv
        return jnp.cos(ang), jnp.sin(ang)

    cb, sb = cs(HEAD_DIM)
    ci, si = cs(IDX_DIM)
    ones, zeros = jnp.ones_like(cb), jnp.zeros_like(cb)
    return (jnp.concatenate([cb, cb], 1), jnp.concatenate([-sb, sb], 1),
            jnp.concatenate([ci, ci, ci, ci], 1), jnp.concatenate([-si, si, -si, si], 1),
            jnp.concatenate([ci, ci, ones], 1), jnp.concatenate([-si, si, zeros], 1))


DSA_ATTN_TILE = 512


def _sort_key(x):
    x = jnp.where(x == 0.0, 0.0, x)
    bits = pltpu.bitcast(x, I32)
    return jnp.where(bits < 0, bits ^ 0x7FFFFFFF, bits)


def _dsa_kernel(qi_ref, w_ref, kia_ref, kib_ref, q_ref, k_ref, v_ref, buf_ref, o_ref, key_ref,
                *, tq, tk, heads, n_sel):
    del buf_ref
    qt = pl.program_id(1)
    q_lo = qt * tq
    n_kb = (q_lo + tq + tk - 1) // tk
    sub = tk // LANES
    row_pos = q_lo + lax.broadcasted_iota(I32, (tq, LANES), 0)
    lane = lax.broadcasted_iota(I32, (tq, LANES), 1)

    w = w_ref[...]
    w_cols = [jnp.broadcast_to(w[:, h:h + 1], (tq, tk)) for h in range(IDX_HEADS)]

    def score_block(kb, carry):
        k0 = pl.multiple_of(kb * tk, tk)
        ka = kia_ref[pl.ds(k0, tk), :]
        kb_ = kib_ref[pl.ds(k0, tk), :]
        acc = jnp.zeros((tq, tk), F32)
        for p in range(IDX_HEADS // 2):
            x = qi_ref[:, p * LANES:(p + 1) * LANES]
            ra = lax.dot_general(x, ka, (((1,), (1,)), ((), ())), preferred_element_type=F32)
            rb = lax.dot_general(x, kb_, (((1,), (1,)), ((), ())), preferred_element_type=F32)
            acc = acc + w_cols[2 * p] * jnp.maximum(ra, 0.0)
            acc = acc + w_cols[2 * p + 1] * jnp.maximum(rb, 0.0)
        key = _sort_key(acc)
        for j in range(sub):
            col_pos = k0 + j * LANES + lane
            key_ref[kb * sub + j] = jnp.where(col_pos <= row_pos, key[:, j * LANES:(j + 1) * LANES], INT_MIN)
        return carry

    lax.fori_loop(0, n_kb, score_block, 0)
    n_slabs = n_kb * sub

    def count(pred):
        def body(kb, acc):
            hits = [jnp.where(pred(key_ref[kb * sub + j], kb * sub + j), 1, 0) for j in range(sub)]
            while len(hits) > 1:
                hits = [a + b for a, b in zip(hits[::2], hits[1::2])]
            return acc + hits[0]
        part = lax.fori_loop(0, n_kb, body, jnp.zeros((tq, LANES), I32))
        return jnp.sum(part.astype(F32), axis=1, keepdims=True).astype(I32)

    def search_bit(i, u):
        cand = u | (1 << (31 - i))
        thr = jnp.broadcast_to(cand ^ INT_MIN, (tq, LANES))
        total = count(lambda kv, c: kv >= thr)
        return jnp.where(total >= n_sel, cand, u)

    u = lax.fori_loop(0, 32, search_bit, jnp.zeros((tq, 1), I32))
    thr = u ^ INT_MIN
    thr_b = jnp.broadcast_to(thr, (tq, LANES))
    n_gt = count(lambda kv, c: kv > thr_b)
    n_ge = count(lambda kv, c: kv >= thr_b)
    need = n_sel - n_gt
    tie_rows = jnp.logical_and(n_ge > n_sel, need > 0)

    seq_bits = max(1, (key_ref.shape[0] * LANES).bit_length())

    def tie_search():
        need_b = need

        def bit_step(i, j):
            cand = j | (1 << (seq_bits - 1 - i))
            cand_b = jnp.broadcast_to(cand, (tq, LANES))
            total = count(lambda kv, c: jnp.logical_and(kv == thr_b, c * LANES + lane < cand_b))
            return jnp.where(total <= need_b, cand, j)

        return lax.fori_loop(0, seq_bits, bit_step, jnp.zeros((tq, 1), I32))

    any_tie = jnp.max(jnp.where(tie_rows, 1.0, 0.0)) > 0.0
    bound = lax.cond(any_tie, tie_search, lambda: jnp.full((tq, 1), 2 ** 30, I32))
    bound = jnp.where(tie_rows, bound, 2 ** 30)
    bound_b = jnp.broadcast_to(bound, (tq, LANES))

    q_all = jnp.concatenate([q_ref[:, h * HEAD_DIM:(h + 1) * HEAD_DIM] for h in range(heads)], axis=0)

    ta = min(tk, DSA_ATTN_TILE)
    sub_a = ta // LANES

    def attn_block(kb, carry):
        m, l, acc = carry
        k0 = pl.multiple_of(kb * ta, ta)
        kk = k_ref[pl.ds(k0, ta), :]
        vv = v_ref[pl.ds(k0, ta), :]
        s = lax.dot_general(q_all, kk, (((1,), (1,)), ((), ())), preferred_element_type=F32)
        bias_cols = []
        for j in range(sub_a):
            kv = key_ref[kb * sub_a + j]
            col_pos = k0 + j * LANES + lane
            take = jnp.logical_or(kv > thr_b, jnp.logical_and(kv == thr_b, col_pos < bound_b))
            take = jnp.logical_and(take, col_pos <= row_pos)
            bias_cols.append(jnp.where(take, 0.0, MASKED))
        bias = jnp.concatenate(bias_cols, axis=1)
        s = s.reshape(heads, tq, ta) + bias[None]
        m_new = jnp.maximum(m, jnp.max(s, axis=-1, keepdims=True))
        alpha = jnp.exp(m - m_new)
        p = jnp.exp((s - m_new).astype(BF16))
        l = alpha * l + jnp.sum(p.astype(F32), axis=-1, keepdims=True)
        pv = jnp.dot(p.reshape(heads * tq, ta), vv, preferred_element_type=F32)
        acc = alpha * acc + pv.reshape(heads, tq, HEAD_DIM)
        return m_new, l, acc

    init = (jnp.full((heads, tq, 1), MASKED, F32), jnp.zeros((heads, tq, 1), F32),
            jnp.zeros((heads, tq, HEAD_DIM), F32))
    m, l, acc = lax.fori_loop(0, n_kb * (tk // ta), attn_block, init)
    out = acc / l
    for h in range(heads):
        o_ref[:, h * HEAD_DIM:(h + 1) * HEAD_DIM] = out[h].astype(o_ref.dtype)


def dsa_mixer(qi, w_idx, kia, kib, q, k, proj, v_col_block, buf, seq, n_sel):
    n = q.shape[0]
    bsz = n // seq
    heads = q.shape[1] // HEAD_DIM
    tq = min(128, seq)
    tk = min(512, seq)
    nq = seq // tq
    kernel = functools.partial(_dsa_kernel, tq=tq, tk=tk, heads=heads, n_sel=n_sel)
    qrow = lambda w: pl.BlockSpec((tq, w), lambda b, i: (b * nq + i, 0))
    kv = lambda c: pl.BlockSpec((seq, LANES), lambda b, i: (b, c))
    return pl.pallas_call(
        kernel,
        grid=(bsz, nq),
        in_specs=[qrow(qi.shape[1]), qrow(LANES), kv(0), kv(0), qrow(q.shape[1]), kv(0), kv(v_col_block),
                  pl.BlockSpec(memory_space=pl.ANY)],
        out_specs=pl.BlockSpec((tq, heads * HEAD_DIM), lambda b, i: (b * nq + i, 1)),
        out_shape=jax.ShapeDtypeStruct(buf.shape, buf.dtype),
        scratch_shapes=[pltpu.VMEM((seq // LANES, tq, LANES), I32)],
        input_output_aliases={7: 0},
        compiler_params=_params("parallel", "arbitrary"), name="dsa_mixer",
    )(qi, w_idx, kia, kib, q, k, proj, buf)


SB_HEADS_PER_STEP = 4
SB_QUERY_TILE = 512
SB_KEY_TILE = 256
SB_LOGIT_SCALE = HEAD_DIM ** -0.5 * LOG2E


def _sb_kernel(q_ref, k_ref, v_ref, o_ref, *, tq, tk, group):
    qt = pl.program_id(2)
    ratio = tq // tk
    row = lax.broadcasted_iota(I32, (tk, tk), 0)
    col = lax.broadcasted_iota(I32, (tk, tk), 1)
    later = jnp.where(row > col, 1.0, 0.0).astype(BF16)
    q_pos = qt * tq + lax.broadcasted_iota(I32, (tq, tk), 0)
    k_off = lax.broadcasted_iota(I32, (tq, tk), 1)

    def block(kb, masked, carry, r0=0):
        k0 = pl.multiple_of(kb * tk, tk)
        heads = range(group)
        cols = [slice(h * HEAD_DIM, (h + 1) * HEAD_DIM) for h in heads]
        z = [lax.dot_general(q_ref[r0:, cols[h]], k_ref[pl.ds(k0, tk), cols[h]], (((1,), (1,)), ((), ())),
                             preferred_element_type=F32).astype(BF16) for h in heads]
        log_beta = [jnp.minimum(z[h], 0.0) - jnp.log(1.0 + jnp.exp2(-jnp.abs(z[h]))) * LOG2E for h in heads]
        log_fail = [log_beta[h] - z[h] for h in heads]
        if masked:
            strict = (k0 + k_off < q_pos)[r0:]
            log_fail = [jnp.where(strict, log_fail[h], 0.0) for h in heads]
        between = [jnp.dot(log_fail[h].astype(BF16), later, preferred_element_type=F32) + carry[h][1][r0:]
                   for h in heads]
        a = [jnp.exp2(log_beta[h] + between[h]) for h in heads]
        if masked:
            a = [jnp.where(strict, a[h], 0.0) for h in heads]
        acc = [carry[h][0][r0:] + jnp.dot(a[h].astype(BF16), v_ref[pl.ds(k0, tk), cols[h]],
                                          preferred_element_type=F32) for h in heads]
        run = [between[h][:, 0:1] + log_fail[h][:, 0:1] for h in heads]
        if r0:
            acc = [jnp.concatenate([carry[h][0][:r0], acc[h]], axis=0) for h in heads]
            run = [jnp.concatenate([carry[h][1][:r0], run[h]], axis=0) for h in heads]
        return tuple((acc[h], run[h]) for h in heads)

    carry = tuple((jnp.zeros((tq, HEAD_DIM), F32), jnp.zeros((tq, 1), F32)) for _ in range(group))
    first_full = qt * ratio
    for j in reversed(range(ratio)):
        carry = block(first_full + j, True, carry, r0=j * tk)
    carry = lax.fori_loop(0, first_full, lambda i, c: block(first_full - 1 - i, False, c), carry)
    for h in range(group):
        o_ref[:, h * HEAD_DIM:(h + 1) * HEAD_DIM] = carry[h][0].astype(o_ref.dtype)


def stick_breaking_mixer(qkv, seq, heads):
    n = qkv.shape[0]
    bsz = n // seq
    tq = min(SB_QUERY_TILE, seq)
    tk = min(SB_KEY_TILE, seq)
    nq = seq // tq
    group = SB_HEADS_PER_STEP
    hg = heads // group
    width = group * HEAD_DIM
    kernel = functools.partial(_sb_kernel, tq=tq, tk=tk, group=group)
    return pl.pallas_call(
        kernel,
        grid=(bsz, hg, nq),
        in_specs=[pl.BlockSpec((tq, width), lambda b, h, i: (b * nq + i, h)),
                  pl.BlockSpec((seq, width), lambda b, h, i: (b, hg + h)),
                  pl.BlockSpec((seq, width), lambda b, h, i: (b, 2 * hg + h))],
        out_specs=pl.BlockSpec((tq, width), lambda b, h, i: (b * nq + i, h)),
        out_shape=jax.ShapeDtypeStruct((n, heads * HEAD_DIM), BF16),
        compiler_params=_params("parallel", "parallel", "arbitrary"), name="stick_breaking",
    )(qkv, qkv, qkv)


def _router_kernel(h_ref, wr_ref, b_ref, s_ref, w_ref, c_ref, cnt_ref, *, n_experts):
    per_group = n_experts // N_GROUPS
    tm = h_ref.shape[0]
    logits = lax.dot_general(wr_ref[...], h_ref[...], (((1,), (1,)), ((), ())), preferred_element_type=F32)
    scores = jax.nn.sigmoid(logits)
    choice = (scores + b_ref[...]).reshape(N_GROUPS, per_group, tm)
    s3 = scores.reshape(N_GROUPS, per_group, tm)
    neg = -jnp.inf
    in_group = lax.broadcasted_iota(I32, choice.shape, 1)
    m1 = jnp.max(choice, axis=1, keepdims=True)
    first = jnp.min(jnp.where(choice == m1, in_group, per_group), axis=1, keepdims=True)
    m2 = jnp.max(jnp.where(in_group == first, neg, choice), axis=1, keepdims=True)
    group_score = m1 + m2
    gid = lax.broadcasted_iota(I32, group_score.shape, 0)
    group_sel = jnp.zeros(group_score.shape, jnp.bool_)
    for _ in range(TOPK_GROUPS):
        m = jnp.max(group_score, axis=0, keepdims=True)
        f = jnp.min(jnp.where(group_score == m, gid, N_GROUPS), axis=0, keepdims=True)
        hit = gid == f
        group_sel = jnp.logical_or(group_sel, hit)
        group_score = jnp.where(hit, neg, group_score)
    cand = jnp.where(group_sel, choice, neg)
    eid = lax.broadcasted_iota(I32, choice.shape, 0) * per_group + in_group
    sel = jnp.zeros(choice.shape, jnp.bool_)
    hits = []
    for _ in range(TOP_K):
        m = jnp.max(jnp.max(cand, axis=1, keepdims=True), axis=0, keepdims=True)
        f = jnp.min(jnp.min(jnp.where(cand == m, eid, n_experts), axis=1, keepdims=True), axis=0, keepdims=True)
        hit = eid == f
        hits.append((hit, f))
        sel = jnp.logical_or(sel, hit)
        cand = jnp.where(hit, neg, cand)
    top_w = jnp.where(sel, s3, 0.0)
    total = jnp.sum(jnp.sum(top_w, axis=1, keepdims=True), axis=0, keepdims=True)
    gates = top_w / total * ROUTED_SCALE

    @pl.when(pl.program_id(0) == 0)
    def _():
        cnt_ref[...] = jnp.zeros_like(cnt_ref)

    sel_f = jnp.where(sel, 1.0, 0.0).reshape(n_experts, tm)
    row = lax.broadcasted_iota(I32, (tm, tm), 0)
    col = lax.broadcasted_iota(I32, (tm, tm), 1)
    before = jnp.where(row < col, 1.0, 0.0).astype(BF16)
    prefix = jnp.dot(sel_f.astype(BF16), before, preferred_element_type=F32)
    rank_all = (prefix + cnt_ref[:, 0:1]).reshape(choice.shape)

    def pick(hit, val):
        return jnp.sum(jnp.sum(jnp.where(hit, val, 0.0), axis=1, keepdims=True), axis=0, keepdims=True)

    for k, (hit, f) in enumerate(hits):
        rank = pick(hit, rank_all).reshape(1, tm).astype(I32)
        s_ref[k:k + 1, :] = rank | lax.shift_left(f.reshape(1, tm), SLOT_RANK_BITS)
        w_ref[k:k + 1, :] = pick(hit, gates).reshape(1, tm)
    cnt_ref[...] = cnt_ref[...] + jnp.sum(sel_f, axis=1, keepdims=True)
    c_ref[...] = cnt_ref[...].astype(I32)


def moe_router(h, w_router, bias):
    n, d = h.shape
    e = w_router.shape[1]
    tm = min(512, n)
    top = lambda dt: jax.ShapeDtypeStruct((TOP_K, n), dt)
    top_spec = pl.BlockSpec((TOP_K, tm), lambda i: (0, i))
    slots, wts, counts = pl.pallas_call(
        functools.partial(_router_kernel, n_experts=e),
        grid=(n // tm,),
        in_specs=[pl.BlockSpec((tm, d), lambda i: (i, 0)),
                  pl.BlockSpec((e, d), lambda i: (0, 0)),
                  pl.BlockSpec((e, 1), lambda i: (0, 0))],
        out_specs=[top_spec, top_spec, pl.BlockSpec((e, LANES), lambda i: (0, 0))],
        out_shape=[top(I32), top(F32), jax.ShapeDtypeStruct((e, LANES), I32)],
        scratch_shapes=[pltpu.VMEM((e, LANES), F32)],
        compiler_params=_params("arbitrary"), name="moe_router",
    )(h, w_router.T.astype(BF16), bias.reshape(e, 1))
    return slots.reshape(-1), wts, counts[:, 0]


MOE_TILE = 512
DISPATCH_TOKENS = 256
COMBINE_TOKENS = 128
DMA_ISSUE_UNROLL = 16
PAD_CHUNK = 512


def _pack_pairs(x):
    w = x.shape[1] // 2
    lo = pltpu.bitcast(x[:, :w].astype(BF16).astype(F32), I32)
    hi = pltpu.bitcast(x[:, w:].astype(BF16).astype(F32), I32)
    return lax.shift_right_logical(lo, 16) | hi


def _unpack_pairs(p):
    lo = pltpu.bitcast(lax.shift_left(p, 16), F32)
    hi = pltpu.bitcast(p & jnp.int32(-65536), F32)
    return lo, hi


def _store_slabs(ref, x):
    r, n_slabs, _ = ref.shape
    flat = ref.reshape(r * n_slabs, LANES)
    for s in range(n_slabs):
        flat[pl.ds(s, r, stride=n_slabs), :] = x[:, s * LANES:(s + 1) * LANES]


def _load_slab_column(ref, s):
    r, n_slabs, _ = ref.shape
    return ref.reshape(r * n_slabs, LANES)[pl.ds(s, r, stride=n_slabs), :]


def _load_slabs(ref):
    return jnp.concatenate([_load_slab_column(ref, s) for s in range(ref.shape[1])], axis=1)


SLOT_RANK_BITS = 20


def _dispatch_kernel(row_ref, off_ref, cnt_ref, hp_ref, xs_ref, zero_ref, sem, pad_sem, *, td, n_tok, n_rows):
    base = pl.program_id(0) * td

    @pl.when(pl.program_id(0) == 0)
    def _():
        zero_ref[...] = jnp.zeros(zero_ref.shape, zero_ref.dtype)
        n_experts = cnt_ref.shape[0]

        chunk = zero_ref.shape[0]
        bits = chunk.bit_length() - 1

        def zero_copy(row, size):
            return pltpu.make_async_copy(zero_ref.at[pl.ds(0, size)], xs_ref.at[pl.ds(row, size)], pad_sem)

        def full_chunk(c, row):
            zero_copy(row, chunk).start()
            zero_copy(row, chunk).wait()
            return row + chunk

        def pad_expert(e, carry):
            start = off_ref[e] + cnt_ref[e]
            end = jnp.where(e + 1 < n_experts, off_ref[jnp.minimum(e + 1, n_experts - 1)], n_rows)
            row = lax.fori_loop(0, (end - start) // chunk, full_chunk, start)
            rem = (end - start) % chunk
            pieces = [(rem & (1 << b), row + lax.shift_left(lax.shift_right_logical(rem, b + 1), b + 1), 1 << b)
                      for b in reversed(range(bits))]
            for go, at, size in pieces:
                @pl.when(go != 0)
                def _(at=at, size=size):
                    zero_copy(at, size).start()
            for go, at, size in pieces:
                @pl.when(go != 0)
                def _(at=at, size=size):
                    zero_copy(at, size).wait()
            return carry

        lax.fori_loop(0, n_experts, pad_expert, 0)

    for k in range(TOP_K):
        def issue(t, carry, k=k):
            pltpu.make_async_copy(hp_ref.at[t], xs_ref.at[row_ref[k * n_tok + base + t]], sem).start()
            return carry

        lax.fori_loop(0, td, issue, 0, unroll=DMA_ISSUE_UNROLL)
    for k in range(TOP_K):
        pltpu.make_async_copy(hp_ref, xs_ref.at[pl.ds(0, td)], sem).wait()


def moe_dispatch(hp, slots, offsets, counts, n_rows):
    n, s, _ = hp.shape
    td = min(DISPATCH_TOKENS, n)
    kernel = functools.partial(_dispatch_kernel, td=td, n_tok=n, n_rows=n_rows)
    return pl.pallas_call(
        kernel,
        grid_spec=pltpu.PrefetchScalarGridSpec(
            num_scalar_prefetch=3,
            grid=(n // td,),
            in_specs=[pl.BlockSpec((td, s, LANES), lambda i, *_: (i, 0, 0))],
            out_specs=pl.BlockSpec(memory_space=pl.ANY),
            scratch_shapes=[pltpu.VMEM((PAD_CHUNK, s, LANES), I32), pltpu.SemaphoreType.DMA(()),
                            pltpu.SemaphoreType.DMA(())]),
        out_shape=jax.ShapeDtypeStruct((n_rows, s, LANES), I32),
        compiler_params=_params("arbitrary"), name="moe_dispatch",
    )(slots, offsets, counts, hp)


def _swiglu(lo, hi, wg_ref, wu_ref, wd_ref):
    w = lo.shape[1]
    g = (jnp.dot(lo, wg_ref[:w, :].astype(BF16), preferred_element_type=F32)
         + jnp.dot(hi, wg_ref[w:, :].astype(BF16), preferred_element_type=F32))
    u = (jnp.dot(lo, wu_ref[:w, :].astype(BF16), preferred_element_type=F32)
         + jnp.dot(hi, wu_ref[w:, :].astype(BF16), preferred_element_type=F32))
    act = g * jax.nn.sigmoid(g) * u
    return jnp.dot(act.astype(BF16), wd_ref[...].astype(BF16), preferred_element_type=F32)


def _grouped_kernel(te_ref, nu_ref, x_ref, wg_ref, wu_ref, wd_ref, y_ref, wg_bf, wu_bf, wd_bf):
    i = pl.program_id(0)

    @pl.when(i < nu_ref[0])
    def _():
        @pl.when(jnp.logical_or(i == 0, te_ref[i] != te_ref[jnp.maximum(i - 1, 0)]))
        def _():
            wg_bf[...] = wg_ref[...].astype(BF16)
            wu_bf[...] = wu_ref[...].astype(BF16)
            wd_bf[...] = wd_ref[...].astype(BF16)

        lo, hi = _unpack_pairs(_load_slabs(x_ref))
        _store_slabs(y_ref, _pack_pairs(_swiglu(lo.astype(BF16), hi.astype(BF16), wg_bf, wu_bf, wd_bf)))


def moe_grouped(xs, tile_expert, n_used, wg, wu, wd, layer):
    p, s, _ = xs.shape
    _, e, d, f = wg.shape
    tm = MOE_TILE
    row = lambda i, te, nu: (jnp.minimum(i, nu[0] - 1), 0, 0)
    expert = lambda i, te, nu: (layer, te[i], 0, 0)
    return pl.pallas_call(
        _grouped_kernel,
        grid_spec=pltpu.PrefetchScalarGridSpec(
            num_scalar_prefetch=2,
            grid=(p // tm,),
            in_specs=[pl.BlockSpec((tm, s, LANES), row),
                      pl.BlockSpec((None, None, d, f), expert),
                      pl.BlockSpec((None, None, d, f), expert),
                      pl.BlockSpec((None, None, f, d), expert)],
            out_specs=pl.BlockSpec((tm, s, LANES), row),
            scratch_shapes=[pltpu.VMEM((d, f), BF16), pltpu.VMEM((d, f), BF16), pltpu.VMEM((f, d), BF16)]),
        out_shape=jax.ShapeDtypeStruct((p, s, LANES), I32),
        compiler_params=_params("arbitrary"), name="moe_grouped",
    )(tile_expert, n_used, xs, wg, wu, wd)


def _shared_kernel(h_ref, wg_ref, wu_ref, wd_ref, o_ref):
    x = h_ref[...]
    w = x.shape[1] // 2
    o_ref[...] = _swiglu(x[:, :w], x[:, w:], wg_ref, wu_ref, wd_ref).astype(o_ref.dtype)


def shared_expert(h, wg, wu, wd):
    n, d = h.shape
    f = wg.shape[1]
    tm = min(512, n)
    return pl.pallas_call(
        _shared_kernel,
        grid=(n // tm,),
        in_specs=[pl.BlockSpec((tm, d), lambda i: (i, 0)),
                  pl.BlockSpec((d, f), lambda i: (0, 0)),
                  pl.BlockSpec((d, f), lambda i: (0, 0)),
                  pl.BlockSpec((f, d), lambda i: (0, 0))],
        out_specs=pl.BlockSpec((tm, d), lambda i: (i, 0)),
        out_shape=jax.ShapeDtypeStruct((n, d), BF16),
        compiler_params=_params("parallel"), name="shared_expert",
    )(h, wg, wu, wd)


def _combine_kernel(row_ref, ys_ref, w_ref, x_ref, sh_ref, g_ref, o_ref, rows_ref, sems, *, tc, n_tok, n_steps):
    step = pl.program_id(0)
    slot = step % 2

    def start_gather(which_step, which_slot):
        base = which_step * tc
        for k in range(TOP_K):
            def issue(t, carry, k=k):
                pltpu.make_async_copy(ys_ref.at[row_ref[k * n_tok + base + t]], rows_ref.at[which_slot, k, t],
                                      sems.at[which_slot]).start()
                return carry

            lax.fori_loop(0, tc, issue, 0, unroll=DMA_ISSUE_UNROLL)

    @pl.when(step == 0)
    def _():
        start_gather(0, 0)

    @pl.when(step + 1 < n_steps)
    def _():
        start_gather(step + 1, 1 - slot)

    for k in range(TOP_K):
        pltpu.make_async_copy(ys_ref.at[pl.ds(0, tc)], rows_ref.at[slot, k], sems.at[slot]).wait()

    half = x_ref.shape[1] // 2
    w_cols = [w_ref[:, k:k + 1] for k in range(TOP_K)]
    for s in range(rows_ref.shape[3]):
        acc_lo = jnp.zeros((tc, LANES), F32)
        acc_hi = jnp.zeros((tc, LANES), F32)
        for k in range(TOP_K):
            lo, hi = _unpack_pairs(_load_slab_column(rows_ref.at[slot, k], s))
            acc_lo = acc_lo + w_cols[k] * lo
            acc_hi = acc_hi + w_cols[k] * hi
        for acc, c0 in ((acc_lo, s * LANES), (acc_hi, half + s * LANES)):
            cols = slice(c0, c0 + LANES)
            o_ref[:, cols] = x_ref[:, cols] + g_ref[:, cols] * (acc + sh_ref[:, cols].astype(F32))


def moe_combine(ys, slots, wts, x2, shared, gate, seq):
    n, d = x2.shape
    s = ys.shape[1]
    tc = min(COMBINE_TOKENS, seq)
    per_batch = seq // tc
    kernel = functools.partial(_combine_kernel, tc=tc, n_tok=n, n_steps=n // tc)
    return pl.pallas_call(
        kernel,
        grid_spec=pltpu.PrefetchScalarGridSpec(
            num_scalar_prefetch=1,
            grid=(n // tc,),
            in_specs=[pl.BlockSpec(memory_space=pl.ANY),
                      pl.BlockSpec((tc, TOP_K), lambda i, *_: (i, 0)),
                      pl.BlockSpec((tc, d), lambda i, *_: (i, 0)),
                      pl.BlockSpec((tc, d), lambda i, *_: (i, 0)),
                      pl.BlockSpec((None, 1, d), lambda i, *_: (i // per_batch, 0, 0))],
            out_specs=pl.BlockSpec((tc, d), lambda i, *_: (i, 0)),
            scratch_shapes=[pltpu.VMEM((2, TOP_K, tc, s, LANES), I32), pltpu.SemaphoreType.DMA((2,))]),
        out_shape=jax.ShapeDtypeStruct((n, d), F32),
        compiler_params=_params("arbitrary"), name="moe_combine",
    )(slots, ys, wts, x2, shared, gate)


def moe_layout(slots, counts, n_tiles):
    e = counts.shape[0]
    tiles = (counts + MOE_TILE - 1) // MOE_TILE
    earlier = jnp.arange(e)[None, :] <= jnp.arange(e)[:, None]
    tile_end = jnp.sum(jnp.where(earlier, tiles[None, :], 0), axis=1)
    offsets = (tile_end - tiles) * MOE_TILE
    n_used = tile_end[-1]
    tile_ids = jnp.minimum(jnp.arange(n_tiles, dtype=I32), n_used - 1)
    tile_expert = jnp.sum(tile_end[None, :] <= tile_ids[:, None], axis=1)
    expert = lax.shift_right_logical(slots, SLOT_RANK_BITS)
    rows = (jnp.sum(jnp.where(expert[:, None] == jnp.arange(e)[None, :], offsets[None, :], 0), axis=1)
            + (slots & ((1 << SLOT_RANK_BITS) - 1)))
    return rows.astype(I32), offsets.astype(I32), tile_expert.astype(I32), n_used.reshape(1).astype(I32)


def _pad_cols(w, width):
    return jnp.pad(w, ((0, 0), (0, width - w.shape[1])))


def kernel(x, c, positions, ab_w_in, ab_w_out, gmlp_ln_g, gmlp_ln_b, gmlp_w_s, gmlp_b_s, sb_w_qkv, sb_w_out,
           norm_mix_g, ada_mix_w, ada_mix_b, norm_ffn_g, ada_ffn_w, ada_ffn_b, router_w, router_bias,
           expert_w_gate, expert_w_up, expert_w_down, shared_w_gate, shared_w_up, shared_w_down, final_norm_g):
    bsz, seq, d = x.shape
    depth = norm_mix_g.shape[0]
    a_width = gmlp_ln_g.shape[1]
    b_width = ab_w_out.shape[1] - a_width
    b_heads = b_width // HEAD_DIM
    c_heads = sb_w_out.shape[1] // HEAD_DIM
    n_sel = min(TOPK_MAX, seq // 4)
    main_width = 2 * a_width + b_width + 2 * HEAD_DIM
    idx_width = IDX_HEADS * IDX_DIM + LANES

    mod_mix = ada_modulation_all(c, ada_mix_w, ada_mix_b)
    mod_ffn = ada_modulation_all(c, ada_ffn_w, ada_ffn_b)
    tables = rope_tables(positions)
    x2 = x.reshape(bsz * seq, d)

    for layer in range(depth):
        j = layer // 2
        shift, scale, gate = mod_mix[layer, :, 0], mod_mix[layer, :, 1], mod_mix[layer, :, 2]
        h = norm_modulate(x2, norm_mix_g[layer], scale, shift, seq)
        if layer % 2 == 0:
            w_in = ab_w_in[j]
            proj = matmul(h, w_in, BF16, tn=640 if main_width % 640 == 0 else 128, n_out=main_width)
            proj_idx = matmul(h, _pad_cols(w_in[:, main_width:], idx_width).astype(BF16), F32, tn=idx_width)
            q, k, qi, kia, kib, w_idx = rope_prepare(proj, proj_idx, tables, 2 * a_width, b_heads)
            buf = gmlp_mixer(proj, a_width, gmlp_ln_g[j], gmlp_ln_b[j], gmlp_w_s[j], gmlp_b_s[j],
                             a_width + b_width)
            v_col_block = (2 * a_width + b_width + HEAD_DIM) // HEAD_DIM
            mixed = dsa_mixer(qi, w_idx, kia, kib, q, k, proj, v_col_block, buf, seq, n_sel)
            x2 = matmul_residual(mixed, ab_w_out[j], x2, gate, seq)
        else:
            c_width = c_heads * HEAD_DIM
            col_scale = jnp.where(jnp.arange(3 * c_width) < c_width, SB_LOGIT_SCALE, 1.0).astype(F32)
            qkv = matmul(h, sb_w_qkv[j], BF16, col_scale=col_scale)
            o = stick_breaking_mixer(qkv, seq, c_heads)
            x2 = matmul_residual(o, sb_w_out[j], x2, gate, seq)

        shift, scale, gate = mod_ffn[layer, :, 0], mod_ffn[layer, :, 1], mod_ffn[layer, :, 2]
        h, h_packed = norm_modulate(x2, norm_ffn_g[layer], scale, shift, seq, pack=True)
        slots, wts, counts = moe_router(h, router_w[layer], router_bias[layer])
        n_tiles = bsz * seq * TOP_K // MOE_TILE + router_w.shape[2]
        rows, offsets, tile_expert, n_used = moe_layout(slots, counts, n_tiles)
        xs = moe_dispatch(h_packed, rows, offsets, counts, n_tiles * MOE_TILE)
        ys = moe_grouped(xs, tile_expert, n_used, expert_w_gate, expert_w_up, expert_w_down, layer)
        shared = shared_expert(h, shared_w_gate[layer], shared_w_up[layer], shared_w_down[layer])
        x2 = moe_combine(ys, rows, wts.T, x2, shared, gate, seq)

    return final_norm(x2, final_norm_g).reshape(bsz, seq, d)
```
